```python
import math
import numpy as np
import jax
import jax.numpy as jnp
from jax import lax

D_MODEL = 1024
BATCH = 4
SEQ = 8192
DEPTH = 2

BRANCH_WIDTH = 512
N_BRANCH = 4
CONV_WIDTH = BRANCH_WIDTH
CONV_K = 3
RET_HEADS = 4
RET_DK = 64
RET_DV = 128
RET_CHUNK = 128
RET_THETA = 10000.0
SGU_GROUPS = 4
SGU_GROUP_DIM = BRANCH_WIDTH // SGU_GROUPS
SGU_CHUNK = 128
SGU_WIDTH = BRANCH_WIDTH
DIFF_HEADS = 4
DIFF_HEAD_DIM = 64
DIFF_V_DIM = 2 * DIFF_HEAD_DIM
Q_BLOCK = 128
ROPE_THETA = 500000.0
ROT_DIM = DIFF_HEAD_DIM // 4
CONV_COLS = 3 * CONV_WIDTH
RET_COLS = 2 * RET_HEADS * RET_DK + 2 * RET_HEADS * RET_DV
SGU_COLS = 2 * SGU_WIDTH
DIFF_COLS = 2 * DIFF_HEADS * 2 * DIFF_HEAD_DIM + DIFF_HEADS * DIFF_V_DIM
IN_COLS = CONV_COLS + RET_COLS + SGU_COLS + DIFF_COLS
N_EXPERTS = 16
N_GROUPS = 4
EXPERTS_PER_GROUP = N_EXPERTS // N_GROUPS
TOP_K = 2
D_EXPERT = 1024
MOE_BLOCK = 128
ALPHA = (2.0 * DEPTH) ** 0.25
BETA = (8.0 * DEPTH) ** -0.25
LN_EPS = 1e-5
RMS_EPS = 1e-6

kernel_name = "hybrid_gated_parallel_mixers_deepnorm_moe"


def layer_norm(x, g, b):
    xf = x.astype(jnp.float32)
    mu = jnp.mean(xf, axis=-1, keepdims=True)
    var = jnp.mean(jnp.square(xf - mu), axis=-1, keepdims=True)
    y = (xf - mu) * lax.rsqrt(var + LN_EPS) * g.astype(jnp.float32) + b.astype(jnp.float32)
    return y.astype(x.dtype)


def rms_norm(x):
    xf = x.astype(jnp.float32)
    return xf * lax.rsqrt(jnp.mean(jnp.square(xf), axis=-1, keepdims=True) + RMS_EPS)


def rotary(x, pos, rot_dim, theta):
    half = rot_dim // 2
    inv_freq = theta ** (-jnp.arange(half, dtype=jnp.float32) / half)
    ang = pos.astype(jnp.float32)[..., None] * inv_freq
    ang = ang.reshape(ang.shape[:2] + (1,) * (x.ndim - 3) + (half,))
    cos, sin = jnp.cos(ang), jnp.sin(ang)
    xr = x[..., :rot_dim].astype(jnp.float32)
    x1, x2 = xr[..., :half], xr[..., half:]
    rot = jnp.concatenate([x1 * cos - x2 * sin, x2 * cos + x1 * sin], axis=-1).astype(x.dtype)
    return jnp.concatenate([rot, x[..., rot_dim:]], axis=-1)


def short_conv_mixer(z, conv_w):
    b_gate, c_gate, u = jnp.split(z, 3, axis=-1)
    y = lax.conv_general_dilated(
        c_gate * u, conv_w.astype(z.dtype), window_strides=(1,), padding=[(CONV_K - 1, 0)],
        dimension_numbers=("NWC", "WIO", "NWC"), feature_group_count=CONV_WIDTH)
    return b_gate * y


def retention_mixer(z, pos):
    bsz, seq, _ = z.shape
    hk, hv = RET_HEADS * RET_DK, RET_HEADS * RET_DV
    q, k, v, g = jnp.split(z, [hk, 2 * hk, 2 * hk + hv], axis=-1)
    q = rotary(q.reshape(bsz, seq, RET_HEADS, RET_DK), pos, RET_DK, RET_THETA).astype(jnp.float32)
    k = rotary(k.reshape(bsz, seq, RET_HEADS, RET_DK), pos, RET_DK, RET_THETA).astype(jnp.float32)
    k = k * (RET_DK ** -0.5)
    v = v.reshape(bsz, seq, RET_HEADS, RET_DV).astype(jnp.float32)
    log_gamma = jnp.log1p(-jnp.exp2(-5.0 - jnp.arange(RET_HEADS, dtype=jnp.float32)))
    c, n = RET_CHUNK, seq // RET_CHUNK
    qc = q.reshape(bsz, n, c, RET_HEADS, RET_DK)
    kc = k.reshape(bsz, n, c, RET_HEADS, RET_DK)
    vc = v.reshape(bsz, n, c, RET_HEADS, RET_DV)
    idx = jnp.arange(c, dtype=jnp.float32)
    rel = idx[:, None] - idx[None, :]
    decay = jnp.where(rel >= 0, jnp.exp(jnp.maximum(rel, 0.0)[None] * log_gamma[:, None, None]), 0.0)
    scores = jnp.einsum("bnihd,bnjhd->bnhij", qc, kc) * decay
    inner = jnp.einsum("bnhij,bnjhe->bnihe", scores, vc)
    k_decay = jnp.exp((c - 1 - idx)[:, None] * log_gamma[None, :])
    kv = jnp.einsum("bnjhd,bnjhe->bnhde", kc * k_decay[:, :, None], vc)
    chunk_decay = jnp.exp(c * log_gamma)[None, :, None, None]

    def step(state, kv_n):
        return state * chunk_decay + kv_n, state

    _, prev = lax.scan(step, jnp.zeros((bsz, RET_HEADS, RET_DK, RET_DV), jnp.float32),
                       jnp.moveaxis(kv, 1, 0))
    prev = jnp.moveaxis(prev, 0, 1)
    q_decay = jnp.exp((idx + 1.0)[:, None] * log_gamma[None, :])
    cross = jnp.einsum("bnihd,bnhde->bnihe", qc * q_decay[:, :, None], prev)
    o = rms_norm((inner + cross).reshape(bsz, seq, RET_HEADS, RET_DV))
    gate = jax.nn.silu(g.reshape(bsz, seq, RET_HEADS, RET_DV).astype(jnp.float32))
    return (gate * o).reshape(bsz, seq, RET_HEADS * RET_DV).astype(z.dtype)


def spatial_gating_mixer(z, ln_g, ln_b, w_s, b_s):
    bsz, seq, _ = z.shape
    u, v = jnp.split(z, 2, axis=-1)
    v = layer_norm(v, ln_g, ln_b)
    n = seq // SGU_CHUNK
    v = v.reshape(bsz, n, SGU_CHUNK, SGU_GROUPS, SGU_GROUP_DIM)
    w = jnp.tril(w_s).astype(z.dtype)
    s = jnp.einsum("gts,bnsgc->bntgc", w, v) + b_s.T.astype(z.dtype)[None, None, :, :, None]
    return u * s.reshape(bsz, seq, SGU_WIDTH)


def diff_attention_mixer(z, pos, lq1, lk1, lq2, lk2, subln_g, lambda_init):
    bsz, seq, _ = z.shape
    hq = DIFF_HEADS * 2 * DIFF_HEAD_DIM
    q, k, v = jnp.split(z, [hq, 2 * hq], axis=-1)
    q = q.reshape(bsz, seq, DIFF_HEADS, 2, DIFF_HEAD_DIM)
    k = k.reshape(bsz, seq, DIFF_HEADS, 2, DIFF_HEAD_DIM)
    v = v.reshape(bsz, seq, DIFF_HEADS, DIFF_V_DIM)
    q = rotary(q, pos, ROT_DIM, ROPE_THETA) * (DIFF_HEAD_DIM ** -0.5)
    k = rotary(k, pos, ROT_DIM, ROPE_THETA)
    lam = (jnp.exp(jnp.sum(lq1.astype(jnp.float32) * lk1.astype(jnp.float32)))
           - jnp.exp(jnp.sum(lq2.astype(jnp.float32) * lk2.astype(jnp.float32))) + lambda_init)
    n = seq // Q_BLOCK
    qb = jnp.moveaxis(q.reshape(bsz, n, Q_BLOCK, DIFF_HEADS, 2, DIFF_HEAD_DIM), 1, 0)
    key_idx = jnp.arange(seq)

    def block(args):
        q_blk, blk = args
        s = jnp.einsum("bqhcd,bkhcd->bhcqk", q_blk, k).astype(jnp.float32)
        q_idx = blk * Q_BLOCK + jnp.arange(Q_BLOCK)
        s = jnp.where(key_idx[None, :] <= q_idx[:, None], s, -jnp.inf)
        p = jax.nn.softmax(s, axis=-1)
        a = p[:, :, 0] - lam * p[:, :, 1]
        return jnp.einsum("bhqk,bkhe->bqhe", a.astype(v.dtype), v)

    o = lax.map(block, (qb, jnp.arange(n)))
    o = jnp.moveaxis(o, 0, 1).reshape(bsz, seq, DIFF_HEADS, DIFF_V_DIM)
    o = rms_norm(o) * subln_g.astype(jnp.float32) * (1.0 - lambda_init)
    return o.reshape(bsz, seq, DIFF_HEADS * DIFF_V_DIM).astype(z.dtype)


def route(xf, w_router, b_router):
    scores = jax.nn.softmax((xf @ w_router).astype(jnp.float32), axis=-1)
    sel = scores + b_router.astype(jnp.float32)
    grp_score = jnp.sum(lax.top_k(sel.reshape(-1, N_GROUPS, EXPERTS_PER_GROUP), TOP_K)[0], axis=-1)
    best = jnp.argmax(grp_score, axis=-1)
    expert_group = jnp.arange(N_EXPERTS) // EXPERTS_PER_GROUP
    masked = jnp.where(expert_group[None, :] == best[:, None], sel, -jnp.inf)
    _, idx = lax.top_k(masked, TOP_K)
    w = jnp.take_along_axis(scores, idx, axis=-1)
    return idx, w / jnp.sum(w, axis=-1, keepdims=True)


def moe_ffn(h, w_router, b_router, w1, w3, w2):
    bsz, seq, d = h.shape
    xf = h.reshape(-1, d)
    t = xf.shape[0]
    idx, gw = route(xf, w_router, b_router)
    a = t * TOP_K
    flat_e = idx.reshape(-1)
    order = jnp.argsort(flat_e)
    sorted_e = flat_e[order]
    tok = order // TOP_K
    counts = jnp.bincount(flat_e, length=N_EXPERTS)
    padded = ((counts + MOE_BLOCK - 1) // MOE_BLOCK) * MOE_BLOCK
    pad_end = jnp.cumsum(padded)
    pad_start = pad_end - padded
    start = jnp.cumsum(counts) - counts
    dest = pad_start[sorted_e] + (jnp.arange(a) - start[sorted_e])
    n_rows = a + N_EXPERTS * MOE_BLOCK
    n_blk = n_rows // MOE_BLOCK
    buf = jnp.zeros((n_rows, d), xf.dtype).at[dest].set(xf[tok])
    blk_e = jnp.minimum(jnp.searchsorted(pad_end, jnp.arange(n_blk) * MOE_BLOCK, side="right"),
                        N_EXPERTS - 1)

    def expert_block(args):
        xb, e = args
        return (jax.nn.silu(xb @ w1[e]) * (xb @ w3[e])) @ w2[e]

    out = lax.map(expert_block, (buf.reshape(n_blk, MOE_BLOCK, d), blk_e)).reshape(n_rows, d)
    contrib = out[dest] * gw.reshape(-1)[order][:, None].astype(xf.dtype)
    y = jnp.zeros((t, d), xf.dtype).at[tok].add(contrib)
    return y.reshape(bsz, seq, d)


def setup_inputs(seed: int = 0) -> dict:
    key = jax.random.key(seed)
    ks = jax.random.split(key, 32)
    nrm = lambda k, shape, s: jax.random.normal(k, shape, jnp.float32) * s
    gain = lambda k, shape: 1.0 + 0.05 * jax.random.normal(k, shape, jnp.float32)
    L, D, W = DEPTH, D_MODEL, BRANCH_WIDTH
    positions = (jnp.arange(SEQ, dtype=jnp.int32)[None, :]
                 + jax.random.randint(ks[1], (BATCH, 1), 0, 4096, dtype=jnp.int32))
    return {
        "x": nrm(ks[0], (BATCH, SEQ, D), 1.0),
        "positions": positions,
        "ln_in_g": gain(ks[2], (D,)),
        "ln_in_b": nrm(ks[3], (D,), 0.05),
        "w_in": nrm(ks[4], (L, D, IN_COLS), D ** -0.5),
        "conv_w": nrm(ks[5], (L, CONV_K, 1, CONV_WIDTH), CONV_K ** -0.5),
        "sgu_ln_g": gain(ks[6], (L, SGU_WIDTH)),
        "sgu_ln_b": nrm(ks[7], (L, SGU_WIDTH), 0.05),
        "w_s": nrm(ks[8], (L, SGU_GROUPS, SGU_CHUNK, SGU_CHUNK), SGU_CHUNK ** -0.5),
        "b_s": gain(ks[9], (L, SGU_GROUPS, SGU_CHUNK)),
        "lambda_q1": nrm(ks[10], (L, DIFF_HEAD_DIM), 0.1),
        "lambda_k1": nrm(ks[11], (L, DIFF_HEAD_DIM), 0.1),
        "lambda_q2": nrm(ks[12], (L, DIFF_HEAD_DIM), 0.1),
        "lambda_k2": nrm(ks[13], (L, DIFF_HEAD_DIM), 0.1),
        "diff_subln_g": gain(ks[14], (L, DIFF_V_DIM)),
        "w_gate": nrm(ks[15], (L, N_BRANCH, D, D), D ** -0.5),
        "b_gate": nrm(ks[16], (L, N_BRANCH, D), 0.05),
        "w_branch": nrm(ks[17], (L, N_BRANCH, W, D), BETA * W ** -0.5),
        "w_o": nrm(ks[18], (L, D, D), BETA * D ** -0.5),
        "ln1_g": gain(ks[19], (L, D)),
        "ln1_b": nrm(ks[20], (L, D), 0.05),
        "w_router": nrm(ks[21], (D, N_EXPERTS), D ** -0.5),
        "b_router": nrm(ks[22], (N_EXPERTS,), 0.01),
        "w1": nrm(ks[23], (L, N_EXPERTS, D, D_EXPERT), D ** -0.5),
        "w3": nrm(ks[24], (L, N_EXPERTS, D, D_EXPERT), D ** -0.5),
        "w2": nrm(ks[25], (L, N_EXPERTS, D_EXPERT, D), BETA * D_EXPERT ** -0.5),
        "ln2_g": gain(ks[26], (L, D)),
        "ln2_b": nrm(ks[27], (L, D), 0.05),
    }


def reference(x, positions, ln_in_g, ln_in_b, w_in, conv_w, sgu_ln_g, sgu_ln_b, w_s, b_s,
              lambda_q1, lambda_k1, lambda_q2, lambda_k2, diff_subln_g, w_gate, b_gate,
              w_branch, w_o, ln1_g, ln1_b, w_router, b_router, w1, w3, w2, ln2_g, ln2_b):
    split_at = [CONV_COLS, CONV_COLS + RET_COLS, CONV_COLS + RET_COLS + SGU_COLS]
    h = layer_norm(x, ln_in_g, ln_in_b)
    for l in range(DEPTH):
        z = h @ w_in[l]
        z_conv, z_ret, z_sgu, z_diff = jnp.split(z, split_at, axis=-1)
        lambda_init = 0.8 - 0.6 * math.exp(-0.3 * l)
        branches = (
            short_conv_mixer(z_conv, conv_w[l]),
            retention_mixer(z_ret, positions),
            spatial_gating_mixer(z_sgu, sgu_ln_g[l], sgu_ln_b[l], w_s[l], b_s[l]),
            diff_attention_mixer(z_diff, positions, lambda_q1[l], lambda_k1[l], lambda_q2[l],
                                 lambda_k2[l], diff_subln_g[l], lambda_init),
        )
        merged = jnp.zeros_like(h)
        for g, y in enumerate(branches):
            gate = jax.nn.sigmoid(h @ w_gate[l, g] + b_gate[l, g])
            merged = merged + gate * (y @ w_branch[l, g])
        h = layer_norm(ALPHA * h + merged @ w_o[l], ln1_g[l], ln1_b[l])
        h = layer_norm(ALPHA * h + moe_ffn(h, w_router, b_router, w1[l], w3[l], w2[l]),
                       ln2_g[l], ln2_b[l])
    return h
```

```python
import functools
import math

import jax
import jax.numpy as jnp
from jax import lax
from jax.experimental import pallas as pl
from jax.experimental.pallas import tpu as pltpu

D_MODEL = 1024
BRANCH_WIDTH = 512
N_BRANCH = 4
CONV_WIDTH = BRANCH_WIDTH
CONV_K = 3
RET_HEADS = 4
RET_DK = 64
RET_DV = 128
RET_CHUNK = 128
RET_THETA = 10000.0
SGU_GROUPS = 4
SGU_GROUP_DIM = BRANCH_WIDTH // SGU_GROUPS
SGU_CHUNK = 128
SGU_WIDTH = BRANCH_WIDTH
DIFF_HEADS = 4
DIFF_HEAD_DIM = 64
DIFF_V_DIM = 2 * DIFF_HEAD_DIM
ROPE_THETA = 500000.0
ROT_DIM = DIFF_HEAD_DIM // 4
CONV_COLS = 3 * CONV_WIDTH
RET_COLS = 2 * RET_HEADS * RET_DK + 2 * RET_HEADS * RET_DV
SGU_COLS = 2 * SGU_WIDTH
DIFF_COLS = 2 * DIFF_HEADS * 2 * DIFF_HEAD_DIM + DIFF_HEADS * DIFF_V_DIM
IN_COLS = CONV_COLS + RET_COLS + SGU_COLS + DIFF_COLS
N_EXPERTS = 16
N_GROUPS = 4
EXPERTS_PER_GROUP = N_EXPERTS // N_GROUPS
TOP_K = 2
D_EXPERT = 1024
LN_EPS = 1e-5
RMS_EPS = 1e-6

LANES = 128
CONV_OFF = 0
RET_OFF = CONV_COLS
SGU_OFF = RET_OFF + RET_COLS
DIFF_OFF = SGU_OFF + SGU_COLS

NEG_BIG = -1e30
VMEM_LIMIT = 56 * 1024 * 1024

BF16 = jnp.bfloat16
F32 = jnp.float32


def _cparams(sem):
    return pltpu.CompilerParams(dimension_semantics=sem, vmem_limit_bytes=VMEM_LIMIT)


def _layer_norm(xf, g, b):
    mu = jnp.mean(xf, axis=-1, keepdims=True)
    xc = xf - mu
    var = jnp.mean(xc * xc, axis=-1, keepdims=True)
    return xc * lax.rsqrt(var + LN_EPS) * g + b


def _dot(a, b):
    return jnp.dot(a, b, preferred_element_type=F32)


def _dot_nt(a, b):
    return lax.dot_general(a, b, (((1,), (1,)), ((), ())), preferred_element_type=F32)


def _dot_tn(a, b):
    return lax.dot_general(a, b, (((0,), (0,)), ((), ())), preferred_element_type=F32)


def _ln_kernel(x_ref, g_ref, b_ref, o_ref):
    o_ref[...] = _layer_norm(x_ref[...], g_ref[...], b_ref[...])


def _ln_call(x2d, g, b, tm=1024):
    t, d = x2d.shape
    return pl.pallas_call(
        _ln_kernel,
        grid=(t // tm,),
        in_specs=[pl.BlockSpec((tm, d), lambda i: (i, 0)),
                  pl.BlockSpec((1, d), lambda i: (0, 0)),
                  pl.BlockSpec((1, d), lambda i: (0, 0))],
        out_specs=pl.BlockSpec((tm, d), lambda i: (i, 0)),
        out_shape=jax.ShapeDtypeStruct((t, d), F32),
        compiler_params=_cparams(("parallel",)),
        name="ln_in",
    )(x2d, g.reshape(1, d), b.reshape(1, d))


def _rope_kernel(pos_ref, freq_ref, sign_ref, cos_ref, sin_ref):
    ang = pos_ref[...] * freq_ref[...]
    cos_ref[...] = jnp.cos(ang)
    sin_ref[...] = jnp.sin(ang) * sign_ref[...]


def _rope_tables(positions):
    t = positions.size
    pos = positions.reshape(t, 1).astype(F32)
    half_r = RET_DK // 2
    fr = RET_THETA ** (-jnp.arange(half_r, dtype=F32) / half_r)
    half_d = ROT_DIM // 2
    fd = ROPE_THETA ** (-jnp.arange(half_d, dtype=F32) / half_d)
    zeros_d = jnp.zeros((DIFF_HEAD_DIM - ROT_DIM,), F32)
    freq = jnp.concatenate([fr, fr, fd, fd, zeros_d]).reshape(1, LANES)
    sign = jnp.concatenate([-jnp.ones((half_r,), F32), jnp.ones((half_r,), F32),
                            -jnp.ones((half_d,), F32), jnp.ones((half_d,), F32),
                            zeros_d]).reshape(1, LANES)
    tm = min(t, 2048)
    return pl.pallas_call(
        _rope_kernel,
        grid=(t // tm,),
        in_specs=[pl.BlockSpec((tm, 1), lambda i: (i, 0)),
                  pl.BlockSpec((1, LANES), lambda i: (0, 0)),
                  pl.BlockSpec((1, LANES), lambda i: (0, 0))],
        out_specs=[pl.BlockSpec((tm, LANES), lambda i: (i, 0)),
                   pl.BlockSpec((tm, LANES), lambda i: (i, 0))],
        out_shape=[jax.ShapeDtypeStruct((t, LANES), F32)] * 2,
        compiler_params=_cparams(("parallel",)),
        name="rope_tables",
    )(pos, freq, sign)


def _tile_lanes(x, reps):
    return jnp.concatenate([x] * reps, axis=1)


def _rotate_half_split(x, cos, sin_signed, group, half):
    w = x.shape[1]
    lane = lax.broadcasted_iota(jnp.int32, x.shape, 1) % group
    partner = jnp.where(lane < half, pltpu.roll(x, w - half, axis=1), pltpu.roll(x, half, axis=1))
    return x * cos + partner * sin_signed


def _proj_kernel(h_ref, w_ref, z_ref, *, chunk):
    hb = h_ref[...].astype(BF16)
    for n0 in range(0, IN_COLS, chunk):
        z_ref[:, n0:n0 + chunk] = _dot(hb, w_ref[:, n0:n0 + chunk]).astype(BF16)


def _proj_call(h, w_bf16, tm=512):
    t, d = h.shape
    return pl.pallas_call(
        functools.partial(_proj_kernel, chunk=512),
        grid=(t // tm,),
        in_specs=[pl.BlockSpec((tm, d), lambda i: (i, 0)),
                  pl.BlockSpec((d, IN_COLS), lambda i: (0, 0), pipeline_mode=pl.Buffered(1))],
        out_specs=pl.BlockSpec((tm, IN_COLS), lambda i: (i, 0)),
        out_shape=jax.ShapeDtypeStruct((t, IN_COLS), BF16),
        compiler_params=_cparams(("parallel",)),
        name="proj_in",
    )(h, w_bf16)


CONV_HALO = 16


def _conv_kernel(b_ref, c_ref, u_ref, ch_ref, uh_ref, w_ref, o_ref):
    ts = c_ref.shape[0]
    first = pl.program_id(1) == 0
    cu = c_ref[...].astype(F32) * u_ref[...].astype(F32)
    halo = ch_ref[...].astype(F32) * uh_ref[...].astype(F32)
    halo = jnp.where(first, 0.0, halo)
    ext = jnp.concatenate([halo, cu], axis=0)
    n = ext.shape[0]
    prev1 = pltpu.roll(ext, 1, axis=0)[CONV_HALO:n]
    prev2 = pltpu.roll(ext, 2, axis=0)[CONV_HALO:n]
    w = w_ref[...]
    y = prev2 * w[0:1, :] + prev1 * w[1:2, :] + cu * w[2:3, :]
    o_ref[...] = (b_ref[...].astype(F32) * y).astype(BF16)


def _conv_call(z, conv_w, bsz, seq, ts=1024):
    t = bsz * seq
    ts = min(ts, seq)
    nt = seq // ts
    wb = CONV_WIDTH
    hb = ts // CONV_HALO
    col = lambda k: (lambda b, i: (b * nt + i, CONV_OFF // wb + k))
    halo = lambda k: (lambda b, i: (jnp.maximum((b * nt + i) * hb - 1, 0), CONV_OFF // wb + k))
    return pl.pallas_call(
        _conv_kernel,
        grid=(bsz, nt),
        in_specs=[pl.BlockSpec((ts, wb), col(0)),
                  pl.BlockSpec((ts, wb), col(1)),
                  pl.BlockSpec((ts, wb), col(2)),
                  pl.BlockSpec((CONV_HALO, wb), halo(1)),
                  pl.BlockSpec((CONV_HALO, wb), halo(2)),
                  pl.BlockSpec((CONV_K, wb), lambda b, i: (0, 0))],
        out_specs=pl.BlockSpec((ts, wb), lambda b, i: (b * nt + i, 0)),
        out_shape=jax.ShapeDtypeStruct((t, wb), BF16),
        compiler_params=_cparams(("parallel", "parallel")),
        name="conv_mixer",
    )(z, z, z, z, z, conv_w.reshape(CONV_K, wb))


def _sgu_kernel(u_ref, v_ref, g_ref, b_ref, ws_ref, bias_ref, o_ref):
    ts = u_ref.shape[0]
    c = SGU_CHUNK
    v = _layer_norm(v_ref[...].astype(F32), g_ref[...], b_ref[...]).astype(BF16)
    row = lax.broadcasted_iota(jnp.int32, (c, c), 0)
    colm = lax.broadcasted_iota(jnp.int32, (c, c), 1)
    bias = bias_ref[...]
    for g in range(SGU_GROUPS):
        w = jnp.where(row >= colm, ws_ref[g], 0.0).astype(BF16)
        lo = g * SGU_GROUP_DIM
        for n in range(ts // c):
            s = _dot(w, v[n * c:(n + 1) * c, lo:lo + SGU_GROUP_DIM]) + bias[:, lo:lo + SGU_GROUP_DIM]
            u = u_ref[n * c:(n + 1) * c, lo:lo + SGU_GROUP_DIM].astype(F32)
            o_ref[n * c:(n + 1) * c, lo:lo + SGU_GROUP_DIM] = (u * s).astype(BF16)


def _sgu_call(z, ln_g, ln_b, w_s, b_s, ts=512):
    t = z.shape[0]
    ts = min(ts, t)
    wb = SGU_WIDTH
    bias = jnp.repeat(b_s.T, SGU_GROUP_DIM, axis=1)
    return pl.pallas_call(
        _sgu_kernel,
        grid=(t // ts,),
        in_specs=[pl.BlockSpec((ts, wb), lambda i: (i, SGU_OFF // wb)),
                  pl.BlockSpec((ts, wb), lambda i: (i, SGU_OFF // wb + 1)),
                  pl.BlockSpec((1, wb), lambda i: (0, 0)),
                  pl.BlockSpec((1, wb), lambda i: (0, 0)),
                  pl.BlockSpec((SGU_GROUPS, SGU_CHUNK, SGU_CHUNK), lambda i: (0, 0, 0)),
                  pl.BlockSpec((SGU_CHUNK, wb), lambda i: (0, 0))],
        out_specs=pl.BlockSpec((ts, wb), lambda i: (i, 0)),
        out_shape=jax.ShapeDtypeStruct((t, wb), BF16),
        compiler_params=_cparams(("parallel",)),
        name="sgu_mixer",
    )(z, z, ln_g.reshape(1, wb), ln_b.reshape(1, wb), w_s, bias)


def _ret_tables():
    c = RET_CHUNK
    log_gamma = jnp.log1p(-jnp.exp2(-5.0 - jnp.arange(RET_HEADS, dtype=F32)))
    idx = jnp.arange(c, dtype=F32)
    rel = idx[:, None] - idx[None, :]
    decay = jnp.where(rel >= 0, jnp.exp(jnp.maximum(rel, 0.0)[None] * log_gamma[:, None, None]), 0.0)
    k_decay = jnp.exp((c - 1 - idx)[:, None] * log_gamma[None, :])
    q_decay = jnp.exp((idx + 1.0)[:, None] * log_gamma[None, :])
    chunk_decay = jnp.exp(c * log_gamma)
    hk = RET_HEADS * RET_DK
    kd = jnp.repeat(k_decay, RET_DK, axis=1) * (RET_DK ** -0.5)
    qd = jnp.repeat(q_decay, RET_DK, axis=1)
    cd = jnp.broadcast_to(jnp.repeat(chunk_decay, RET_DV)[None, :], (8, RET_HEADS * RET_DV))
    del hk
    return decay, kd, qd, cd


def _ret_kernel(q_ref, k_ref, v_ref, g_ref, cos_ref, sin_ref, decay_ref, kd_ref, qd_ref, cd_ref,
                o_ref, state_ref):
    ts = q_ref.shape[0]
    c = RET_CHUNK
    hk = RET_HEADS * RET_DK

    @pl.when(pl.program_id(1) == 0)
    def _():
        state_ref[...] = jnp.zeros_like(state_ref)

    cos = _tile_lanes(cos_ref[:, 0:RET_DK], RET_HEADS)
    sin = _tile_lanes(sin_ref[:, 0:RET_DK], RET_HEADS)
    q = _rotate_half_split(q_ref[...].astype(F32), cos, sin, RET_DK, RET_DK // 2)
    k = _rotate_half_split(k_ref[...].astype(F32), cos, sin, RET_DK, RET_DK // 2)
    del hk
    for n in range(ts // c):
        r0 = n * c
        qn = q[r0:r0 + c]
        kn = k[r0:r0 + c]
        qb = qn.astype(BF16)
        kb = (kn * (RET_DK ** -0.5)).astype(BF16)
        qdb = (qn * qd_ref[...]).astype(BF16)
        kdb = (kn * kd_ref[...]).astype(BF16)
        for h in range(RET_HEADS):
            ks = slice(h * RET_DK, (h + 1) * RET_DK)
            vs = slice(h * RET_DV, (h + 1) * RET_DV)
            vb = v_ref[r0:r0 + c, vs]
            scores = _dot_nt(qb[:, ks], kb[:, ks]) * decay_ref[h]
            inner = _dot(scores.astype(BF16), vb)
            state = state_ref[h]
            cross = _dot(qdb[:, ks], state.astype(BF16))
            kv = _dot_tn(kdb[:, ks], vb)
            state_ref[h] = state * cd_ref[0:1, vs] + kv
            o = inner + cross
            o = o * lax.rsqrt(jnp.mean(o * o, axis=-1, keepdims=True) + RMS_EPS)
            gate = g_ref[r0:r0 + c, vs].astype(F32)
            gate = gate * jax.nn.sigmoid(gate)
            o_ref[r0:r0 + c, vs] = (gate * o).astype(BF16)


def _ret_call(z, cos_t, sin_t, bsz, seq, ts=512):
    t = bsz * seq
    ts = min(ts, seq)
    nt = seq // ts
    decay, kd, qd, cd = _ret_tables()
    hk = RET_HEADS * RET_DK
    hv = RET_HEADS * RET_DV
    row = lambda b, i: b * nt + i
    return pl.pallas_call(
        _ret_kernel,
        grid=(bsz, nt),
        in_specs=[pl.BlockSpec((ts, hk), lambda b, i: (row(b, i), RET_OFF // hk)),
                  pl.BlockSpec((ts, hk), lambda b, i: (row(b, i), RET_OFF // hk + 1)),
                  pl.BlockSpec((ts, hv), lambda b, i: (row(b, i), (RET_OFF + 2 * hk) // hv)),
                  pl.BlockSpec((ts, hv), lambda b, i: (row(b, i), (RET_OFF + 2 * hk) // hv + 1)),
                  pl.BlockSpec((ts, LANES), lambda b, i: (row(b, i), 0)),
                  pl.BlockSpec((ts, LANES), lambda b, i: (row(b, i), 0)),
                  pl.BlockSpec((RET_HEADS, RET_CHUNK, RET_CHUNK), lambda b, i: (0, 0, 0)),
                  pl.BlockSpec((RET_CHUNK, hk), lambda b, i: (0, 0)),
                  pl.BlockSpec((RET_CHUNK, hk), lambda b, i: (0, 0)),
                  pl.BlockSpec((8, hv), lambda b, i: (0, 0))],
        out_specs=pl.BlockSpec((ts, hv), lambda b, i: (row(b, i), 0)),
        out_shape=jax.ShapeDtypeStruct((t, hv), BF16),
        scratch_shapes=[pltpu.VMEM((RET_HEADS, RET_DK, RET_DV), F32)],
        compiler_params=_cparams(("parallel", "arbitrary")),
        name="ret_mixer",
    )(z, z, z, z, cos_t, sin_t, decay, kd, qd, cd)


def _qkprep_kernel(q_ref, k_ref, cos_ref, sin_ref, qo_ref, ko_ref):
    reps = q_ref.shape[1] // DIFF_HEAD_DIM
    cos = _tile_lanes(cos_ref[:, DIFF_HEAD_DIM:2 * DIFF_HEAD_DIM], reps)
    sin = _tile_lanes(sin_ref[:, DIFF_HEAD_DIM:2 * DIFF_HEAD_DIM], reps)
    q = _rotate_half_split(q_ref[...].astype(F32), cos, sin, DIFF_HEAD_DIM, ROT_DIM // 2)
    k = _rotate_half_split(k_ref[...].astype(F32), cos, sin, DIFF_HEAD_DIM, ROT_DIM // 2)
    qo_ref[...] = (q * (DIFF_HEAD_DIM ** -0.5)).astype(BF16)
    ko_ref[...] = k.astype(BF16)


def _qkprep_call(z, cos_t, sin_t, tm=1024):
    t = z.shape[0]
    tm = min(tm, t)
    hq = DIFF_HEADS * 2 * DIFF_HEAD_DIM
    return pl.pallas_call(
        _qkprep_kernel,
        grid=(t // tm,),
        in_specs=[pl.BlockSpec((tm, hq), lambda i: (i, DIFF_OFF // hq)),
                  pl.BlockSpec((tm, hq), lambda i: (i, DIFF_OFF // hq + 1)),
                  pl.BlockSpec((tm, LANES), lambda i: (i, 0)),
                  pl.BlockSpec((tm, LANES), lambda i: (i, 0))],
        out_specs=[pl.BlockSpec((tm, hq), lambda i: (i, 0)),
                   pl.BlockSpec((tm, hq), lambda i: (i, 0))],
        out_shape=[jax.ShapeDtypeStruct((t, hq), BF16)] * 2,
        compiler_params=_cparams(("parallel",)),
        name="diff_qkprep",
    )(z, z, cos_t, sin_t)


def _flash_kernel(qi_ref, ki_ref, q_ref, k_ref, v_ref, lam_ref, g_ref, o_ref,
                  lhs_ref, m_ref, l_ref, acc_ref, *, tq, tk, lambda_init):
    p = pl.program_id(2)
    qi = qi_ref[p]
    ki = ki_ref[p]
    last_k = ((qi + 1) * tq - 1) // tk

    @pl.when(ki == 0)
    def _():
        q = q_ref[...]
        lane = lax.broadcasted_iota(jnp.int32, q.shape, 1)
        zero = jnp.zeros_like(q)
        lhs_ref[0:tq, :] = jnp.where(lane < DIFF_HEAD_DIM, q, zero)
        lhs_ref[tq:2 * tq, :] = jnp.where(lane >= DIFF_HEAD_DIM, q, zero)
        m_ref[...] = jnp.full_like(m_ref, NEG_BIG)
        l_ref[...] = jnp.zeros_like(l_ref)
        acc_ref[...] = jnp.zeros_like(acc_ref)

    def step(masked):
        s = _dot_nt(lhs_ref[...], k_ref[...])
        if masked:
            row = lax.broadcasted_iota(jnp.int32, (tq, tk), 0) + qi * tq
            col = lax.broadcasted_iota(jnp.int32, (tq, tk), 1) + ki * tk
            keep = col <= row
            keep = jnp.concatenate([keep, keep], axis=0)
            s = jnp.where(keep, s, NEG_BIG)
        m_prev = m_ref[...]
        m_new = jnp.maximum(m_prev, jnp.max(s, axis=-1, keepdims=True))
        alpha = jnp.exp(m_prev - m_new)
        pexp = jnp.exp(s - m_new)
        l_ref[...] = alpha * l_ref[...] + jnp.sum(pexp, axis=-1, keepdims=True)
        acc_ref[...] = alpha * acc_ref[...] + _dot(pexp.astype(BF16), v_ref[...])
        m_ref[...] = m_new

    needs_mask = (ki + 1) * tk - 1 > qi * tq

    @pl.when(needs_mask)
    def _():
        step(True)

    @pl.when(jnp.logical_not(needs_mask))
    def _():
        step(False)

    @pl.when(ki == last_k)
    def _():
        lam_p = lam_ref[...]
        lam = (jnp.exp(jnp.sum(lam_p[0:1] * lam_p[1:2], axis=-1, keepdims=True))
               - jnp.exp(jnp.sum(lam_p[2:3] * lam_p[3:4], axis=-1, keepdims=True)) + lambda_init)
        o = acc_ref[...] / l_ref[...]
        o = o[0:tq] - lam * o[tq:2 * tq]
        o = o * lax.rsqrt(jnp.mean(o * o, axis=-1, keepdims=True) + RMS_EPS)
        o_ref[...] = (o * g_ref[...] * (1.0 - lambda_init)).astype(BF16)


def _flash_call(qr, kr, z, lam_params, subln_g, bsz, seq, lambda_init, tq=512, tk=512):
    t = bsz * seq
    tq = min(tq, seq)
    tk = min(tk, seq)
    nq = seq // tq
    nk = seq // tk
    qi_list, ki_list = [], []
    for qi in range(nq):
        for ki in range(((qi + 1) * tq - 1) // tk + 1):
            qi_list.append(qi)
            ki_list.append(ki)
    qi_arr = jnp.asarray(qi_list, jnp.int32)
    ki_arr = jnp.asarray(ki_list, jnp.int32)
    del nk
    hd = 2 * DIFF_HEAD_DIM
    v_off = (DIFF_OFF + 2 * DIFF_HEADS * hd) // DIFF_V_DIM
    grid_spec = pltpu.PrefetchScalarGridSpec(
        num_scalar_prefetch=2,
        grid=(bsz, DIFF_HEADS, len(qi_list)),
        in_specs=[pl.BlockSpec((tq, hd), lambda b, h, p, qi, ki: (b * (seq // tq) + qi[p], h)),
                  pl.BlockSpec((tk, hd), lambda b, h, p, qi, ki: (b * (seq // tk) + ki[p], h)),
                  pl.BlockSpec((tk, DIFF_V_DIM),
                               lambda b, h, p, qi, ki: (b * (seq // tk) + ki[p], v_off + h)),
                  pl.BlockSpec((4, DIFF_HEAD_DIM), lambda b, h, p, qi, ki: (0, 0)),
                  pl.BlockSpec((1, DIFF_V_DIM), lambda b, h, p, qi, ki: (0, 0))],
        out_specs=pl.BlockSpec((tq, DIFF_V_DIM), lambda b, h, p, qi, ki: (b * (seq // tq) + qi[p], h)),
        scratch_shapes=[pltpu.VMEM((2 * tq, hd), BF16),
                        pltpu.VMEM((2 * tq, 1), F32),
                        pltpu.VMEM((2 * tq, 1), F32),
                        pltpu.VMEM((2 * tq, DIFF_V_DIM), F32)],
    )
    return pl.pallas_call(
        functools.partial(_flash_kernel, tq=tq, tk=tk, lambda_init=lambda_init),
        grid_spec=grid_spec,
        out_shape=jax.ShapeDtypeStruct((t, DIFF_HEADS * DIFF_V_DIM), BF16),
        compiler_params=_cparams(("parallel", "parallel", "arbitrary")),
        name="diff_flash",
    )(qi_arr, ki_arr, qr, kr, z, lam_params, subln_g.reshape(1, DIFF_V_DIM))


def _merge_kernel(h_ref, y0_ref, y1_ref, y2_ref, y3_ref, wg_ref, bg_ref, wb_ref, wo_ref,
                  g_ref, b_ref, o_ref, *, alpha):
    h = h_ref[...]
    hb = h.astype(BF16)
    merged = None
    for g, y_ref in enumerate((y0_ref, y1_ref, y2_ref, y3_ref)):
        gate = jax.nn.sigmoid(_dot(hb, wg_ref[g]) + bg_ref[g:g + 1, :])
        term = gate * _dot(y_ref[...], wb_ref[g])
        merged = term if merged is None else merged + term
    t = _dot(merged.astype(BF16), wo_ref[...])
    o_ref[...] = _layer_norm(alpha * h + t, g_ref[...], b_ref[...])


def _merge_call(h, ys, wg, bg, wb, wo, ln_g, ln_b, alpha, tm=512):
    t, d = h.shape
    tm = min(tm, t)
    w = BRANCH_WIDTH
    const = dict(pipeline_mode=pl.Buffered(1))
    return pl.pallas_call(
        functools.partial(_merge_kernel, alpha=alpha),
        grid=(t // tm,),
        in_specs=[pl.BlockSpec((tm, d), lambda i: (i, 0))]
        + [pl.BlockSpec((tm, w), lambda i: (i, 0))] * N_BRANCH
        + [pl.BlockSpec((N_BRANCH, d, d), lambda i: (0, 0, 0), **const),
           pl.BlockSpec((N_BRANCH, d), lambda i: (0, 0)),
           pl.BlockSpec((N_BRANCH, w, d), lambda i: (0, 0, 0), **const),
           pl.BlockSpec((d, d), lambda i: (0, 0), **const),
           pl.BlockSpec((1, d), lambda i: (0, 0)),
           pl.BlockSpec((1, d), lambda i: (0, 0))],
        out_specs=pl.BlockSpec((tm, d), lambda i: (i, 0)),
        out_shape=jax.ShapeDtypeStruct((t, d), F32),
        compiler_params=_cparams(("parallel",)),
        name="merge",
    )(h, *ys, wg, bg, wb, wo, ln_g.reshape(1, d), ln_b.reshape(1, d))


def _route_kernel(h_ref, wr_ref, br_ref, idx_ref, gw_ref):
    logits = lax.dot_general(wr_ref[...], h_ref[...], (((1,), (1,)), ((), ())),
                             preferred_element_type=F32, precision=lax.Precision.HIGHEST)
    mx = jnp.max(logits, axis=0, keepdims=True)
    e = jnp.exp(logits - mx)
    scores = e / jnp.sum(e, axis=0, keepdims=True)
    sel = scores + br_ref[...]
    epg = EXPERTS_PER_GROUP
    rows = [sel[i:i + 1, :] for i in range(N_EXPERTS)]
    srows = [scores[i:i + 1, :] for i in range(N_EXPERTS)]
    best_score = None
    best = None
    for g in range(N_GROUPS):
        r = rows[g * epg:(g + 1) * epg]
        gs = None
        for a in range(epg):
            for b in range(a + 1, epg):
                pair = r[a] + r[b]
                gs = pair if gs is None else jnp.maximum(gs, pair)
        if best is None:
            best_score, best = gs, jnp.zeros(gs.shape, jnp.int32)
        else:
            take = gs > best_score
            best_score = jnp.where(take, gs, best_score)
            best = jnp.where(take, g, best)
    cand, cscore = [], []
    for j in range(epg):
        c = rows[j]
        s = srows[j]
        for g in range(1, N_GROUPS):
            c = jnp.where(best == g, rows[g * epg + j], c)
            s = jnp.where(best == g, srows[g * epg + j], s)
        cand.append(c)
        cscore.append(s)
    v1, i1, s1 = cand[0], jnp.zeros(best.shape, jnp.int32), cscore[0]
    for j in range(1, epg):
        take = cand[j] > v1
        v1 = jnp.where(take, cand[j], v1)
        i1 = jnp.where(take, j, i1)
        s1 = jnp.where(take, cscore[j], s1)
    v2 = jnp.full(v1.shape, -jnp.inf, F32)
    i2 = jnp.zeros(best.shape, jnp.int32)
    s2 = jnp.zeros(v1.shape, F32)
    for j in range(epg):
        take = jnp.logical_and(i1 != j, cand[j] > v2)
        v2 = jnp.where(take, cand[j], v2)
        i2 = jnp.where(take, j, i2)
        s2 = jnp.where(take, cscore[j], s2)
    denom = s1 + s2
    idx_ref[0:1, :] = best * epg + i1
    idx_ref[1:2, :] = best * epg + i2
    gw_ref[0:1, :] = s1 / denom
    gw_ref[1:2, :] = s2 / denom


def _route_call(h, w_router, b_router, tm=1024):
    t, d = h.shape
    tm = min(tm, t)
    return pl.pallas_call(
        _route_kernel,
        grid=(t // tm,),
        in_specs=[pl.BlockSpec((tm, d), lambda i: (i, 0)),
                  pl.BlockSpec((N_EXPERTS, d), lambda i: (0, 0)),
                  pl.BlockSpec((N_EXPERTS, 1), lambda i: (0, 0))],
        out_specs=[pl.BlockSpec((TOP_K, tm), lambda i: (0, i)),
                   pl.BlockSpec((TOP_K, tm), lambda i: (0, i))],
        out_shape=[jax.ShapeDtypeStruct((TOP_K, t), jnp.int32),
                   jax.ShapeDtypeStruct((TOP_K, t), F32)],
        compiler_params=_cparams(("parallel",)),
        name="route",
    )(h, w_router.T, b_router.reshape(N_EXPERTS, 1))


def _rank_kernel(idx_ref, dest_ref, cnt_ref, run_ref, start_ref, *, blk):
    phase = pl.program_id(0)
    i = pl.program_id(1)
    tm = idx_ref.shape[1]
    expert = lax.broadcasted_iota(jnp.int32, (N_EXPERTS, tm), 0)
    oh0 = expert == idx_ref[0:1, :]
    oh1 = expert == idx_ref[1:2, :]
    f0 = jnp.where(oh0, 1.0, 0.0)
    f1 = jnp.where(oh1, 1.0, 0.0)

    @pl.when(jnp.logical_and(phase == 0, i == 0))
    def _():
        run_ref[...] = jnp.zeros_like(run_ref)

    @pl.when(phase == 0)
    def _():
        tot = jnp.sum(f0 + f1, axis=1, keepdims=True)
        run_ref[...] = run_ref[...] + tot

    @pl.when(jnp.logical_and(phase == 1, i == 0))
    def _():
        cnt = run_ref[...]
        cnt_ref[...] = cnt.astype(jnp.int32)
        padded = jnp.ceil(cnt * (1.0 / blk)) * blk
        acc = jnp.zeros((1, LANES), F32)
        for e in range(N_EXPERTS):
            start_ref[e:e + 1, :] = acc
            acc = acc + padded[e:e + 1, :]
        run_ref[...] = jnp.zeros_like(run_ref)

    @pl.when(phase == 1)
    def _():
        s_idx = lax.broadcasted_iota(jnp.int32, (tm, tm), 0)
        t_idx = lax.broadcasted_iota(jnp.int32, (tm, tm), 1)
        tri = jnp.where(s_idx < t_idx, 1.0, 0.0).astype(BF16)
        c0 = _dot(f0.astype(BF16), tri)
        c1 = _dot(f1.astype(BF16), tri)
        tot0 = jnp.sum(f0, axis=1, keepdims=True)
        tot1 = jnp.sum(f1, axis=1, keepdims=True)
        base = start_ref[:, 0:1] + run_ref[:, 0:1]
        d0 = jnp.sum(jnp.where(oh0, base + c0, 0.0), axis=0, keepdims=True)
        d1 = jnp.sum(jnp.where(oh1, base + tot0 + c1, 0.0), axis=0, keepdims=True)
        dest_ref[0:1, :] = d0.astype(jnp.int32)
        dest_ref[1:2, :] = d1.astype(jnp.int32)
        run_ref[...] = run_ref[...] + (tot0 + tot1)


def _rank_call(idx, blk, tm=512):
    t = idx.shape[1]
    tm = min(tm, t)
    return pl.pallas_call(
        functools.partial(_rank_kernel, blk=blk),
        grid=(2, t // tm),
        in_specs=[pl.BlockSpec((TOP_K, tm), lambda p, i: (0, i))],
        out_specs=[pl.BlockSpec((TOP_K, tm), lambda p, i: (0, i * p)),
                   pl.BlockSpec((N_EXPERTS, LANES), lambda p, i: (0, 0))],
        out_shape=[jax.ShapeDtypeStruct((TOP_K, t), jnp.int32),
                   jax.ShapeDtypeStruct((N_EXPERTS, LANES), jnp.int32)],
        scratch_shapes=[pltpu.VMEM((N_EXPERTS, LANES), F32),
                        pltpu.VMEM((N_EXPERTS, LANES), F32)],
        compiler_params=_cparams(("arbitrary", "arbitrary")),
        name="moe_rank",
    )(idx)


def _dispatch_kernel(dest_ref, x_ref, zeros_ref, xs_ref, sem):
    del zeros_ref
    i = pl.program_id(0)
    tm = x_ref.shape[0]
    t = pl.num_programs(0) * tm

    def issue(r, carry):
        tok = i * tm + r
        for k in range(TOP_K):
            d = dest_ref[k * t + tok]
            pltpu.make_async_copy(x_ref.at[pl.ds(r, 1), :], xs_ref.at[pl.ds(d, 1), :], sem).start()
        return carry

    lax.fori_loop(0, tm, issue, 0)

    def drain(r, carry):
        for k in range(TOP_K):
            pltpu.make_async_copy(x_ref.at[pl.ds(0, 1), :], xs_ref.at[pl.ds(0, 1), :], sem).wait()
        return carry

    lax.fori_loop(0, tm, drain, 0)


def _dispatch_call(h, dest_flat, n_rows, tm=512):
    t, d = h.shape
    tm = min(tm, t)
    zeros = jnp.zeros((n_rows, d), F32)
    grid_spec = pltpu.PrefetchScalarGridSpec(
        num_scalar_prefetch=1,
        grid=(t // tm,),
        in_specs=[pl.BlockSpec((tm, d), lambda i, dest: (i, 0)),
                  pl.BlockSpec(memory_space=pl.ANY)],
        out_specs=pl.BlockSpec(memory_space=pl.ANY),
        scratch_shapes=[pltpu.SemaphoreType.DMA(())],
    )
    return pl.pallas_call(
        _dispatch_kernel,
        grid_spec=grid_spec,
        out_shape=jax.ShapeDtypeStruct((n_rows, d), F32),
        input_output_aliases={2: 0},
        compiler_params=_cparams(("arbitrary",)),
        name="moe_dispatch",
    )(dest_flat, h, zeros)


def _experts_kernel(be_ref, nb_ref, x_ref, w1_ref, w3_ref, w2_ref, o_ref):
    i = pl.program_id(0)

    @pl.when(i < nb_ref[0])
    def _():
        xb = x_ref[...].astype(BF16)
        a = _dot(xb, w1_ref[0])
        b = _dot(xb, w3_ref[0])
        act = (a * jax.nn.sigmoid(a) * b).astype(BF16)
        o_ref[...] = _dot(act, w2_ref[0])

    @pl.when(i >= nb_ref[0])
    def _():
        o_ref[...] = jnp.zeros_like(o_ref)


def _experts_call(xs, blk_expert, n_used, w1, w3, w2, blk):
    n_rows, d = xs.shape
    de = w1.shape[2]
    wmap = lambda i, be, nb: (be[i], 0, 0)
    grid_spec = pltpu.PrefetchScalarGridSpec(
        num_scalar_prefetch=2,
        grid=(n_rows // blk,),
        in_specs=[pl.BlockSpec((blk, d), lambda i, be, nb: (i, 0)),
                  pl.BlockSpec((1, d, de), wmap),
                  pl.BlockSpec((1, d, de), wmap),
                  pl.BlockSpec((1, de, d), wmap)],
        out_specs=pl.BlockSpec((blk, d), lambda i, be, nb: (i, 0)),
    )
    return pl.pallas_call(
        _experts_kernel,
        grid_spec=grid_spec,
        out_shape=jax.ShapeDtypeStruct((n_rows, d), F32),
        compiler_params=_cparams(("arbitrary",)),
        name="moe_experts",
    )(blk_expert, n_used, xs, w1, w3, w2)


def _combine_kernel(dest_ref, h_ref, gw_ref, ys_ref, g_ref, b_ref, o_ref, buf_ref, sem, *, alpha):
    i = pl.program_id(0)
    tm = h_ref.shape[0]
    t = pl.num_programs(0) * tm

    def issue(r, carry):
        tok = i * tm + r
        for k in range(TOP_K):
            d = dest_ref[k * t + tok]
            pltpu.make_async_copy(ys_ref.at[pl.ds(d, 1), :], buf_ref.at[k, pl.ds(r, 1), :], sem).start()
        return carry

    lax.fori_loop(0, tm, issue, 0)

    def drain(r, carry):
        for k in range(TOP_K):
            pltpu.make_async_copy(ys_ref.at[pl.ds(0, 1), :], buf_ref.at[k, pl.ds(0, 1), :], sem).wait()
        return carry

    lax.fori_loop(0, tm, drain, 0)
    gw = gw_ref[...]
    y = buf_ref[0] * gw[:, 0:1] + buf_ref[1] * gw[:, 1:2]
    o_ref[...] = _layer_norm(alpha * h_ref[...] + y, g_ref[...], b_ref[...])


def _combine_call(h, gw_t, ys, dest_flat, ln_g, ln_b, alpha, tm=256):
    t, d = h.shape
    tm = min(tm, t)
    grid_spec = pltpu.PrefetchScalarGridSpec(
        num_scalar_prefetch=1,
        grid=(t // tm,),
        in_specs=[pl.BlockSpec((tm, d), lambda i, dest: (i, 0)),
                  pl.BlockSpec((tm, TOP_K), lambda i, dest: (i, 0)),
                  pl.BlockSpec(memory_space=pl.ANY),
                  pl.BlockSpec((1, d), lambda i, dest: (0, 0)),
                  pl.BlockSpec((1, d), lambda i, dest: (0, 0))],
        out_specs=pl.BlockSpec((tm, d), lambda i, dest: (i, 0)),
        scratch_shapes=[pltpu.VMEM((TOP_K, tm, d), F32),
                        pltpu.SemaphoreType.DMA(())],
    )
    return pl.pallas_call(
        functools.partial(_combine_kernel, alpha=alpha),
        grid_spec=grid_spec,
        out_shape=jax.ShapeDtypeStruct((t, d), F32),
        compiler_params=_cparams(("arbitrary",)),
        name="moe_combine",
    )(dest_flat, h, gw_t, ys, ln_g.reshape(1, d), ln_b.reshape(1, d))


def _moe_layer(h, w_router, b_router, w1, w3, w2, ln_g, ln_b, alpha, blk=256):
    t, d = h.shape
    idx, gw = _route_call(h, w_router, b_router)
    dest, counts = _rank_call(idx, blk)
    n_rows = TOP_K * t + N_EXPERTS * blk
    n_blk = n_rows // blk
    cnt = counts[:, 0]
    pad_end = jnp.cumsum(((cnt + blk - 1) // blk) * blk)
    blk_expert = jnp.minimum(
        jnp.sum(pad_end[None, :] <= (jnp.arange(n_blk, dtype=jnp.int32) * blk)[:, None], axis=1),
        N_EXPERTS - 1).astype(jnp.int32)
    n_used = (pad_end[N_EXPERTS - 1:] // blk).astype(jnp.int32)
    dest_flat = dest.reshape(-1)
    xs = _dispatch_call(h, dest_flat, n_rows)
    ys = _experts_call(xs, blk_expert, n_used, w1, w3, w2, blk)
    return _combine_call(h, gw.T, ys, dest_flat, ln_g, ln_b, alpha)


def kernel(x, positions, ln_in_g, ln_in_b, w_in, conv_w, sgu_ln_g, sgu_ln_b, w_s, b_s, lambda_q1, lambda_k1, lambda_q2, lambda_k2, diff_subln_g, w_gate, b_gate, w_branch, w_o, ln1_g, ln1_b, w_router, b_router, w1, w3, w2, ln2_g, ln2_b):
    bsz, seq, d = x.shape
    depth = w_in.shape[0]
    alpha = (2.0 * depth) ** 0.25
    t = bsz * seq
    cos_t, sin_t = _rope_tables(positions)
    h = _ln_call(x.reshape(t, d), ln_in_g, ln_in_b)
    for l in range(depth):
        lambda_init = 0.8 - 0.6 * math.exp(-0.3 * l)
        z = _proj_call(h, w_in[l].astype(BF16))
        y_conv = _conv_call(z, conv_w[l], bsz, seq)
        y_ret = _ret_call(z, cos_t, sin_t, bsz, seq)
        y_sgu = _sgu_call(z, sgu_ln_g[l], sgu_ln_b[l], w_s[l], b_s[l])
        qr, kr = _qkprep_call(z, cos_t, sin_t)
        lam_params = jnp.stack([lambda_q1[l], lambda_k1[l], lambda_q2[l], lambda_k2[l]])
        y_diff = _flash_call(qr, kr, z, lam_params, diff_subln_g[l], bsz, seq, lambda_init)
        h = _merge_call(h, (y_conv, y_ret, y_sgu, y_diff), w_gate[l].astype(BF16), b_gate[l],
                        w_branch[l].astype(BF16), w_o[l].astype(BF16), ln1_g[l], ln1_b[l], alpha)
        h = _moe_layer(h, w_router, b_router, w1[l].astype(BF16), w3[l].astype(BF16),
                       w2[l].astype(BF16), ln2_g[l], ln2_b[l], alpha)
    return h.reshape(bsz, seq, d)
```

```python
import functools
import math

import jax
import jax.numpy as jnp
from jax import lax
from jax.experimental import pallas as pl
from jax.experimental.pallas import tpu as pltpu

D_MODEL = 1024
BRANCH_WIDTH = 512
N_BRANCH = 4
CONV_WIDTH = BRANCH_WIDTH
CONV_K = 3
RET_HEADS = 4
RET_DK = 64
RET_DV = 128
RET_CHUNK = 128
RET_THETA = 10000.0
SGU_GROUPS = 4
SGU_GROUP_DIM = BRANCH_WIDTH // SGU_GROUPS
SGU_CHUNK = 128
SGU_WIDTH = BRANCH_WIDTH
DIFF_HEADS = 4
DIFF_HEAD_DIM = 64
DIFF_V_DIM = 2 * DIFF_HEAD_DIM
ROPE_THETA = 500000.0
ROT_DIM = DIFF_HEAD_DIM // 4
CONV_COLS = 3 * CONV_WIDTH
RET_COLS = 2 * RET_HEADS * RET_DK + 2 * RET_HEADS * RET_DV
SGU_COLS = 2 * SGU_WIDTH
DIFF_COLS = 2 * DIFF_HEADS * 2 * DIFF_HEAD_DIM + DIFF_HEADS * DIFF_V_DIM
IN_COLS = CONV_COLS + RET_COLS + SGU_COLS + DIFF_COLS
N_EXPERTS = 16
N_GROUPS = 4
EXPERTS_PER_GROUP = N_EXPERTS // N_GROUPS
TOP_K = 2
D_EXPERT = 1024
LN_EPS = 1e-5
RMS_EPS = 1e-6

LANES = 128
CONV_OFF = 0
RET_OFF = CONV_COLS
SGU_OFF = RET_OFF + RET_COLS
DIFF_OFF = SGU_OFF + SGU_COLS

NEG_BIG = -1e30
VMEM_LIMIT = 56 * 1024 * 1024

BF16 = jnp.bfloat16
F32 = jnp.float32


def _cparams(sem):
    return pltpu.CompilerParams(dimension_semantics=sem, vmem_limit_bytes=VMEM_LIMIT)


def _layer_norm(xf, g, b):
    mu = jnp.mean(xf, axis=-1, keepdims=True)
    xc = xf - mu
    var = jnp.mean(xc * xc, axis=-1, keepdims=True)
    return xc * lax.rsqrt(var + LN_EPS) * g + b


def _dot(a, b):
    return jnp.dot(a, b, preferred_element_type=F32)


def _dot_nt(a, b):
    return lax.dot_general(a, b, (((1,), (1,)), ((), ())), preferred_element_type=F32)


def _dot_tn(a, b):
    return lax.dot_general(a, b, (((0,), (0,)), ((), ())), preferred_element_type=F32)


def _ln_kernel(x_ref, g_ref, b_ref, o_ref):
    o_ref[...] = _layer_norm(x_ref[...], g_ref[...], b_ref[...])


def _ln_call(x2d, g, b, tm=1024):
    t, d = x2d.shape
    return pl.pallas_call(
        _ln_kernel,
        grid=(t // tm,),
        in_specs=[pl.BlockSpec((tm, d), lambda i: (i, 0)),
                  pl.BlockSpec((1, d), lambda i: (0, 0)),
                  pl.BlockSpec((1, d), lambda i: (0, 0))],
        out_specs=pl.BlockSpec((tm, d), lambda i: (i, 0)),
        out_shape=jax.ShapeDtypeStruct((t, d), F32),
        compiler_params=_cparams(("parallel",)),
        name="ln_in",
    )(x2d, g.reshape(1, d), b.reshape(1, d))


def _rope_kernel(pos_ref, freq_ref, sign_ref, cos_ref, sin_ref):
    ang = pos_ref[...] * freq_ref[...]
    cos_ref[...] = jnp.cos(ang)
    sin_ref[...] = jnp.sin(ang) * sign_ref[...]


def _rope_tables(positions):
    t = positions.size
    pos = positions.reshape(t, 1).astype(F32)
    half_r = RET_DK // 2
    fr = RET_THETA ** (-jnp.arange(half_r, dtype=F32) / half_r)
    half_d = ROT_DIM // 2
    fd = ROPE_THETA ** (-jnp.arange(half_d, dtype=F32) / half_d)
    zeros_d = jnp.zeros((DIFF_HEAD_DIM - ROT_DIM,), F32)
    freq = jnp.concatenate([fr, fr, fd, fd, zeros_d]).reshape(1, LANES)
    sign = jnp.concatenate([-jnp.ones((half_r,), F32), jnp.ones((half_r,), F32),
                            -jnp.ones((half_d,), F32), jnp.ones((half_d,), F32),
                            zeros_d]).reshape(1, LANES)
    tm = min(t, 2048)
    return pl.pallas_call(
        _rope_kernel,
        grid=(t // tm,),
        in_specs=[pl.BlockSpec((tm, 1), lambda i: (i, 0)),
                  pl.BlockSpec((1, LANES), lambda i: (0, 0)),
                  pl.BlockSpec((1, LANES), lambda i: (0, 0))],
        out_specs=[pl.BlockSpec((tm, LANES), lambda i: (i, 0)),
                   pl.BlockSpec((tm, LANES), lambda i: (i, 0))],
        out_shape=[jax.ShapeDtypeStruct((t, LANES), F32)] * 2,
        compiler_params=_cparams(("parallel",)),
        name="rope_tables",
    )(pos, freq, sign)


def _tile_lanes(x, reps):
    return jnp.concatenate([x] * reps, axis=1)


def _rotate_half_split(x, cos, sin_signed, group, half):
    w = x.shape[1]
    lane = lax.broadcasted_iota(jnp.int32, x.shape, 1) % group
    partner = jnp.where(lane < half, pltpu.roll(x, w - half, axis=1), pltpu.roll(x, half, axis=1))
    return x * cos + partner * sin_signed


def _proj_kernel(h_ref, w_ref, z_ref, *, chunk):
    hb = h_ref[...].astype(BF16)
    for n0 in range(0, IN_COLS, chunk):
        z_ref[:, n0:n0 + chunk] = _dot(hb, w_ref[:, n0:n0 + chunk]).astype(BF16)


def _proj_call(h, w_bf16, tm=512):
    t, d = h.shape
    return pl.pallas_call(
        functools.partial(_proj_kernel, chunk=512),
        grid=(t // tm,),
        in_specs=[pl.BlockSpec((tm, d), lambda i: (i, 0)),
                  pl.BlockSpec((d, IN_COLS), lambda i: (0, 0), pipeline_mode=pl.Buffered(1))],
        out_specs=pl.BlockSpec((tm, IN_COLS), lambda i: (i, 0)),
        out_shape=jax.ShapeDtypeStruct((t, IN_COLS), BF16),
        compiler_params=_cparams(("parallel",)),
        name="proj_in",
    )(h, w_bf16)


CONV_HALO = 16


def _conv_kernel(b_ref, c_ref, u_ref, ch_ref, uh_ref, w_ref, o_ref):
    ts = c_ref.shape[0]
    first = pl.program_id(1) == 0
    cu = c_ref[...].astype(F32) * u_ref[...].astype(F32)
    halo = ch_ref[...].astype(F32) * uh_ref[...].astype(F32)
    halo = jnp.where(first, 0.0, halo)
    ext = jnp.concatenate([halo, cu], axis=0)
    n = ext.shape[0]
    prev1 = pltpu.roll(ext, 1, axis=0)[CONV_HALO:n]
    prev2 = pltpu.roll(ext, 2, axis=0)[CONV_HALO:n]
    w = w_ref[...]
    y = prev2 * w[0:1, :] + prev1 * w[1:2, :] + cu * w[2:3, :]
    o_ref[...] = (b_ref[...].astype(F32) * y).astype(BF16)


def _conv_call(z, conv_w, bsz, seq, ts=1024):
    t = bsz * seq
    ts = min(ts, seq)
    nt = seq // ts
    wb = CONV_WIDTH
    hb = ts // CONV_HALO
    col = lambda k: (lambda b, i: (b * nt + i, CONV_OFF // wb + k))
    halo = lambda k: (lambda b, i: (jnp.maximum((b * nt + i) * hb - 1, 0), CONV_OFF // wb + k))
    return pl.pallas_call(
        _conv_kernel,
        grid=(bsz, nt),
        in_specs=[pl.BlockSpec((ts, wb), col(0)),
                  pl.BlockSpec((ts, wb), col(1)),
                  pl.BlockSpec((ts, wb), col(2)),
                  pl.BlockSpec((CONV_HALO, wb), halo(1)),
                  pl.BlockSpec((CONV_HALO, wb), halo(2)),
                  pl.BlockSpec((CONV_K, wb), lambda b, i: (0, 0))],
        out_specs=pl.BlockSpec((ts, wb), lambda b, i: (b * nt + i, 0)),
        out_shape=jax.ShapeDtypeStruct((t, wb), BF16),
        compiler_params=_cparams(("parallel", "parallel")),
        name="conv_mixer",
    )(z, z, z, z, z, conv_w.reshape(CONV_K, wb))


def _sgu_kernel(u_ref, v_ref, g_ref, b_ref, ws_ref, bias_ref, o_ref):
    ts = u_ref.shape[0]
    c = SGU_CHUNK
    v = _layer_norm(v_ref[...].astype(F32), g_ref[...], b_ref[...]).astype(BF16)
    row = lax.broadcasted_iota(jnp.int32, (c, c), 0)
    colm = lax.broadcasted_iota(jnp.int32, (c, c), 1)
    bias = bias_ref[...]
    for g in range(SGU_GROUPS):
        w = jnp.where(row >= colm, ws_ref[g], 0.0).astype(BF16)
        lo = g * SGU_GROUP_DIM
        for n in range(ts // c):
            s = _dot(w, v[n * c:(n + 1) * c, lo:lo + SGU_GROUP_DIM]) + bias[:, lo:lo + SGU_GROUP_DIM]
            u = u_ref[n * c:(n + 1) * c, lo:lo + SGU_GROUP_DIM].astype(F32)
            o_ref[n * c:(n + 1) * c, lo:lo + SGU_GROUP_DIM] = (u * s).astype(BF16)


def _sgu_call(z, ln_g, ln_b, w_s, b_s, ts=512):
    t = z.shape[0]
    ts = min(ts, t)
    wb = SGU_WIDTH
    bias = jnp.repeat(b_s.T, SGU_GROUP_DIM, axis=1)
    return pl.pallas_call(
        _sgu_kernel,
        grid=(t // ts,),
        in_specs=[pl.BlockSpec((ts, wb), lambda i: (i, SGU_OFF // wb)),
                  pl.BlockSpec((ts, wb), lambda i: (i, SGU_OFF // wb + 1)),
                  pl.BlockSpec((1, wb), lambda i: (0, 0)),
                  pl.BlockSpec((1, wb), lambda i: (0, 0)),
                  pl.BlockSpec((SGU_GROUPS, SGU_CHUNK, SGU_CHUNK), lambda i: (0, 0, 0)),
                  pl.BlockSpec((SGU_CHUNK, wb), lambda i: (0, 0))],
        out_specs=pl.BlockSpec((ts, wb), lambda i: (i, 0)),
        out_shape=jax.ShapeDtypeStruct((t, wb), BF16),
        compiler_params=_cparams(("parallel",)),
        name="sgu_mixer",
    )(z, z, ln_g.reshape(1, wb), ln_b.reshape(1, wb), w_s, bias)


def _ret_tables():
    c = RET_CHUNK
    log_gamma = jnp.log1p(-jnp.exp2(-5.0 - jnp.arange(RET_HEADS, dtype=F32)))
    idx = jnp.arange(c, dtype=F32)
    rel = idx[:, None] - idx[None, :]
    decay = jnp.where(rel >= 0, jnp.exp(jnp.maximum(rel, 0.0)[None] * log_gamma[:, None, None]), 0.0)
    k_decay = jnp.exp((c - 1 - idx)[:, None] * log_gamma[None, :])
    q_decay = jnp.exp((idx + 1.0)[:, None] * log_gamma[None, :])
    chunk_decay = jnp.exp(c * log_gamma)
    hk = RET_HEADS * RET_DK
    kd = jnp.repeat(k_decay, RET_DK, axis=1) * (RET_DK ** -0.5)
    qd = jnp.repeat(q_decay, RET_DK, axis=1)
    cd = jnp.broadcast_to(jnp.repeat(chunk_decay, RET_DV)[None, :], (8, RET_HEADS * RET_DV))
    del hk
    return decay, kd, qd, cd


def _ret_kernel(q_ref, k_ref, v_ref, g_ref, cos_ref, sin_ref, decay_ref, kd_ref, qd_ref, cd_ref,
                o_ref, state_ref):
    ts = q_ref.shape[0]
    c = RET_CHUNK
    hk = RET_HEADS * RET_DK

    @pl.when(pl.program_id(1) == 0)
    def _():
        state_ref[...] = jnp.zeros_like(state_ref)

    cos = _tile_lanes(cos_ref[:, 0:RET_DK], RET_HEADS)
    sin = _tile_lanes(sin_ref[:, 0:RET_DK], RET_HEADS)
    q = _rotate_half_split(q_ref[...].astype(F32), cos, sin, RET_DK, RET_DK // 2)
    k = _rotate_half_split(k_ref[...].astype(F32), cos, sin, RET_DK, RET_DK // 2)
    del hk
    for n in range(ts // c):
        r0 = n * c
        qn = q[r0:r0 + c]
        kn = k[r0:r0 + c]
        qb = qn.astype(BF16)
        kb = (kn * (RET_DK ** -0.5)).astype(BF16)
        qdb = (qn * qd_ref[...]).astype(BF16)
        kdb = (kn * kd_ref[...]).astype(BF16)
        for h in range(RET_HEADS):
            ks = slice(h * RET_DK, (h + 1) * RET_DK)
            vs = slice(h * RET_DV, (h + 1) * RET_DV)
            vb = v_ref[r0:r0 + c, vs]
            scores = _dot_nt(qb[:, ks], kb[:, ks]) * decay_ref[h]
            inner = _dot(scores.astype(BF16), vb)
            state = state_ref[h]
            cross = _dot(qdb[:, ks], state.astype(BF16))
            kv = _dot_tn(kdb[:, ks], vb)
            state_ref[h] = state * cd_ref[0:1, vs] + kv
            o = inner + cross
            o = o * lax.rsqrt(jnp.mean(o * o, axis=-1, keepdims=True) + RMS_EPS)
            gate = g_ref[r0:r0 + c, vs].astype(F32)
            gate = gate * jax.nn.sigmoid(gate)
            o_ref[r0:r0 + c, vs] = (gate * o).astype(BF16)


def _ret_call(z, cos_t, sin_t, bsz, seq, ts=512):
    t = bsz * seq
    ts = min(ts, seq)
    nt = seq // ts
    decay, kd, qd, cd = _ret_tables()
    hk = RET_HEADS * RET_DK
    hv = RET_HEADS * RET_DV
    row = lambda b, i: b * nt + i
    return pl.pallas_call(
        _ret_kernel,
        grid=(bsz, nt),
        in_specs=[pl.BlockSpec((ts, hk), lambda b, i: (row(b, i), RET_OFF // hk)),
                  pl.BlockSpec((ts, hk), lambda b, i: (row(b, i), RET_OFF // hk + 1)),
                  pl.BlockSpec((ts, hv), lambda b, i: (row(b, i), (RET_OFF + 2 * hk) // hv)),
                  pl.BlockSpec((ts, hv), lambda b, i: (row(b, i), (RET_OFF + 2 * hk) // hv + 1)),
                  pl.BlockSpec((ts, LANES), lambda b, i: (row(b, i), 0)),
                  pl.BlockSpec((ts, LANES), lambda b, i: (row(b, i), 0)),
                  pl.BlockSpec((RET_HEADS, RET_CHUNK, RET_CHUNK), lambda b, i: (0, 0, 0)),
                  pl.BlockSpec((RET_CHUNK, hk), lambda b, i: (0, 0)),
                  pl.BlockSpec((RET_CHUNK, hk), lambda b, i: (0, 0)),
                  pl.BlockSpec((8, hv), lambda b, i: (0, 0))],
        out_specs=pl.BlockSpec((ts, hv), lambda b, i: (row(b, i), 0)),
        out_shape=jax.ShapeDtypeStruct((t, hv), BF16),
        scratch_shapes=[pltpu.VMEM((RET_HEADS, RET_DK, RET_DV), F32)],
        compiler_params=_cparams(("parallel", "arbitrary")),
        name="ret_mixer",
    )(z, z, z, z, cos_t, sin_t, decay, kd, qd, cd)


LOG2E = 1.4426950408889634


def _qkprep_kernel(q_ref, k_ref, v_ref, cos_ref, sin_ref, qt_ref, ko_ref, vt_ref):
    reps = q_ref.shape[1] // DIFF_HEAD_DIM
    cos = _tile_lanes(cos_ref[:, DIFF_HEAD_DIM:2 * DIFF_HEAD_DIM], reps)
    sin = _tile_lanes(sin_ref[:, DIFF_HEAD_DIM:2 * DIFF_HEAD_DIM], reps)
    q = _rotate_half_split(q_ref[...].astype(F32), cos, sin, DIFF_HEAD_DIM, ROT_DIM // 2)
    k = _rotate_half_split(k_ref[...].astype(F32), cos, sin, DIFF_HEAD_DIM, ROT_DIM // 2)
    qt_ref[...] = (q * (DIFF_HEAD_DIM ** -0.5 * LOG2E)).T.astype(BF16)
    ko_ref[...] = k.astype(BF16)
    vt_ref[...] = v_ref[...].astype(F32).T.astype(BF16)


def _qkprep_call(z, cos_t, sin_t, bsz, seq, tm=512):
    t = bsz * seq
    tm = min(tm, seq)
    nt = seq // tm
    hq = DIFF_HEADS * 2 * DIFF_HEAD_DIM
    row = lambda b, i: b * nt + i
    return pl.pallas_call(
        _qkprep_kernel,
        grid=(bsz, nt),
        in_specs=[pl.BlockSpec((tm, hq), lambda b, i: (row(b, i), DIFF_OFF // hq)),
                  pl.BlockSpec((tm, hq), lambda b, i: (row(b, i), DIFF_OFF // hq + 1)),
                  pl.BlockSpec((tm, hq), lambda b, i: (row(b, i), DIFF_OFF // hq + 2)),
                  pl.BlockSpec((tm, LANES), lambda b, i: (row(b, i), 0)),
                  pl.BlockSpec((tm, LANES), lambda b, i: (row(b, i), 0))],
        out_specs=[pl.BlockSpec((None, hq, tm), lambda b, i: (b, 0, i)),
                   pl.BlockSpec((tm, hq), lambda b, i: (row(b, i), 0)),
                   pl.BlockSpec((None, hq, tm), lambda b, i: (b, 0, i))],
        out_shape=[jax.ShapeDtypeStruct((bsz, hq, seq), BF16),
                   jax.ShapeDtypeStruct((t, hq), BF16),
                   jax.ShapeDtypeStruct((bsz, hq, seq), BF16)],
        compiler_params=_cparams(("parallel", "parallel")),
        name="diff_qkprep",
    )(z, z, z, cos_t, sin_t)


def _flash_kernel(qi_ref, ki_ref, qt_ref, k_ref, vt_ref, lam_ref, g_ref, o_ref,
                  qd_ref, m_ref, l_ref, acc_ref, *, tq, tk, cq, lambda_init):
    p = pl.program_id(2)
    qi = qi_ref[p]
    ki = ki_ref[p]
    last_k = ((qi + 1) * tq - 1) // tk
    hd = DIFF_HEAD_DIM

    @pl.when(ki == 0)
    def _():
        qd_ref[...] = jnp.zeros_like(qd_ref)
        qd_ref[0:hd, 0:tq] = qt_ref[0:hd, :]
        qd_ref[hd:2 * hd, tq:2 * tq] = qt_ref[hd:2 * hd, :]
        m_ref[...] = jnp.full_like(m_ref, NEG_BIG)
        l_ref[...] = jnp.zeros_like(l_ref)
        acc_ref[...] = jnp.zeros_like(acc_ref)

    def step(masked):
        k = k_ref[...]
        vt = vt_ref[...]
        if masked:
            key = lax.broadcasted_iota(jnp.int32, (tk, cq), 0) + ki * tk
            lane = lax.broadcasted_iota(jnp.int32, (tk, cq), 1)
        nc = 2 * tq // cq
        cols = [slice(c * cq, (c + 1) * cq) for c in range(nc)]
        m_prev = m_ref[...]
        l_prev = l_ref[...]
        s = [_dot(k, qd_ref[:, cs]) for cs in cols]
        if masked:
            s = [jnp.where(key <= lane + (qi * tq + (c * cq) % tq), s[c], NEG_BIG) for c in range(nc)]
        m_new = [jnp.maximum(m_prev[:, cols[c]], jnp.max(s[c], axis=0, keepdims=True)) for c in range(nc)]
        pexp = [jnp.exp2(s[c] - m_new[c]) for c in range(nc)]
        alpha = [jnp.exp2(m_prev[:, cols[c]] - m_new[c]) for c in range(nc)]
        for c in range(nc):
            cs = cols[c]
            l_ref[:, cs] = alpha[c] * l_prev[:, cs] + jnp.sum(pexp[c], axis=0, keepdims=True)
            m_ref[:, cs] = m_new[c]
            acc_ref[:, cs] = alpha[c] * acc_ref[:, cs] + _dot(vt, pexp[c].astype(BF16))

    needs_mask = (ki + 1) * tk - 1 > qi * tq

    @pl.when(needs_mask)
    def _():
        step(True)

    @pl.when(jnp.logical_not(needs_mask))
    def _():
        step(False)

    @pl.when(ki == last_k)
    def _():
        lam_p = lam_ref[...]
        lam = (jnp.exp(jnp.sum(lam_p[0:1] * lam_p[1:2], axis=-1, keepdims=True))
               - jnp.exp(jnp.sum(lam_p[2:3] * lam_p[3:4], axis=-1, keepdims=True)) + lambda_init)
        o = acc_ref[...] / l_ref[...]
        o = (o[:, 0:tq] - lam * o[:, tq:2 * tq]).T
        o = o * lax.rsqrt(jnp.mean(o * o, axis=-1, keepdims=True) + RMS_EPS)
        o_ref[...] = (o * g_ref[...] * (1.0 - lambda_init)).astype(BF16)


def _flash_call(qt, kr, vt, lam_params, subln_g, bsz, seq, lambda_init, tq=1024, tk=1024, cq=256):
    t = bsz * seq
    tq = min(tq, seq)
    tk = min(tk, seq)
    cq = min(cq, tq)
    nq = seq // tq
    qi_list, ki_list = [], []
    for qi in range(nq):
        for ki in range(((qi + 1) * tq - 1) // tk + 1):
            qi_list.append(qi)
            ki_list.append(ki)
    qi_arr = jnp.asarray(qi_list, jnp.int32)
    ki_arr = jnp.asarray(ki_list, jnp.int32)
    hd = 2 * DIFF_HEAD_DIM
    grid_spec = pltpu.PrefetchScalarGridSpec(
        num_scalar_prefetch=2,
        grid=(bsz, DIFF_HEADS, len(qi_list)),
        in_specs=[pl.BlockSpec((None, hd, tq), lambda b, h, p, qi, ki: (b, h, qi[p])),
                  pl.BlockSpec((tk, hd), lambda b, h, p, qi, ki: (b * (seq // tk) + ki[p], h)),
                  pl.BlockSpec((None, DIFF_V_DIM, tk), lambda b, h, p, qi, ki: (b, h, ki[p])),
                  pl.BlockSpec((4, DIFF_HEAD_DIM), lambda b, h, p, qi, ki: (0, 0)),
                  pl.BlockSpec((1, DIFF_V_DIM), lambda b, h, p, qi, ki: (0, 0))],
        out_specs=pl.BlockSpec((tq, DIFF_V_DIM), lambda b, h, p, qi, ki: (b * (seq // tq) + qi[p], h)),
        scratch_shapes=[pltpu.VMEM((hd, 2 * tq), BF16),
                        pltpu.VMEM((1, 2 * tq), F32),
                        pltpu.VMEM((1, 2 * tq), F32),
                        pltpu.VMEM((DIFF_V_DIM, 2 * tq), F32)],
    )
    return pl.pallas_call(
        functools.partial(_flash_kernel, tq=tq, tk=tk, cq=cq, lambda_init=lambda_init),
        grid_spec=grid_spec,
        out_shape=jax.ShapeDtypeStruct((t, DIFF_HEADS * DIFF_V_DIM), BF16),
        compiler_params=_cparams(("parallel", "parallel", "arbitrary")),
        name="diff_flash",
    )(qi_arr, ki_arr, qt, kr, vt, lam_params, subln_g.reshape(1, DIFF_V_DIM))


def _merge_kernel(h_ref, y0_ref, y1_ref, y2_ref, y3_ref, wg_ref, bg_ref, wb_ref, wo_ref,
                  g_ref, b_ref, o_ref, *, alpha):
    h = h_ref[...]
    hb = h.astype(BF16)
    merged = None
    for g, y_ref in enumerate((y0_ref, y1_ref, y2_ref, y3_ref)):
        gate = jax.nn.sigmoid(_dot(hb, wg_ref[g]) + bg_ref[g:g + 1, :])
        term = gate * _dot(y_ref[...], wb_ref[g])
        merged = term if merged is None else merged + term
    t = _dot(merged.astype(BF16), wo_ref[...])
    o_ref[...] = _layer_norm(alpha * h + t, g_ref[...], b_ref[...])


def _merge_call(h, ys, wg, bg, wb, wo, ln_g, ln_b, alpha, tm=512):
    t, d = h.shape
    tm = min(tm, t)
    w = BRANCH_WIDTH
    const = dict(pipeline_mode=pl.Buffered(1))
    return pl.pallas_call(
        functools.partial(_merge_kernel, alpha=alpha),
        grid=(t // tm,),
        in_specs=[pl.BlockSpec((tm, d), lambda i: (i, 0))]
        + [pl.BlockSpec((tm, w), lambda i: (i, 0))] * N_BRANCH
        + [pl.BlockSpec((N_BRANCH, d, d), lambda i: (0, 0, 0), **const),
           pl.BlockSpec((N_BRANCH, d), lambda i: (0, 0)),
           pl.BlockSpec((N_BRANCH, w, d), lambda i: (0, 0, 0), **const),
           pl.BlockSpec((d, d), lambda i: (0, 0), **const),
           pl.BlockSpec((1, d), lambda i: (0, 0)),
           pl.BlockSpec((1, d), lambda i: (0, 0))],
        out_specs=pl.BlockSpec((tm, d), lambda i: (i, 0)),
        out_shape=jax.ShapeDtypeStruct((t, d), F32),
        compiler_params=_cparams(("parallel",)),
        name="merge",
    )(h, *ys, wg, bg, wb, wo, ln_g.reshape(1, d), ln_b.reshape(1, d))


def _route_kernel(h_ref, wr_ref, br_ref, idx_ref, gw_ref):
    logits = lax.dot_general(wr_ref[...], h_ref[...], (((1,), (1,)), ((), ())),
                             preferred_element_type=F32, precision=lax.Precision.HIGHEST)
    mx = jnp.max(logits, axis=0, keepdims=True)
    e = jnp.exp(logits - mx)
    scores = e / jnp.sum(e, axis=0, keepdims=True)
    sel = scores + br_ref[...]
    epg = EXPERTS_PER_GROUP
    rows = [sel[i:i + 1, :] for i in range(N_EXPERTS)]
    srows = [scores[i:i + 1, :] for i in range(N_EXPERTS)]
    best_score = None
    best = None
    for g in range(N_GROUPS):
        r = rows[g * epg:(g + 1) * epg]
        gs = None
        for a in range(epg):
            for b in range(a + 1, epg):
                pair = r[a] + r[b]
                gs = pair if gs is None else jnp.maximum(gs, pair)
        if best is None:
            best_score, best = gs, jnp.zeros(gs.shape, jnp.int32)
        else:
            take = gs > best_score
            best_score = jnp.where(take, gs, best_score)
            best = jnp.where(take, g, best)
    cand, cscore = [], []
    for j in range(epg):
        c = rows[j]
        s = srows[j]
        for g in range(1, N_GROUPS):
            c = jnp.where(best == g, rows[g * epg + j], c)
            s = jnp.where(best == g, srows[g * epg + j], s)
        cand.append(c)
        cscore.append(s)
    v1, i1, s1 = cand[0], jnp.zeros(best.shape, jnp.int32), cscore[0]
    for j in range(1, epg):
        take = cand[j] > v1
        v1 = jnp.where(take, cand[j], v1)
        i1 = jnp.where(take, j, i1)
        s1 = jnp.where(take, cscore[j], s1)
    v2 = jnp.full(v1.shape, -jnp.inf, F32)
    i2 = jnp.zeros(best.shape, jnp.int32)
    s2 = jnp.zeros(v1.shape, F32)
    for j in range(epg):
        take = jnp.logical_and(i1 != j, cand[j] > v2)
        v2 = jnp.where(take, cand[j], v2)
        i2 = jnp.where(take, j, i2)
        s2 = jnp.where(take, cscore[j], s2)
    denom = s1 + s2
    idx_ref[0:1, :] = best * epg + i1
    idx_ref[1:2, :] = best * epg + i2
    gw_ref[0:1, :] = s1 / denom
    gw_ref[1:2, :] = s2 / denom


def _route_call(h, w_router, b_router, tm=1024):
    t, d = h.shape
    tm = min(tm, t)
    return pl.pallas_call(
        _route_kernel,
        grid=(t // tm,),
        in_specs=[pl.BlockSpec((tm, d), lambda i: (i, 0)),
                  pl.BlockSpec((N_EXPERTS, d), lambda i: (0, 0)),
                  pl.BlockSpec((N_EXPERTS, 1), lambda i: (0, 0))],
        out_specs=[pl.BlockSpec((TOP_K, tm), lambda i: (0, i)),
                   pl.BlockSpec((TOP_K, tm), lambda i: (0, i))],
        out_shape=[jax.ShapeDtypeStruct((TOP_K, t), jnp.int32),
                   jax.ShapeDtypeStruct((TOP_K, t), F32)],
        compiler_params=_cparams(("parallel",)),
        name="route",
    )(h, w_router.T, b_router.reshape(N_EXPERTS, 1))


def _rank_kernel(idx_ref, dest_ref, cnt_ref, run_ref, start_ref, *, blk):
    phase = pl.program_id(0)
    i = pl.program_id(1)
    tm = idx_ref.shape[1]
    expert = lax.broadcasted_iota(jnp.int32, (N_EXPERTS, tm), 0)
    oh0 = expert == idx_ref[0:1, :]
    oh1 = expert == idx_ref[1:2, :]
    f0 = jnp.where(oh0, 1.0, 0.0)
    f1 = jnp.where(oh1, 1.0, 0.0)

    @pl.when(jnp.logical_and(phase == 0, i == 0))
    def _():
        run_ref[...] = jnp.zeros_like(run_ref)

    @pl.when(phase == 0)
    def _():
        tot = jnp.sum(f0 + f1, axis=1, keepdims=True)
        run_ref[...] = run_ref[...] + tot

    @pl.when(jnp.logical_and(phase == 1, i == 0))
    def _():
        cnt = run_ref[...]
        cnt_ref[...] = cnt.astype(jnp.int32)
        padded = jnp.ceil(cnt * (1.0 / blk)) * blk
        acc = jnp.zeros((1, LANES), F32)
        for e in range(N_EXPERTS):
            start_ref[e:e + 1, :] = acc
            acc = acc + padded[e:e + 1, :]
        run_ref[...] = jnp.zeros_like(run_ref)

    @pl.when(phase == 1)
    def _():
        s_idx = lax.broadcasted_iota(jnp.int32, (tm, tm), 0)
        t_idx = lax.broadcasted_iota(jnp.int32, (tm, tm), 1)
        tri = jnp.where(s_idx < t_idx, 1.0, 0.0).astype(BF16)
        c0 = _dot(f0.astype(BF16), tri)
        c1 = _dot(f1.astype(BF16), tri)
        tot0 = jnp.sum(f0, axis=1, keepdims=True)
        tot1 = jnp.sum(f1, axis=1, keepdims=True)
        base = start_ref[:, 0:1] + run_ref[:, 0:1]
        d0 = jnp.sum(jnp.where(oh0, base + c0, 0.0), axis=0, keepdims=True)
        d1 = jnp.sum(jnp.where(oh1, base + tot0 + c1, 0.0), axis=0, keepdims=True)
        dest_ref[0:1, :] = d0.astype(jnp.int32)
        dest_ref[1:2, :] = d1.astype(jnp.int32)
        run_ref[...] = run_ref[...] + (tot0 + tot1)


def _rank_call(idx, blk, tm=512):
    t = idx.shape[1]
    tm = min(tm, t)
    return pl.pallas_call(
        functools.partial(_rank_kernel, blk=blk),
        grid=(2, t // tm),
        in_specs=[pl.BlockSpec((TOP_K, tm), lambda p, i: (0, i))],
        out_specs=[pl.BlockSpec((TOP_K, tm), lambda p, i: (0, i * p)),
                   pl.BlockSpec((N_EXPERTS, LANES), lambda p, i: (0, 0))],
        out_shape=[jax.ShapeDtypeStruct((TOP_K, t), jnp.int32),
                   jax.ShapeDtypeStruct((N_EXPERTS, LANES), jnp.int32)],
        scratch_shapes=[pltpu.VMEM((N_EXPERTS, LANES), F32),
                        pltpu.VMEM((N_EXPERTS, LANES), F32)],
        compiler_params=_cparams(("arbitrary", "arbitrary")),
        name="moe_rank",
    )(idx)


def _dispatch_kernel(dest_ref, x_ref, zeros_ref, xs_ref, sem):
    del zeros_ref
    i = pl.program_id(0)
    tm = x_ref.shape[0]
    t = pl.num_programs(0) * tm

    def issue(r, carry):
        tok = i * tm + r
        for k in range(TOP_K):
            d = dest_ref[k * t + tok]
            pltpu.make_async_copy(x_ref.at[pl.ds(r, 1), :], xs_ref.at[pl.ds(d, 1), :], sem).start()
        return carry

    lax.fori_loop(0, tm, issue, 0)

    def drain(r, carry):
        for k in range(TOP_K):
            pltpu.make_async_copy(x_ref.at[pl.ds(0, 1), :], xs_ref.at[pl.ds(0, 1), :], sem).wait()
        return carry

    lax.fori_loop(0, tm, drain, 0)


def _dispatch_call(h, dest_flat, n_rows, tm=512):
    t, d = h.shape
    tm = min(tm, t)
    zeros = jnp.zeros((n_rows, d), F32)
    grid_spec = pltpu.PrefetchScalarGridSpec(
        num_scalar_prefetch=1,
        grid=(t // tm,),
        in_specs=[pl.BlockSpec((tm, d), lambda i, dest: (i, 0)),
                  pl.BlockSpec(memory_space=pl.ANY)],
        out_specs=pl.BlockSpec(memory_space=pl.ANY),
        scratch_shapes=[pltpu.SemaphoreType.DMA(())],
    )
    return pl.pallas_call(
        _dispatch_kernel,
        grid_spec=grid_spec,
        out_shape=jax.ShapeDtypeStruct((n_rows, d), F32),
        input_output_aliases={2: 0},
        compiler_params=_cparams(("arbitrary",)),
        name="moe_dispatch",
    )(dest_flat, h, zeros)


def _experts_kernel(be_ref, nb_ref, x_ref, w1_ref, w3_ref, w2_ref, o_ref):
    i = pl.program_id(0)

    @pl.when(i < nb_ref[0])
    def _():
        xb = x_ref[...].astype(BF16)
        a = _dot(xb, w1_ref[0])
        b = _dot(xb, w3_ref[0])
        act = (a * jax.nn.sigmoid(a) * b).astype(BF16)
        o_ref[...] = _dot(act, w2_ref[0])

    @pl.when(i >= nb_ref[0])
    def _():
        o_ref[...] = jnp.zeros_like(o_ref)


def _experts_call(xs, blk_expert, n_used, w1, w3, w2, blk):
    n_rows, d = xs.shape
    de = w1.shape[2]
    wmap = lambda i, be, nb: (be[i], 0, 0)
    grid_spec = pltpu.PrefetchScalarGridSpec(
        num_scalar_prefetch=2,
        grid=(n_rows // blk,),
        in_specs=[pl.BlockSpec((blk, d), lambda i, be, nb: (i, 0)),
                  pl.BlockSpec((1, d, de), wmap),
                  pl.BlockSpec((1, d, de), wmap),
                  pl.BlockSpec((1, de, d), wmap)],
        out_specs=pl.BlockSpec((blk, d), lambda i, be, nb: (i, 0)),
    )
    return pl.pallas_call(
        _experts_kernel,
        grid_spec=grid_spec,
        out_shape=jax.ShapeDtypeStruct((n_rows, d), F32),
        compiler_params=_cparams(("arbitrary",)),
        name="moe_experts",
    )(blk_expert, n_used, xs, w1, w3, w2)


def _combine_kernel(dest_ref, h_ref, gw_ref, ys_ref, g_ref, b_ref, o_ref, buf_ref, sem, *, alpha):
    i = pl.program_id(0)
    tm = h_ref.shape[0]
    t = pl.num_programs(0) * tm

    def issue(r, carry):
        tok = i * tm + r
        for k in range(TOP_K):
            d = dest_ref[k * t + tok]
            pltpu.make_async_copy(ys_ref.at[pl.ds(d, 1), :], buf_ref.at[k, pl.ds(r, 1), :], sem).start()
        return carry

    lax.fori_loop(0, tm, issue, 0)

    def drain(r, carry):
        for k in range(TOP_K):
            pltpu.make_async_copy(ys_ref.at[pl.ds(0, 1), :], buf_ref.at[k, pl.ds(0, 1), :], sem).wait()
        return carry

    lax.fori_loop(0, tm, drain, 0)
    gw = gw_ref[...]
    y = buf_ref[0] * gw[:, 0:1] + buf_ref[1] * gw[:, 1:2]
    o_ref[...] = _layer_norm(alpha * h_ref[...] + y, g_ref[...], b_ref[...])


def _combine_call(h, gw_t, ys, dest_flat, ln_g, ln_b, alpha, tm=256):
    t, d = h.shape
    tm = min(tm, t)
    grid_spec = pltpu.PrefetchScalarGridSpec(
        num_scalar_prefetch=1,
        grid=(t // tm,),
        in_specs=[pl.BlockSpec((tm, d), lambda i, dest: (i, 0)),
                  pl.BlockSpec((tm, TOP_K), lambda i, dest: (i, 0)),
                  pl.BlockSpec(memory_space=pl.ANY),
                  pl.BlockSpec((1, d), lambda i, dest: (0, 0)),
                  pl.BlockSpec((1, d), lambda i, dest: (0, 0))],
        out_specs=pl.BlockSpec((tm, d), lambda i, dest: (i, 0)),
        scratch_shapes=[pltpu.VMEM((TOP_K, tm, d), F32),
                        pltpu.SemaphoreType.DMA(())],
    )
    return pl.pallas_call(
        functools.partial(_combine_kernel, alpha=alpha),
        grid_spec=grid_spec,
        out_shape=jax.ShapeDtypeStruct((t, d), F32),
        compiler_params=_cparams(("arbitrary",)),
        name="moe_combine",
    )(dest_flat, h, gw_t, ys, ln_g.reshape(1, d), ln_b.reshape(1, d))


def _moe_layer(h, w_router, b_router, w1, w3, w2, ln_g, ln_b, alpha, blk=256):
    t, d = h.shape
    idx, gw = _route_call(h, w_router, b_router)
    dest, counts = _rank_call(idx, blk)
    n_rows = TOP_K * t + N_EXPERTS * blk
    n_blk = n_rows // blk
    cnt = counts[:, 0]
    pad_end = jnp.cumsum(((cnt + blk - 1) // blk) * blk)
    blk_expert = jnp.minimum(
        jnp.sum(pad_end[None, :] <= (jnp.arange(n_blk, dtype=jnp.int32) * blk)[:, None], axis=1),
        N_EXPERTS - 1).astype(jnp.int32)
    n_used = (pad_end[N_EXPERTS - 1:] // blk).astype(jnp.int32)
    dest_flat = dest.reshape(-1)
    xs = _dispatch_call(h, dest_flat, n_rows)
    ys = _experts_call(xs, blk_expert, n_used, w1, w3, w2, blk)
    return _combine_call(h, gw.T, ys, dest_flat, ln_g, ln_b, alpha)


def kernel(x, positions, ln_in_g, ln_in_b, w_in, conv_w, sgu_ln_g, sgu_ln_b, w_s, b_s, lambda_q1, lambda_k1, lambda_q2, lambda_k2, diff_subln_g, w_gate, b_gate, w_branch, w_o, ln1_g, ln1_b, w_router, b_router, w1, w3, w2, ln2_g, ln2_b):
    bsz, seq, d = x.shape
    depth = w_in.shape[0]
    alpha = (2.0 * depth) ** 0.25
    t = bsz * seq
    cos_t, sin_t = _rope_tables(positions)
    h = _ln_call(x.reshape(t, d), ln_in_g, ln_in_b)
    for l in range(depth):
        lambda_init = 0.8 - 0.6 * math.exp(-0.3 * l)
        z = _proj_call(h, w_in[l].astype(BF16))
        y_conv = _conv_call(z, conv_w[l], bsz, seq)
        y_ret = _ret_call(z, cos_t, sin_t, bsz, seq)
        y_sgu = _sgu_call(z, sgu_ln_g[l], sgu_ln_b[l], w_s[l], b_s[l])
        qt, kr, vt = _qkprep_call(z, cos_t, sin_t, bsz, seq)
        lam_params = jnp.stack([lambda_q1[l], lambda_k1[l], lambda_q2[l], lambda_k2[l]])
        y_diff = _flash_call(qt, kr, vt, lam_params, diff_subln_g[l], bsz, seq, lambda_init)
        h = _merge_call(h, (y_conv, y_ret, y_sgu, y_diff), w_gate[l].astype(BF16), b_gate[l],
                        w_branch[l].astype(BF16), w_o[l].astype(BF16), ln1_g[l], ln1_b[l], alpha)
        h = _moe_layer(h, w_router, b_router, w1[l].astype(BF16), w3[l].astype(BF16),
                       w2[l].astype(BF16), ln2_g[l], ln2_b[l], alpha)
    return h.reshape(bsz, seq, d)
```

```python
import functools
import math

import jax
import jax.numpy as jnp
from jax import lax
from jax.experimental import pallas as pl
from jax.experimental.pallas import tpu as pltpu

D_MODEL = 1024
BRANCH_WIDTH = 512
N_BRANCH = 4
CONV_WIDTH = BRANCH_WIDTH
CONV_K = 3
RET_HEADS = 4
RET_DK = 64
RET_DV = 128
RET_CHUNK = 128
RET_THETA = 10000.0
SGU_GROUPS = 4
SGU_GROUP_DIM = BRANCH_WIDTH // SGU_GROUPS
SGU_CHUNK = 128
SGU_WIDTH = BRANCH_WIDTH
DIFF_HEADS = 4
DIFF_HEAD_DIM = 64
DIFF_V_DIM = 2 * DIFF_HEAD_DIM
ROPE_THETA = 500000.0
ROT_DIM = DIFF_HEAD_DIM // 4
CONV_COLS = 3 * CONV_WIDTH
RET_COLS = 2 * RET_HEADS * RET_DK + 2 * RET_HEADS * RET_DV
SGU_COLS = 2 * SGU_WIDTH
DIFF_COLS = 2 * DIFF_HEADS * 2 * DIFF_HEAD_DIM + DIFF_HEADS * DIFF_V_DIM
IN_COLS = CONV_COLS + RET_COLS + SGU_COLS + DIFF_COLS
N_EXPERTS = 16
N_GROUPS = 4
EXPERTS_PER_GROUP = N_EXPERTS // N_GROUPS
TOP_K = 2
D_EXPERT = 1024
LN_EPS = 1e-5
RMS_EPS = 1e-6

LANES = 128
CONV_OFF = 0
RET_OFF = CONV_COLS
SGU_OFF = RET_OFF + RET_COLS
DIFF_OFF = SGU_OFF + SGU_COLS

NEG_BIG = -1e30
VMEM_LIMIT = 56 * 1024 * 1024

BF16 = jnp.bfloat16
F32 = jnp.float32


def _cparams(sem):
    return pltpu.CompilerParams(dimension_semantics=sem, vmem_limit_bytes=VMEM_LIMIT)


def _layer_norm(xf, g, b):
    mu = jnp.mean(xf, axis=-1, keepdims=True)
    xc = xf - mu
    var = jnp.mean(xc * xc, axis=-1, keepdims=True)
    return xc * lax.rsqrt(var + LN_EPS) * g + b


def _dot(a, b):
    return jnp.dot(a, b, preferred_element_type=F32)


def _dot_nt(a, b):
    return lax.dot_general(a, b, (((1,), (1,)), ((), ())), preferred_element_type=F32)


def _dot_tn(a, b):
    return lax.dot_general(a, b, (((0,), (0,)), ((), ())), preferred_element_type=F32)


def _ln_kernel(x_ref, g_ref, b_ref, o_ref):
    o_ref[...] = _layer_norm(x_ref[...], g_ref[...], b_ref[...])


def _ln_call(x2d, g, b, tm=1024):
    t, d = x2d.shape
    return pl.pallas_call(
        _ln_kernel,
        grid=(t // tm,),
        in_specs=[pl.BlockSpec((tm, d), lambda i: (i, 0)),
                  pl.BlockSpec((1, d), lambda i: (0, 0)),
                  pl.BlockSpec((1, d), lambda i: (0, 0))],
        out_specs=pl.BlockSpec((tm, d), lambda i: (i, 0)),
        out_shape=jax.ShapeDtypeStruct((t, d), F32),
        compiler_params=_cparams(("parallel",)),
        name="ln_in",
    )(x2d, g.reshape(1, d), b.reshape(1, d))


def _rope_kernel(pos_ref, freq_ref, sign_ref, cos_ref, sin_ref):
    ang = pos_ref[...] * freq_ref[...]
    cos_ref[...] = jnp.cos(ang)
    sin_ref[...] = jnp.sin(ang) * sign_ref[...]


def _rope_tables(positions):
    t = positions.size
    pos = positions.reshape(t, 1).astype(F32)
    half_r = RET_DK // 2
    fr = RET_THETA ** (-jnp.arange(half_r, dtype=F32) / half_r)
    half_d = ROT_DIM // 2
    fd = ROPE_THETA ** (-jnp.arange(half_d, dtype=F32) / half_d)
    zeros_d = jnp.zeros((DIFF_HEAD_DIM - ROT_DIM,), F32)
    freq = jnp.concatenate([fr, fr, fd, fd, zeros_d]).reshape(1, LANES)
    sign = jnp.concatenate([-jnp.ones((half_r,), F32), jnp.ones((half_r,), F32),
                            -jnp.ones((half_d,), F32), jnp.ones((half_d,), F32),
                            zeros_d]).reshape(1, LANES)
    tm = min(t, 2048)
    return pl.pallas_call(
        _rope_kernel,
        grid=(t // tm,),
        in_specs=[pl.BlockSpec((tm, 1), lambda i: (i, 0)),
                  pl.BlockSpec((1, LANES), lambda i: (0, 0)),
                  pl.BlockSpec((1, LANES), lambda i: (0, 0))],
        out_specs=[pl.BlockSpec((tm, LANES), lambda i: (i, 0)),
                   pl.BlockSpec((tm, LANES), lambda i: (i, 0))],
        out_shape=[jax.ShapeDtypeStruct((t, LANES), F32)] * 2,
        compiler_params=_cparams(("parallel",)),
        name="rope_tables",
    )(pos, freq, sign)


def _tile_lanes(x, reps):
    return jnp.concatenate([x] * reps, axis=1)


def _rotate_half_split(x, cos, sin_signed, group, half):
    w = x.shape[1]
    lane = lax.broadcasted_iota(jnp.int32, x.shape, 1) % group
    partner = jnp.where(lane < half, pltpu.roll(x, w - half, axis=1), pltpu.roll(x, half, axis=1))
    return x * cos + partner * sin_signed


def _proj_kernel(h_ref, w_ref, z_ref, *, chunk):
    hb = h_ref[...].astype(BF16)
    for n0 in range(0, IN_COLS, chunk):
        z_ref[:, n0:n0 + chunk] = _dot(hb, w_ref[:, n0:n0 + chunk]).astype(BF16)


def _proj_call(h, w_bf16, tm=512):
    t, d = h.shape
    return pl.pallas_call(
        functools.partial(_proj_kernel, chunk=512),
        grid=(t // tm,),
        in_specs=[pl.BlockSpec((tm, d), lambda i: (i, 0)),
                  pl.BlockSpec((d, IN_COLS), lambda i: (0, 0), pipeline_mode=pl.Buffered(1))],
        out_specs=pl.BlockSpec((tm, IN_COLS), lambda i: (i, 0)),
        out_shape=jax.ShapeDtypeStruct((t, IN_COLS), BF16),
        compiler_params=_cparams(("parallel",)),
        name="proj_in",
    )(h, w_bf16)


CONV_HALO = 16


def _conv_kernel(b_ref, c_ref, u_ref, ch_ref, uh_ref, w_ref, o_ref):
    ts = c_ref.shape[0]
    first = pl.program_id(1) == 0
    cu = c_ref[...].astype(F32) * u_ref[...].astype(F32)
    halo = ch_ref[...].astype(F32) * uh_ref[...].astype(F32)
    halo = jnp.where(first, 0.0, halo)
    ext = jnp.concatenate([halo, cu], axis=0)
    n = ext.shape[0]
    prev1 = pltpu.roll(ext, 1, axis=0)[CONV_HALO:n]
    prev2 = pltpu.roll(ext, 2, axis=0)[CONV_HALO:n]
    w = w_ref[...]
    y = prev2 * w[0:1, :] + prev1 * w[1:2, :] + cu * w[2:3, :]
    o_ref[...] = (b_ref[...].astype(F32) * y).astype(BF16)


def _conv_call(z, conv_w, bsz, seq, ts=1024):
    t = bsz * seq
    ts = min(ts, seq)
    nt = seq // ts
    wb = CONV_WIDTH
    hb = ts // CONV_HALO
    col = lambda k: (lambda b, i: (b * nt + i, CONV_OFF // wb + k))
    halo = lambda k: (lambda b, i: (jnp.maximum((b * nt + i) * hb - 1, 0), CONV_OFF // wb + k))
    return pl.pallas_call(
        _conv_kernel,
        grid=(bsz, nt),
        in_specs=[pl.BlockSpec((ts, wb), col(0)),
                  pl.BlockSpec((ts, wb), col(1)),
                  pl.BlockSpec((ts, wb), col(2)),
                  pl.BlockSpec((CONV_HALO, wb), halo(1)),
                  pl.BlockSpec((CONV_HALO, wb), halo(2)),
                  pl.BlockSpec((CONV_K, wb), lambda b, i: (0, 0))],
        out_specs=pl.BlockSpec((ts, wb), lambda b, i: (b * nt + i, 0)),
        out_shape=jax.ShapeDtypeStruct((t, wb), BF16),
        compiler_params=_cparams(("parallel", "parallel")),
        name="conv_mixer",
    )(z, z, z, z, z, conv_w.reshape(CONV_K, wb))


def _sgu_kernel(u_ref, v_ref, g_ref, b_ref, ws_ref, bias_ref, o_ref):
    ts = u_ref.shape[0]
    c = SGU_CHUNK
    v = _layer_norm(v_ref[...].astype(F32), g_ref[...], b_ref[...]).astype(BF16)
    row = lax.broadcasted_iota(jnp.int32, (c, c), 0)
    colm = lax.broadcasted_iota(jnp.int32, (c, c), 1)
    bias = bias_ref[...]
    for g in range(SGU_GROUPS):
        w = jnp.where(row >= colm, ws_ref[g], 0.0).astype(BF16)
        lo = g * SGU_GROUP_DIM
        for n in range(ts // c):
            s = _dot(w, v[n * c:(n + 1) * c, lo:lo + SGU_GROUP_DIM]) + bias[:, lo:lo + SGU_GROUP_DIM]
            u = u_ref[n * c:(n + 1) * c, lo:lo + SGU_GROUP_DIM].astype(F32)
            o_ref[n * c:(n + 1) * c, lo:lo + SGU_GROUP_DIM] = (u * s).astype(BF16)


def _sgu_call(z, ln_g, ln_b, w_s, b_s, ts=512):
    t = z.shape[0]
    ts = min(ts, t)
    wb = SGU_WIDTH
    bias = jnp.repeat(b_s.T, SGU_GROUP_DIM, axis=1)
    return pl.pallas_call(
        _sgu_kernel,
        grid=(t // ts,),
        in_specs=[pl.BlockSpec((ts, wb), lambda i: (i, SGU_OFF // wb)),
                  pl.BlockSpec((ts, wb), lambda i: (i, SGU_OFF // wb + 1)),
                  pl.BlockSpec((1, wb), lambda i: (0, 0)),
                  pl.BlockSpec((1, wb), lambda i: (0, 0)),
                  pl.BlockSpec((SGU_GROUPS, SGU_CHUNK, SGU_CHUNK), lambda i: (0, 0, 0)),
                  pl.BlockSpec((SGU_CHUNK, wb), lambda i: (0, 0))],
        out_specs=pl.BlockSpec((ts, wb), lambda i: (i, 0)),
        out_shape=jax.ShapeDtypeStruct((t, wb), BF16),
        compiler_params=_cparams(("parallel",)),
        name="sgu_mixer",
    )(z, z, ln_g.reshape(1, wb), ln_b.reshape(1, wb), w_s, bias)


def _ret_tables():
    c = RET_CHUNK
    log_gamma = jnp.log1p(-jnp.exp2(-5.0 - jnp.arange(RET_HEADS, dtype=F32)))
    idx = jnp.arange(c, dtype=F32)
    rel = idx[:, None] - idx[None, :]
    decay = jnp.where(rel >= 0, jnp.exp(jnp.maximum(rel, 0.0)[None] * log_gamma[:, None, None]), 0.0)
    k_decay = jnp.exp((c - 1 - idx)[:, None] * log_gamma[None, :])
    q_decay = jnp.exp((idx + 1.0)[:, None] * log_gamma[None, :])
    chunk_decay = jnp.exp(c * log_gamma)
    hk = RET_HEADS * RET_DK
    kd = jnp.repeat(k_decay, RET_DK, axis=1) * (RET_DK ** -0.5)
    qd = jnp.repeat(q_decay, RET_DK, axis=1)
    cd = jnp.broadcast_to(jnp.repeat(chunk_decay, RET_DV)[None, :], (8, RET_HEADS * RET_DV))
    del hk
    return decay, kd, qd, cd


def _ret_kernel(q_ref, k_ref, v_ref, g_ref, cos_ref, sin_ref, decay_ref, kd_ref, qd_ref, cd_ref,
                o_ref, state_ref):
    ts = q_ref.shape[0]
    c = RET_CHUNK
    hk = RET_HEADS * RET_DK

    @pl.when(pl.program_id(1) == 0)
    def _():
        state_ref[...] = jnp.zeros_like(state_ref)

    cos = _tile_lanes(cos_ref[:, 0:RET_DK], RET_HEADS)
    sin = _tile_lanes(sin_ref[:, 0:RET_DK], RET_HEADS)
    q = _rotate_half_split(q_ref[...].astype(F32), cos, sin, RET_DK, RET_DK // 2)
    k = _rotate_half_split(k_ref[...].astype(F32), cos, sin, RET_DK, RET_DK // 2)
    del hk
    for n in range(ts // c):
        r0 = n * c
        qn = q[r0:r0 + c]
        kn = k[r0:r0 + c]
        qb = qn.astype(BF16)
        kb = (kn * (RET_DK ** -0.5)).astype(BF16)
        qdb = (qn * qd_ref[...]).astype(BF16)
        kdb = (kn * kd_ref[...]).astype(BF16)
        for h in range(RET_HEADS):
            ks = slice(h * RET_DK, (h + 1) * RET_DK)
            vs = slice(h * RET_DV, (h + 1) * RET_DV)
            vb = v_ref[r0:r0 + c, vs]
            scores = _dot_nt(qb[:, ks], kb[:, ks]) * decay_ref[h]
            inner = _dot(scores.astype(BF16), vb)
            state = state_ref[h]
            cross = _dot(qdb[:, ks], state.astype(BF16))
            kv = _dot_tn(kdb[:, ks], vb)
            state_ref[h] = state * cd_ref[0:1, vs] + kv
            o = inner + cross
            o = o * lax.rsqrt(jnp.mean(o * o, axis=-1, keepdims=True) + RMS_EPS)
            gate = g_ref[r0:r0 + c, vs].astype(F32)
            gate = gate * jax.nn.sigmoid(gate)
            o_ref[r0:r0 + c, vs] = (gate * o).astype(BF16)


def _ret_call(z, cos_t, sin_t, bsz, seq, ts=512):
    t = bsz * seq
    ts = min(ts, seq)
    nt = seq // ts
    decay, kd, qd, cd = _ret_tables()
    hk = RET_HEADS * RET_DK
    hv = RET_HEADS * RET_DV
    row = lambda b, i: b * nt + i
    return pl.pallas_call(
        _ret_kernel,
        grid=(bsz, nt),
        in_specs=[pl.BlockSpec((ts, hk), lambda b, i: (row(b, i), RET_OFF // hk)),
                  pl.BlockSpec((ts, hk), lambda b, i: (row(b, i), RET_OFF // hk + 1)),
                  pl.BlockSpec((ts, hv), lambda b, i: (row(b, i), (RET_OFF + 2 * hk) // hv)),
                  pl.BlockSpec((ts, hv), lambda b, i: (row(b, i), (RET_OFF + 2 * hk) // hv + 1)),
                  pl.BlockSpec((ts, LANES), lambda b, i: (row(b, i), 0)),
                  pl.BlockSpec((ts, LANES), lambda b, i: (row(b, i), 0)),
                  pl.BlockSpec((RET_HEADS, RET_CHUNK, RET_CHUNK), lambda b, i: (0, 0, 0)),
                  pl.BlockSpec((RET_CHUNK, hk), lambda b, i: (0, 0)),
                  pl.BlockSpec((RET_CHUNK, hk), lambda b, i: (0, 0)),
                  pl.BlockSpec((8, hv), lambda b, i: (0, 0))],
        out_specs=pl.BlockSpec((ts, hv), lambda b, i: (row(b, i), 0)),
        out_shape=jax.ShapeDtypeStruct((t, hv), BF16),
        scratch_shapes=[pltpu.VMEM((RET_HEADS, RET_DK, RET_DV), F32)],
        compiler_params=_cparams(("parallel", "arbitrary")),
        name="ret_mixer",
    )(z, z, z, z, cos_t, sin_t, decay, kd, qd, cd)


LOG2E = 1.4426950408889634


def _qkprep_kernel(q_ref, k_ref, v_ref, cos_ref, sin_ref, qt_ref, ko_ref, vt_ref):
    reps = q_ref.shape[1] // DIFF_HEAD_DIM
    cos = _tile_lanes(cos_ref[:, DIFF_HEAD_DIM:2 * DIFF_HEAD_DIM], reps)
    sin = _tile_lanes(sin_ref[:, DIFF_HEAD_DIM:2 * DIFF_HEAD_DIM], reps)
    q = _rotate_half_split(q_ref[...].astype(F32), cos, sin, DIFF_HEAD_DIM, ROT_DIM // 2)
    k = _rotate_half_split(k_ref[...].astype(F32), cos, sin, DIFF_HEAD_DIM, ROT_DIM // 2)
    qt_ref[...] = (q * (DIFF_HEAD_DIM ** -0.5 * LOG2E)).T.astype(BF16)
    ko_ref[...] = k.astype(BF16)
    vt_ref[...] = v_ref[...].astype(F32).T.astype(BF16)


def _qkprep_call(z, cos_t, sin_t, bsz, seq, tm=512):
    t = bsz * seq
    tm = min(tm, seq)
    nt = seq // tm
    hq = DIFF_HEADS * 2 * DIFF_HEAD_DIM
    row = lambda b, i: b * nt + i
    return pl.pallas_call(
        _qkprep_kernel,
        grid=(bsz, nt),
        in_specs=[pl.BlockSpec((tm, hq), lambda b, i: (row(b, i), DIFF_OFF // hq)),
                  pl.BlockSpec((tm, hq), lambda b, i: (row(b, i), DIFF_OFF // hq + 1)),
                  pl.BlockSpec((tm, hq), lambda b, i: (row(b, i), DIFF_OFF // hq + 2)),
                  pl.BlockSpec((tm, LANES), lambda b, i: (row(b, i), 0)),
                  pl.BlockSpec((tm, LANES), lambda b, i: (row(b, i), 0))],
        out_specs=[pl.BlockSpec((None, hq, tm), lambda b, i: (b, 0, i)),
                   pl.BlockSpec((tm, hq), lambda b, i: (row(b, i), 0)),
                   pl.BlockSpec((None, hq, tm), lambda b, i: (b, 0, i))],
        out_shape=[jax.ShapeDtypeStruct((bsz, hq, seq), BF16),
                   jax.ShapeDtypeStruct((t, hq), BF16),
                   jax.ShapeDtypeStruct((bsz, hq, seq), BF16)],
        compiler_params=_cparams(("parallel", "parallel")),
        name="diff_qkprep",
    )(z, z, z, cos_t, sin_t)


def _flash_kernel(qi_ref, ki_ref, qt_ref, k_ref, vt_ref, lam_ref, g_ref, o_ref,
                  qd_ref, m_ref, l_ref, acc_ref, *, tq, tk, cq, lambda_init):
    p = pl.program_id(2)
    qi = qi_ref[p]
    ki = ki_ref[p]
    last_k = ((qi + 1) * tq - 1) // tk
    hd = DIFF_HEAD_DIM

    @pl.when(ki == 0)
    def _():
        qd_ref[...] = jnp.zeros_like(qd_ref)
        qd_ref[0:hd, 0:tq] = qt_ref[0:hd, :]
        qd_ref[hd:2 * hd, tq:2 * tq] = qt_ref[hd:2 * hd, :]
        m_ref[...] = jnp.full_like(m_ref, NEG_BIG)
        l_ref[...] = jnp.zeros_like(l_ref)
        acc_ref[...] = jnp.zeros_like(acc_ref)

    def step(masked):
        k = k_ref[...]
        vt = vt_ref[...]
        if masked:
            key = lax.broadcasted_iota(jnp.int32, (tk, cq), 0) + ki * tk
            lane = lax.broadcasted_iota(jnp.int32, (tk, cq), 1)
        nc = 2 * tq // cq
        cols = [slice(c * cq, (c + 1) * cq) for c in range(nc)]
        m_prev = m_ref[...]
        l_prev = l_ref[...]
        s = [_dot(k, qd_ref[:, cs]) for cs in cols]
        if masked:
            s = [jnp.where(key <= lane + (qi * tq + (c * cq) % tq), s[c], NEG_BIG) for c in range(nc)]
        m_new = [jnp.maximum(m_prev[:, cols[c]], jnp.max(s[c], axis=0, keepdims=True)) for c in range(nc)]
        pexp = [jnp.exp2(s[c] - m_new[c]) for c in range(nc)]
        alpha = [jnp.exp2(m_prev[:, cols[c]] - m_new[c]) for c in range(nc)]
        for c in range(nc):
            cs = cols[c]
            l_ref[:, cs] = alpha[c] * l_prev[:, cs] + jnp.sum(pexp[c], axis=0, keepdims=True)
            m_ref[:, cs] = m_new[c]
            acc_ref[:, cs] = alpha[c] * acc_ref[:, cs] + _dot(vt, pexp[c].astype(BF16))

    needs_mask = (ki + 1) * tk - 1 > qi * tq

    @pl.when(needs_mask)
    def _():
        step(True)

    @pl.when(jnp.logical_not(needs_mask))
    def _():
        step(False)

    @pl.when(ki == last_k)
    def _():
        lam_p = lam_ref[...]
        lam = (jnp.exp(jnp.sum(lam_p[0:1] * lam_p[1:2], axis=-1, keepdims=True))
               - jnp.exp(jnp.sum(lam_p[2:3] * lam_p[3:4], axis=-1, keepdims=True)) + lambda_init)
        o = acc_ref[...] / l_ref[...]
        o = (o[:, 0:tq] - lam * o[:, tq:2 * tq]).T
        o = o * lax.rsqrt(jnp.mean(o * o, axis=-1, keepdims=True) + RMS_EPS)
        o_ref[...] = (o * g_ref[...] * (1.0 - lambda_init)).astype(BF16)


def _flash_call(qt, kr, vt, lam_params, subln_g, bsz, seq, lambda_init, tq=1024, tk=1024, cq=256):
    t = bsz * seq
    tq = min(tq, seq)
    tk = min(tk, seq)
    cq = min(cq, tq)
    nq = seq // tq
    qi_list, ki_list = [], []
    for qi in range(nq):
        for ki in range(((qi + 1) * tq - 1) // tk + 1):
            qi_list.append(qi)
            ki_list.append(ki)
    qi_arr = jnp.asarray(qi_list, jnp.int32)
    ki_arr = jnp.asarray(ki_list, jnp.int32)
    hd = 2 * DIFF_HEAD_DIM
    grid_spec = pltpu.PrefetchScalarGridSpec(
        num_scalar_prefetch=2,
        grid=(bsz, DIFF_HEADS, len(qi_list)),
        in_specs=[pl.BlockSpec((None, hd, tq), lambda b, h, p, qi, ki: (b, h, qi[p])),
                  pl.BlockSpec((tk, hd), lambda b, h, p, qi, ki: (b * (seq // tk) + ki[p], h)),
                  pl.BlockSpec((None, DIFF_V_DIM, tk), lambda b, h, p, qi, ki: (b, h, ki[p])),
                  pl.BlockSpec((4, DIFF_HEAD_DIM), lambda b, h, p, qi, ki: (0, 0)),
                  pl.BlockSpec((1, DIFF_V_DIM), lambda b, h, p, qi, ki: (0, 0))],
        out_specs=pl.BlockSpec((tq, DIFF_V_DIM), lambda b, h, p, qi, ki: (b * (seq // tq) + qi[p], h)),
        scratch_shapes=[pltpu.VMEM((hd, 2 * tq), BF16),
                        pltpu.VMEM((1, 2 * tq), F32),
                        pltpu.VMEM((1, 2 * tq), F32),
                        pltpu.VMEM((DIFF_V_DIM, 2 * tq), F32)],
    )
    return pl.pallas_call(
        functools.partial(_flash_kernel, tq=tq, tk=tk, cq=cq, lambda_init=lambda_init),
        grid_spec=grid_spec,
        out_shape=jax.ShapeDtypeStruct((t, DIFF_HEADS * DIFF_V_DIM), BF16),
        compiler_params=_cparams(("parallel", "parallel", "arbitrary")),
        name="diff_flash",
    )(qi_arr, ki_arr, qt, kr, vt, lam_params, subln_g.reshape(1, DIFF_V_DIM))


def _merge_kernel(h_ref, y0_ref, y1_ref, y2_ref, y3_ref, wg_ref, bg_ref, wb_ref, wo_ref,
                  g_ref, b_ref, o_ref, *, alpha):
    h = h_ref[...]
    hb = h.astype(BF16)
    merged = None
    for g, y_ref in enumerate((y0_ref, y1_ref, y2_ref, y3_ref)):
        gate = jax.nn.sigmoid(_dot(hb, wg_ref[g]) + bg_ref[g:g + 1, :])
        term = gate * _dot(y_ref[...], wb_ref[g])
        merged = term if merged is None else merged + term
    t = _dot(merged.astype(BF16), wo_ref[...])
    o_ref[...] = _layer_norm(alpha * h + t, g_ref[...], b_ref[...])


def _merge_call(h, ys, wg, bg, wb, wo, ln_g, ln_b, alpha, tm=512):
    t, d = h.shape
    tm = min(tm, t)
    w = BRANCH_WIDTH
    const = dict(pipeline_mode=pl.Buffered(1))
    return pl.pallas_call(
        functools.partial(_merge_kernel, alpha=alpha),
        grid=(t // tm,),
        in_specs=[pl.BlockSpec((tm, d), lambda i: (i, 0))]
        + [pl.BlockSpec((tm, w), lambda i: (i, 0))] * N_BRANCH
        + [pl.BlockSpec((N_BRANCH, d, d), lambda i: (0, 0, 0), **const),
           pl.BlockSpec((N_BRANCH, d), lambda i: (0, 0)),
           pl.BlockSpec((N_BRANCH, w, d), lambda i: (0, 0, 0), **const),
           pl.BlockSpec((d, d), lambda i: (0, 0), **const),
           pl.BlockSpec((1, d), lambda i: (0, 0)),
           pl.BlockSpec((1, d), lambda i: (0, 0))],
        out_specs=pl.BlockSpec((tm, d), lambda i: (i, 0)),
        out_shape=jax.ShapeDtypeStruct((t, d), F32),
        compiler_params=_cparams(("parallel",)),
        name="merge",
    )(h, *ys, wg, bg, wb, wo, ln_g.reshape(1, d), ln_b.reshape(1, d))


def _route_kernel(h_ref, wr_ref, br_ref, idx_ref, gw_ref):
    logits = lax.dot_general(wr_ref[...], h_ref[...], (((1,), (1,)), ((), ())),
                             preferred_element_type=F32, precision=lax.Precision.HIGHEST)
    mx = jnp.max(logits, axis=0, keepdims=True)
    e = jnp.exp(logits - mx)
    scores = e / jnp.sum(e, axis=0, keepdims=True)
    sel = scores + br_ref[...]
    epg = EXPERTS_PER_GROUP
    rows = [sel[i:i + 1, :] for i in range(N_EXPERTS)]
    srows = [scores[i:i + 1, :] for i in range(N_EXPERTS)]
    best_score = None
    best = None
    for g in range(N_GROUPS):
        r = rows[g * epg:(g + 1) * epg]
        gs = None
        for a in range(epg):
            for b in range(a + 1, epg):
                pair = r[a] + r[b]
                gs = pair if gs is None else jnp.maximum(gs, pair)
        if best is None:
            best_score, best = gs, jnp.zeros(gs.shape, jnp.int32)
        else:
            take = gs > best_score
            best_score = jnp.where(take, gs, best_score)
            best = jnp.where(take, g, best)
    cand, cscore = [], []
    for j in range(epg):
        c = rows[j]
        s = srows[j]
        for g in range(1, N_GROUPS):
            c = jnp.where(best == g, rows[g * epg + j], c)
            s = jnp.where(best == g, srows[g * epg + j], s)
        cand.append(c)
        cscore.append(s)
    v1, i1, s1 = cand[0], jnp.zeros(best.shape, jnp.int32), cscore[0]
    for j in range(1, epg):
        take = cand[j] > v1
        v1 = jnp.where(take, cand[j], v1)
        i1 = jnp.where(take, j, i1)
        s1 = jnp.where(take, cscore[j], s1)
    v2 = jnp.full(v1.shape, -jnp.inf, F32)
    i2 = jnp.zeros(best.shape, jnp.int32)
    s2 = jnp.zeros(v1.shape, F32)
    for j in range(epg):
        take = jnp.logical_and(i1 != j, cand[j] > v2)
        v2 = jnp.where(take, cand[j], v2)
        i2 = jnp.where(take, j, i2)
        s2 = jnp.where(take, cscore[j], s2)
    denom = s1 + s2
    idx_ref[0:1, :] = best * epg + i1
    idx_ref[1:2, :] = best * epg + i2
    gw_ref[0:1, :] = s1 / denom
    gw_ref[1:2, :] = s2 / denom


def _route_call(h, w_router, b_router, tm=1024):
    t, d = h.shape
    tm = min(tm, t)
    return pl.pallas_call(
        _route_kernel,
        grid=(t // tm,),
        in_specs=[pl.BlockSpec((tm, d), lambda i: (i, 0)),
                  pl.BlockSpec((N_EXPERTS, d), lambda i: (0, 0)),
                  pl.BlockSpec((N_EXPERTS, 1), lambda i: (0, 0))],
        out_specs=[pl.BlockSpec((TOP_K, tm), lambda i: (0, i)),
                   pl.BlockSpec((TOP_K, tm), lambda i: (0, i))],
        out_shape=[jax.ShapeDtypeStruct((TOP_K, t), jnp.int32),
                   jax.ShapeDtypeStruct((TOP_K, t), F32)],
        compiler_params=_cparams(("parallel",)),
        name="route",
    )(h, w_router.T, b_router.reshape(N_EXPERTS, 1))


def _rank_kernel(idx_ref, dest_ref, cnt_ref, run_ref, start_ref, *, blk):
    phase = pl.program_id(0)
    i = pl.program_id(1)
    tm = idx_ref.shape[1]
    expert = lax.broadcasted_iota(jnp.int32, (N_EXPERTS, tm), 0)
    oh0 = expert == idx_ref[0:1, :]
    oh1 = expert == idx_ref[1:2, :]
    f0 = jnp.where(oh0, 1.0, 0.0)
    f1 = jnp.where(oh1, 1.0, 0.0)

    @pl.when(jnp.logical_and(phase == 0, i == 0))
    def _():
        run_ref[...] = jnp.zeros_like(run_ref)

    @pl.when(phase == 0)
    def _():
        tot = jnp.sum(f0 + f1, axis=1, keepdims=True)
        run_ref[...] = run_ref[...] + tot

    @pl.when(jnp.logical_and(phase == 1, i == 0))
    def _():
        cnt = run_ref[...]
        cnt_ref[...] = cnt.astype(jnp.int32)
        padded = jnp.ceil(cnt * (1.0 / blk)) * blk
        acc = jnp.zeros((1, LANES), F32)
        for e in range(N_EXPERTS):
            start_ref[e:e + 1, :] = acc
            acc = acc + padded[e:e + 1, :]
        run_ref[...] = jnp.zeros_like(run_ref)

    @pl.when(phase == 1)
    def _():
        s_idx = lax.broadcasted_iota(jnp.int32, (tm, tm), 0)
        t_idx = lax.broadcasted_iota(jnp.int32, (tm, tm), 1)
        tri = jnp.where(s_idx < t_idx, 1.0, 0.0).astype(BF16)
        c0 = _dot(f0.astype(BF16), tri)
        c1 = _dot(f1.astype(BF16), tri)
        tot0 = jnp.sum(f0, axis=1, keepdims=True)
        tot1 = jnp.sum(f1, axis=1, keepdims=True)
        base = start_ref[:, 0:1] + run_ref[:, 0:1]
        d0 = jnp.sum(jnp.where(oh0, base + c0, 0.0), axis=0, keepdims=True)
        d1 = jnp.sum(jnp.where(oh1, base + tot0 + c1, 0.0), axis=0, keepdims=True)
        dest_ref[0:1, :] = d0.astype(jnp.int32)
        dest_ref[1:2, :] = d1.astype(jnp.int32)
        run_ref[...] = run_ref[...] + (tot0 + tot1)


def _rank_call(idx, blk, tm=512):
    t = idx.shape[1]
    tm = min(tm, t)
    return pl.pallas_call(
        functools.partial(_rank_kernel, blk=blk),
        grid=(2, t // tm),
        in_specs=[pl.BlockSpec((TOP_K, tm), lambda p, i: (0, i))],
        out_specs=[pl.BlockSpec((TOP_K, tm), lambda p, i: (0, i * p)),
                   pl.BlockSpec((N_EXPERTS, LANES), lambda p, i: (0, 0))],
        out_shape=[jax.ShapeDtypeStruct((TOP_K, t), jnp.int32),
                   jax.ShapeDtypeStruct((N_EXPERTS, LANES), jnp.int32)],
        scratch_shapes=[pltpu.VMEM((N_EXPERTS, LANES), F32),
                        pltpu.VMEM((N_EXPERTS, LANES), F32)],
        compiler_params=_cparams(("arbitrary", "arbitrary")),
        name="moe_rank",
    )(idx)


def _experts_kernel(dest_ref, be_ref, ve_ref, pe_ref, nb_ref, h_ref, w1_ref, w3_ref, w2_ref, y_ref,
                    inv_ref, xbuf0, xbuf1, obuf0, obuf1, wb1, wb3, wb2, gsem, ssem, *, blk, t, n_blk):
    i = pl.program_id(0)
    nb = nb_ref[0]
    xbuf = (xbuf0, xbuf1)
    obuf = (obuf0, obuf1)
    dump0 = TOP_K * t

    def gather(block, s):
        base = block * blk
        for j in range(blk):
            tok = inv_ref[base + j] & (t - 1)
            pltpu.make_async_copy(h_ref.at[pl.ds(tok, 1), :], xbuf[s].at[pl.ds(j, 1), :],
                                  gsem.at[s]).start()

    def wait_gather(s):
        for j in range(blk):
            pltpu.make_async_copy(h_ref.at[pl.ds(0, 1), :], xbuf[s].at[pl.ds(j, 1), :],
                                  gsem.at[s]).wait()

    def scatter(block, s):
        base = block * blk
        for j in range(blk):
            pltpu.make_async_copy(obuf[s].at[pl.ds(j, 1), :], y_ref.at[pl.ds(inv_ref[base + j], 1), :],
                                  ssem.at[s]).start()

    def wait_scatter(s):
        for j in range(blk):
            pltpu.make_async_copy(obuf[s].at[pl.ds(j, 1), :], y_ref.at[pl.ds(0, 1), :],
                                  ssem.at[s]).wait()

    @pl.when(i == 0)
    def _():
        def fill(a, carry):
            inv_ref[dest_ref[a]] = a
            return carry

        lax.fori_loop(0, TOP_K * t, fill, 0, unroll=8)

        def pad(r, carry):
            inv_ref[r] = dump0 + (r & (2 * blk - 1))
            return carry

        for e in range(N_EXPERTS):
            lax.fori_loop(ve_ref[e], pe_ref[e], pad, 0)

        def dummy(j, carry):
            inv_ref[n_blk * blk + j] = dump0 + 2 * blk + j
            return carry

        lax.fori_loop(0, blk, dummy, 0)
        obuf1[...] = jnp.zeros_like(obuf1)
        for q in range(3):
            zero_dump = pltpu.make_async_copy(obuf1, y_ref.at[pl.ds(dump0 + q * blk, blk), :], ssem.at[0])
            zero_dump.start()
            zero_dump.wait()
        gather(0, 0)

    def block_step(s):
        wait_gather(s)

        @pl.when(i > 0)
        def _():
            wait_scatter(s)

        gather(jnp.minimum(i + 1, nb - 1), 1 - s)
        scatter(jnp.where(i == 0, n_blk, i - 1), 1 - s)
        xb = xbuf[s][...].astype(BF16)
        a = _dot(xb, wb1[...])
        b = _dot(xb, wb3[...])
        act = (a * jax.nn.sigmoid(a) * b).astype(BF16)
        obuf[s][...] = _dot(act, wb2[...])

        @pl.when(i == nb - 1)
        def _():
            wait_gather(1 - s)
            wait_scatter(1 - s)
            scatter(i, s)
            wait_scatter(s)

    new_expert = jnp.logical_or(i == 0, be_ref[i] != be_ref[jnp.maximum(i - 1, 0)])

    @pl.when(jnp.logical_and(i < nb, new_expert))
    def _():
        wb1[...] = w1_ref[0].astype(BF16)
        wb3[...] = w3_ref[0].astype(BF16)
        wb2[...] = w2_ref[0].astype(BF16)

    for s in range(2):
        @pl.when(jnp.logical_and(i < nb, i % 2 == s))
        def _(s=s):
            block_step(s)


def _experts_call(h, dest_flat, blk_expert, valid_end, pad_end, n_used, w1, w3, w2, layer, blk):
    t, d = h.shape
    de = w1.shape[3]
    n_blk = blk_expert.shape[0]
    assert t & (t - 1) == 0 and blk & (blk - 1) == 0, "token count and block size must be powers of two"
    wmap = lambda i, dest, be, ve, pe, nb: (layer, be[i], 0, 0)
    grid_spec = pltpu.PrefetchScalarGridSpec(
        num_scalar_prefetch=5,
        grid=(n_blk,),
        in_specs=[pl.BlockSpec(memory_space=pl.ANY),
                  pl.BlockSpec((None, 1, d, de), wmap),
                  pl.BlockSpec((None, 1, d, de), wmap),
                  pl.BlockSpec((None, 1, de, d), wmap)],
        out_specs=pl.BlockSpec(memory_space=pl.ANY),
        scratch_shapes=[pltpu.SMEM(((n_blk + 1) * blk,), jnp.int32),
                        pltpu.VMEM((blk, d), F32),
                        pltpu.VMEM((blk, d), F32),
                        pltpu.VMEM((blk, d), F32),
                        pltpu.VMEM((blk, d), F32),
                        pltpu.VMEM((d, de), BF16),
                        pltpu.VMEM((d, de), BF16),
                        pltpu.VMEM((de, d), BF16),
                        pltpu.SemaphoreType.DMA((2,)),
                        pltpu.SemaphoreType.DMA((2,))],
    )
    return pl.pallas_call(
        functools.partial(_experts_kernel, blk=blk, t=t, n_blk=n_blk),
        grid_spec=grid_spec,
        out_shape=jax.ShapeDtypeStruct((TOP_K * t + 3 * blk, d), F32),
        compiler_params=_cparams(("arbitrary",)),
        name="moe_experts",
    )(dest_flat, blk_expert, valid_end, pad_end, n_used, h, w1, w3, w2)


def _combine_kernel(h_ref, gw_ref, y0_ref, y1_ref, g_ref, b_ref, o_ref, *, alpha):
    gw = gw_ref[...]
    y = y0_ref[...] * gw[:, 0:1] + y1_ref[...] * gw[:, 1:2]
    o_ref[...] = _layer_norm(alpha * h_ref[...] + y, g_ref[...], b_ref[...])


def _combine_call(h, gw_t, y, ln_g, ln_b, alpha, tm=512):
    t, d = h.shape
    tm = min(tm, t)
    nt = t // tm
    return pl.pallas_call(
        functools.partial(_combine_kernel, alpha=alpha),
        grid=(nt,),
        in_specs=[pl.BlockSpec((tm, d), lambda i: (i, 0)),
                  pl.BlockSpec((tm, TOP_K), lambda i: (i, 0)),
                  pl.BlockSpec((tm, d), lambda i: (i, 0)),
                  pl.BlockSpec((tm, d), lambda i: (nt + i, 0)),
                  pl.BlockSpec((1, d), lambda i: (0, 0)),
                  pl.BlockSpec((1, d), lambda i: (0, 0))],
        out_specs=pl.BlockSpec((tm, d), lambda i: (i, 0)),
        out_shape=jax.ShapeDtypeStruct((t, d), F32),
        compiler_params=_cparams(("parallel",)),
        name="moe_combine",
    )(h, gw_t, y, y, ln_g.reshape(1, d), ln_b.reshape(1, d))


def _moe_layer(h, w_router, b_router, w1, w3, w2, layer, ln_g, ln_b, alpha, blk=256):
    t, d = h.shape
    idx, gw = _route_call(h, w_router, b_router)
    dest, counts = _rank_call(idx, blk)
    n_blk = (TOP_K * t) // blk + N_EXPERTS
    cnt = counts[:, 0]
    padded = ((cnt + blk - 1) // blk) * blk
    pad_end = jnp.cumsum(padded)
    blk_start = jnp.arange(n_blk, dtype=jnp.int32) * blk
    blk_expert = jnp.minimum(jnp.sum(pad_end[None, :] <= blk_start[:, None], axis=1),
                             N_EXPERTS - 1).astype(jnp.int32)
    valid_end = (pad_end - padded + cnt).astype(jnp.int32)
    n_used = (pad_end[N_EXPERTS - 1:] // blk).astype(jnp.int32)
    y = _experts_call(h, dest.reshape(-1), blk_expert, valid_end, pad_end.astype(jnp.int32), n_used,
                      w1, w3, w2, layer, blk)
    return _combine_call(h, gw.T, y, ln_g, ln_b, alpha)


def kernel(x, positions, ln_in_g, ln_in_b, w_in, conv_w, sgu_ln_g, sgu_ln_b, w_s, b_s, lambda_q1, lambda_k1, lambda_q2, lambda_k2, diff_subln_g, w_gate, b_gate, w_branch, w_o, ln1_g, ln1_b, w_router, b_router, w1, w3, w2, ln2_g, ln2_b):
    bsz, seq, d = x.shape
    depth = w_in.shape[0]
    alpha = (2.0 * depth) ** 0.25
    t = bsz * seq
    cos_t, sin_t = _rope_tables(positions)
    h = _ln_call(x.reshape(t, d), ln_in_g, ln_in_b)
    for l in range(depth):
        lambda_init = 0.8 - 0.6 * math.exp(-0.3 * l)
        z = _proj_call(h, w_in[l].astype(BF16))
        y_conv = _conv_call(z, conv_w[l], bsz, seq)
        y_ret = _ret_call(z, cos_t, sin_t, bsz, seq)
        y_sgu = _sgu_call(z, sgu_ln_g[l], sgu_ln_b[l], w_s[l], b_s[l])
        qt, kr, vt = _qkprep_call(z, cos_t, sin_t, bsz, seq)
        lam_params = jnp.stack([lambda_q1[l], lambda_k1[l], lambda_q2[l], lambda_k2[l]])
        y_diff = _flash_call(qt, kr, vt, lam_params, diff_subln_g[l], bsz, seq, lambda_init)
        h = _merge_call(h, (y_conv, y_ret, y_sgu, y_diff), w_gate[l].astype(BF16), b_gate[l],
                        w_branch[l].astype(BF16), w_o[l].astype(BF16), ln1_g[l], ln1_b[l], alpha)
        h = _moe_layer(h, w_router, b_router, w1, w3, w2, l, ln2_g[l], ln2_b[l], alpha)
    return h.reshape(bsz, seq, d)
```

```python
import functools
import math

import jax
import jax.numpy as jnp
from jax import lax
from jax.experimental import pallas as pl
from jax.experimental.pallas import tpu as pltpu

D_MODEL = 1024
BRANCH_WIDTH = 512
N_BRANCH = 4
CONV_WIDTH = BRANCH_WIDTH
CONV_K = 3
RET_HEADS = 4
RET_DK = 64
RET_DV = 128
RET_CHUNK = 128
RET_THETA = 10000.0
SGU_GROUPS = 4
SGU_GROUP_DIM = BRANCH_WIDTH // SGU_GROUPS
SGU_CHUNK = 128
SGU_WIDTH = BRANCH_WIDTH
DIFF_HEADS = 4
DIFF_HEAD_DIM = 64
DIFF_V_DIM = 2 * DIFF_HEAD_DIM
ROPE_THETA = 500000.0
ROT_DIM = DIFF_HEAD_DIM // 4
CONV_COLS = 3 * CONV_WIDTH
RET_COLS = 2 * RET_HEADS * RET_DK + 2 * RET_HEADS * RET_DV
SGU_COLS = 2 * SGU_WIDTH
DIFF_COLS = 2 * DIFF_HEADS * 2 * DIFF_HEAD_DIM + DIFF_HEADS * DIFF_V_DIM
IN_COLS = CONV_COLS + RET_COLS + SGU_COLS + DIFF_COLS
N_EXPERTS = 16
N_GROUPS = 4
EXPERTS_PER_GROUP = N_EXPERTS // N_GROUPS
TOP_K = 2
D_EXPERT = 1024
LN_EPS = 1e-5
RMS_EPS = 1e-6

LANES = 128
CONV_OFF = 0
RET_OFF = CONV_COLS
SGU_OFF = RET_OFF + RET_COLS
DIFF_OFF = SGU_OFF + SGU_COLS

NEG_BIG = -1e30
VMEM_LIMIT = 56 * 1024 * 1024

BF16 = jnp.bfloat16
F32 = jnp.float32


def _cparams(sem):
    return pltpu.CompilerParams(dimension_semantics=sem, vmem_limit_bytes=VMEM_LIMIT)


def _layer_norm(xf, g, b):
    mu = jnp.mean(xf, axis=-1, keepdims=True)
    xc = xf - mu
    var = jnp.mean(xc * xc, axis=-1, keepdims=True)
    return xc * lax.rsqrt(var + LN_EPS) * g + b


def _dot(a, b):
    return jnp.dot(a, b, preferred_element_type=F32)


def _dot_nt(a, b):
    return lax.dot_general(a, b, (((1,), (1,)), ((), ())), preferred_element_type=F32)


def _dot_tn(a, b):
    return lax.dot_general(a, b, (((0,), (0,)), ((), ())), preferred_element_type=F32)


def _ln_kernel(x_ref, g_ref, b_ref, o_ref):
    o_ref[...] = _layer_norm(x_ref[...], g_ref[...], b_ref[...])


def _ln_call(x2d, g, b, tm=1024):
    t, d = x2d.shape
    return pl.pallas_call(
        _ln_kernel,
        grid=(t // tm,),
        in_specs=[pl.BlockSpec((tm, d), lambda i: (i, 0)),
                  pl.BlockSpec((1, d), lambda i: (0, 0)),
                  pl.BlockSpec((1, d), lambda i: (0, 0))],
        out_specs=pl.BlockSpec((tm, d), lambda i: (i, 0)),
        out_shape=jax.ShapeDtypeStruct((t, d), F32),
        compiler_params=_cparams(("parallel",)),
        name="ln_in",
    )(x2d, g.reshape(1, d), b.reshape(1, d))


def _rope_kernel(pos_ref, freq_ref, sign_ref, cos_ref, sin_ref):
    ang = pos_ref[...] * freq_ref[...]
    cos_ref[...] = jnp.cos(ang)
    sin_ref[...] = jnp.sin(ang) * sign_ref[...]


def _rope_tables(positions):
    t = positions.size
    pos = positions.reshape(t, 1).astype(F32)
    half_r = RET_DK // 2
    fr = RET_THETA ** (-jnp.arange(half_r, dtype=F32) / half_r)
    half_d = ROT_DIM // 2
    fd = ROPE_THETA ** (-jnp.arange(half_d, dtype=F32) / half_d)
    zeros_d = jnp.zeros((DIFF_HEAD_DIM - ROT_DIM,), F32)
    freq = jnp.concatenate([fr, fr, fd, fd, zeros_d]).reshape(1, LANES)
    sign = jnp.concatenate([-jnp.ones((half_r,), F32), jnp.ones((half_r,), F32),
                            -jnp.ones((half_d,), F32), jnp.ones((half_d,), F32),
                            zeros_d]).reshape(1, LANES)
    tm = min(t, 2048)
    return pl.pallas_call(
        _rope_kernel,
        grid=(t // tm,),
        in_specs=[pl.BlockSpec((tm, 1), lambda i: (i, 0)),
                  pl.BlockSpec((1, LANES), lambda i: (0, 0)),
                  pl.BlockSpec((1, LANES), lambda i: (0, 0))],
        out_specs=[pl.BlockSpec((tm, LANES), lambda i: (i, 0)),
                   pl.BlockSpec((tm, LANES), lambda i: (i, 0))],
        out_shape=[jax.ShapeDtypeStruct((t, LANES), F32)] * 2,
        compiler_params=_cparams(("parallel",)),
        name="rope_tables",
    )(pos, freq, sign)


def _tile_lanes(x, reps):
    return jnp.concatenate([x] * reps, axis=1)


def _rotate_half_split(x, cos, sin_signed, group, half):
    w = x.shape[1]
    lane = lax.broadcasted_iota(jnp.int32, x.shape, 1) % group
    partner = jnp.where(lane < half, pltpu.roll(x, w - half, axis=1), pltpu.roll(x, half, axis=1))
    return x * cos + partner * sin_signed


def _proj_kernel(h_ref, w_ref, z_ref, *, chunk):
    hb = h_ref[...].astype(BF16)
    for n0 in range(0, IN_COLS, chunk):
        z_ref[:, n0:n0 + chunk] = _dot(hb, w_ref[:, n0:n0 + chunk]).astype(BF16)


def _proj_call(h, w_bf16, tm=512):
    t, d = h.shape
    return pl.pallas_call(
        functools.partial(_proj_kernel, chunk=512),
        grid=(t // tm,),
        in_specs=[pl.BlockSpec((tm, d), lambda i: (i, 0)),
                  pl.BlockSpec((d, IN_COLS), lambda i: (0, 0), pipeline_mode=pl.Buffered(1))],
        out_specs=pl.BlockSpec((tm, IN_COLS), lambda i: (i, 0)),
        out_shape=jax.ShapeDtypeStruct((t, IN_COLS), BF16),
        compiler_params=_cparams(("parallel",)),
        name="proj_in",
    )(h, w_bf16)


CONV_HALO = 16


def _conv_kernel(b_ref, c_ref, u_ref, ch_ref, uh_ref, w_ref, o_ref):
    ts = c_ref.shape[0]
    first = pl.program_id(1) == 0
    cu = c_ref[...].astype(F32) * u_ref[...].astype(F32)
    halo = ch_ref[...].astype(F32) * uh_ref[...].astype(F32)
    halo = jnp.where(first, 0.0, halo)
    ext = jnp.concatenate([halo, cu], axis=0)
    n = ext.shape[0]
    prev1 = pltpu.roll(ext, 1, axis=0)[CONV_HALO:n]
    prev2 = pltpu.roll(ext, 2, axis=0)[CONV_HALO:n]
    w = w_ref[...]
    y = prev2 * w[0:1, :] + prev1 * w[1:2, :] + cu * w[2:3, :]
    o_ref[...] = (b_ref[...].astype(F32) * y).astype(BF16)


def _conv_call(z, conv_w, bsz, seq, ts=1024):
    t = bsz * seq
    ts = min(ts, seq)
    nt = seq // ts
    wb = CONV_WIDTH
    hb = ts // CONV_HALO
    col = lambda k: (lambda b, i: (b * nt + i, CONV_OFF // wb + k))
    halo = lambda k: (lambda b, i: (jnp.maximum((b * nt + i) * hb - 1, 0), CONV_OFF // wb + k))
    return pl.pallas_call(
        _conv_kernel,
        grid=(bsz, nt),
        in_specs=[pl.BlockSpec((ts, wb), col(0)),
                  pl.BlockSpec((ts, wb), col(1)),
                  pl.BlockSpec((ts, wb), col(2)),
                  pl.BlockSpec((CONV_HALO, wb), halo(1)),
                  pl.BlockSpec((CONV_HALO, wb), halo(2)),
                  pl.BlockSpec((CONV_K, wb), lambda b, i: (0, 0))],
        out_specs=pl.BlockSpec((ts, wb), lambda b, i: (b * nt + i, 0)),
        out_shape=jax.ShapeDtypeStruct((t, wb), BF16),
        compiler_params=_cparams(("parallel", "parallel")),
        name="conv_mixer",
    )(z, z, z, z, z, conv_w.reshape(CONV_K, wb))


def _sgu_kernel(u_ref, v_ref, g_ref, b_ref, ws_ref, bias_ref, o_ref):
    ts = u_ref.shape[0]
    c = SGU_CHUNK
    v = _layer_norm(v_ref[...].astype(F32), g_ref[...], b_ref[...]).astype(BF16)
    row = lax.broadcasted_iota(jnp.int32, (c, c), 0)
    colm = lax.broadcasted_iota(jnp.int32, (c, c), 1)
    bias = bias_ref[...]
    for g in range(SGU_GROUPS):
        w = jnp.where(row >= colm, ws_ref[g], 0.0).astype(BF16)
        lo = g * SGU_GROUP_DIM
        for n in range(ts // c):
            s = _dot(w, v[n * c:(n + 1) * c, lo:lo + SGU_GROUP_DIM]) + bias[:, lo:lo + SGU_GROUP_DIM]
            u = u_ref[n * c:(n + 1) * c, lo:lo + SGU_GROUP_DIM].astype(F32)
            o_ref[n * c:(n + 1) * c, lo:lo + SGU_GROUP_DIM] = (u * s).astype(BF16)


def _sgu_call(z, ln_g, ln_b, w_s, b_s, ts=512):
    t = z.shape[0]
    ts = min(ts, t)
    wb = SGU_WIDTH
    bias = jnp.repeat(b_s.T, SGU_GROUP_DIM, axis=1)
    return pl.pallas_call(
        _sgu_kernel,
        grid=(t // ts,),
        in_specs=[pl.BlockSpec((ts, wb), lambda i: (i, SGU_OFF // wb)),
                  pl.BlockSpec((ts, wb), lambda i: (i, SGU_OFF // wb + 1)),
                  pl.BlockSpec((1, wb), lambda i: (0, 0)),
                  pl.BlockSpec((1, wb), lambda i: (0, 0)),
                  pl.BlockSpec((SGU_GROUPS, SGU_CHUNK, SGU_CHUNK), lambda i: (0, 0, 0)),
                  pl.BlockSpec((SGU_CHUNK, wb), lambda i: (0, 0))],
        out_specs=pl.BlockSpec((ts, wb), lambda i: (i, 0)),
        out_shape=jax.ShapeDtypeStruct((t, wb), BF16),
        compiler_params=_cparams(("parallel",)),
        name="sgu_mixer",
    )(z, z, ln_g.reshape(1, wb), ln_b.reshape(1, wb), w_s, bias)


def _ret_tables():
    c = RET_CHUNK
    log_gamma = jnp.log1p(-jnp.exp2(-5.0 - jnp.arange(RET_HEADS, dtype=F32)))
    idx = jnp.arange(c, dtype=F32)
    rel = idx[:, None] - idx[None, :]
    decay = jnp.where(rel >= 0, jnp.exp(jnp.maximum(rel, 0.0)[None] * log_gamma[:, None, None]), 0.0)
    k_decay = jnp.exp((c - 1 - idx)[:, None] * log_gamma[None, :])
    q_decay = jnp.exp((idx + 1.0)[:, None] * log_gamma[None, :])
    chunk_decay = jnp.exp(c * log_gamma)
    hk = RET_HEADS * RET_DK
    kd = jnp.repeat(k_decay, RET_DK, axis=1) * (RET_DK ** -0.5)
    qd = jnp.repeat(q_decay, RET_DK, axis=1)
    cd = jnp.broadcast_to(jnp.repeat(chunk_decay, RET_DV)[None, :], (8, RET_HEADS * RET_DV))
    del hk
    return decay, kd, qd, cd


def _ret_kernel(q_ref, k_ref, v_ref, g_ref, cos_ref, sin_ref, decay_ref, kd_ref, qd_ref, cd_ref,
                o_ref, state_ref):
    ts = q_ref.shape[0]
    c = RET_CHUNK
    hk = RET_HEADS * RET_DK

    @pl.when(pl.program_id(1) == 0)
    def _():
        state_ref[...] = jnp.zeros_like(state_ref)

    cos = _tile_lanes(cos_ref[:, 0:RET_DK], RET_HEADS)
    sin = _tile_lanes(sin_ref[:, 0:RET_DK], RET_HEADS)
    q = _rotate_half_split(q_ref[...].astype(F32), cos, sin, RET_DK, RET_DK // 2)
    k = _rotate_half_split(k_ref[...].astype(F32), cos, sin, RET_DK, RET_DK // 2)
    del hk
    for n in range(ts // c):
        r0 = n * c
        qn = q[r0:r0 + c]
        kn = k[r0:r0 + c]
        qb = qn.astype(BF16)
        kb = (kn * (RET_DK ** -0.5)).astype(BF16)
        qdb = (qn * qd_ref[...]).astype(BF16)
        kdb = (kn * kd_ref[...]).astype(BF16)
        for h in range(RET_HEADS):
            ks = slice(h * RET_DK, (h + 1) * RET_DK)
            vs = slice(h * RET_DV, (h + 1) * RET_DV)
            vb = v_ref[r0:r0 + c, vs]
            scores = _dot_nt(qb[:, ks], kb[:, ks]) * decay_ref[h]
            inner = _dot(scores.astype(BF16), vb)
            state = state_ref[h]
            cross = _dot(qdb[:, ks], state.astype(BF16))
            kv = _dot_tn(kdb[:, ks], vb)
            state_ref[h] = state * cd_ref[0:1, vs] + kv
            o = inner + cross
            o = o * lax.rsqrt(jnp.mean(o * o, axis=-1, keepdims=True) + RMS_EPS)
            gate = g_ref[r0:r0 + c, vs].astype(F32)
            gate = gate * jax.nn.sigmoid(gate)
            o_ref[r0:r0 + c, vs] = (gate * o).astype(BF16)


def _ret_call(z, cos_t, sin_t, bsz, seq, ts=512):
    t = bsz * seq
    ts = min(ts, seq)
    nt = seq // ts
    decay, kd, qd, cd = _ret_tables()
    hk = RET_HEADS * RET_DK
    hv = RET_HEADS * RET_DV
    row = lambda b, i: b * nt + i
    return pl.pallas_call(
        _ret_kernel,
        grid=(bsz, nt),
        in_specs=[pl.BlockSpec((ts, hk), lambda b, i: (row(b, i), RET_OFF // hk)),
                  pl.BlockSpec((ts, hk), lambda b, i: (row(b, i), RET_OFF // hk + 1)),
                  pl.BlockSpec((ts, hv), lambda b, i: (row(b, i), (RET_OFF + 2 * hk) // hv)),
                  pl.BlockSpec((ts, hv), lambda b, i: (row(b, i), (RET_OFF + 2 * hk) // hv + 1)),
                  pl.BlockSpec((ts, LANES), lambda b, i: (row(b, i), 0)),
                  pl.BlockSpec((ts, LANES), lambda b, i: (row(b, i), 0)),
                  pl.BlockSpec((RET_HEADS, RET_CHUNK, RET_CHUNK), lambda b, i: (0, 0, 0)),
                  pl.BlockSpec((RET_CHUNK, hk), lambda b, i: (0, 0)),
                  pl.BlockSpec((RET_CHUNK, hk), lambda b, i: (0, 0)),
                  pl.BlockSpec((8, hv), lambda b, i: (0, 0))],
        out_specs=pl.BlockSpec((ts, hv), lambda b, i: (row(b, i), 0)),
        out_shape=jax.ShapeDtypeStruct((t, hv), BF16),
        scratch_shapes=[pltpu.VMEM((RET_HEADS, RET_DK, RET_DV), F32)],
        compiler_params=_cparams(("parallel", "arbitrary")),
        name="ret_mixer",
    )(z, z, z, z, cos_t, sin_t, decay, kd, qd, cd)


LOG2E = 1.4426950408889634


def _qkprep_kernel(q_ref, k_ref, v_ref, cos_ref, sin_ref, qt_ref, ko_ref, vt_ref):
    reps = q_ref.shape[1] // DIFF_HEAD_DIM
    cos = _tile_lanes(cos_ref[:, DIFF_HEAD_DIM:2 * DIFF_HEAD_DIM], reps)
    sin = _tile_lanes(sin_ref[:, DIFF_HEAD_DIM:2 * DIFF_HEAD_DIM], reps)
    q = _rotate_half_split(q_ref[...].astype(F32), cos, sin, DIFF_HEAD_DIM, ROT_DIM // 2)
    k = _rotate_half_split(k_ref[...].astype(F32), cos, sin, DIFF_HEAD_DIM, ROT_DIM // 2)
    qt_ref[...] = (q * (DIFF_HEAD_DIM ** -0.5 * LOG2E)).T.astype(BF16)
    ko_ref[...] = k.astype(BF16)
    vt_ref[...] = v_ref[...].astype(F32).T.astype(BF16)


def _qkprep_call(z, cos_t, sin_t, bsz, seq, tm=512):
    t = bsz * seq
    tm = min(tm, seq)
    nt = seq // tm
    hq = DIFF_HEADS * 2 * DIFF_HEAD_DIM
    row = lambda b, i: b * nt + i
    return pl.pallas_call(
        _qkprep_kernel,
        grid=(bsz, nt),
        in_specs=[pl.BlockSpec((tm, hq), lambda b, i: (row(b, i), DIFF_OFF // hq)),
                  pl.BlockSpec((tm, hq), lambda b, i: (row(b, i), DIFF_OFF // hq + 1)),
                  pl.BlockSpec((tm, hq), lambda b, i: (row(b, i), DIFF_OFF // hq + 2)),
                  pl.BlockSpec((tm, LANES), lambda b, i: (row(b, i), 0)),
                  pl.BlockSpec((tm, LANES), lambda b, i: (row(b, i), 0))],
        out_specs=[pl.BlockSpec((None, hq, tm), lambda b, i: (b, 0, i)),
                   pl.BlockSpec((tm, hq), lambda b, i: (row(b, i), 0)),
                   pl.BlockSpec((None, hq, tm), lambda b, i: (b, 0, i))],
        out_shape=[jax.ShapeDtypeStruct((bsz, hq, seq), BF16),
                   jax.ShapeDtypeStruct((t, hq), BF16),
                   jax.ShapeDtypeStruct((bsz, hq, seq), BF16)],
        compiler_params=_cparams(("parallel", "parallel")),
        name="diff_qkprep",
    )(z, z, z, cos_t, sin_t)


SUM_ROWS = 16


def _flash_kernel(qi_ref, ki_ref, qt_ref, k_ref, vt_ref, lam_ref, g_ref, o_ref,
                  qd_ref, m_ref, acc_ref, *, tb, cq, lambda_init):
    p = pl.program_id(2)
    qi = qi_ref[p]
    ki = ki_ref[p]
    hd = DIFF_HEAD_DIM
    dv = DIFF_V_DIM

    @pl.when(ki == 0)
    def _():
        qd_ref[...] = jnp.zeros_like(qd_ref)
        qd_ref[0:hd, 0:tb] = qt_ref[0:hd, :]
        qd_ref[hd:2 * hd, tb:2 * tb] = qt_ref[hd:2 * hd, :]
        m_ref[...] = jnp.full_like(m_ref, NEG_BIG)
        acc_ref[...] = jnp.zeros_like(acc_ref)

    def step(diagonal):
        k = k_ref[...]
        vta = jnp.concatenate([vt_ref[...], jnp.ones((SUM_ROWS, tb), BF16)], axis=0)
        nc = 2 * tb // cq
        cols = [slice(c * cq, (c + 1) * cq) for c in range(nc)]
        nkey = [((c * cq) % tb + cq) if diagonal else tb for c in range(nc)]
        m_prev = m_ref[...]
        s = [_dot(k[0:nkey[c]], qd_ref[:, cols[c]]) for c in range(nc)]
        if diagonal:
            for c in range(nc):
                key = lax.broadcasted_iota(jnp.int32, (nkey[c], cq), 0)
                qpos = lax.broadcasted_iota(jnp.int32, (nkey[c], cq), 1) + (c * cq) % tb
                s[c] = jnp.where(key <= qpos, s[c], NEG_BIG)
        m_new = [jnp.maximum(m_prev[:, cols[c]], jnp.max(s[c], axis=0, keepdims=True)) for c in range(nc)]
        pexp = [jnp.exp2(s[c] - m_new[c]).astype(BF16) for c in range(nc)]
        alpha = [jnp.exp2(m_prev[:, cols[c]] - m_new[c]) for c in range(nc)]
        for c in range(nc):
            cs = cols[c]
            m_ref[:, cs] = m_new[c]
            acc_ref[:, cs] = alpha[c] * acc_ref[:, cs] + _dot(vta[:, 0:nkey[c]], pexp[c])

    @pl.when(ki == qi)
    def _():
        step(True)

    @pl.when(ki != qi)
    def _():
        step(False)

    @pl.when(ki == qi)
    def _():
        lam_p = lam_ref[...]
        lam = (jnp.exp(jnp.sum(lam_p[0:1] * lam_p[1:2], axis=-1, keepdims=True))
               - jnp.exp(jnp.sum(lam_p[2:3] * lam_p[3:4], axis=-1, keepdims=True)) + lambda_init)
        o = acc_ref[0:dv, :] / acc_ref[dv:dv + 1, :]
        o = (o[:, 0:tb] - lam * o[:, tb:2 * tb]).T
        o = o * lax.rsqrt(jnp.mean(o * o, axis=-1, keepdims=True) + RMS_EPS)
        o_ref[...] = (o * g_ref[...] * (1.0 - lambda_init)).astype(BF16)


def _flash_call(qt, kr, vt, lam_params, subln_g, bsz, seq, lambda_init, tb=1024, cq=256):
    t = bsz * seq
    tb = min(tb, seq)
    cq = min(cq, tb)
    nb = seq // tb
    qi_list, ki_list = [], []
    for qi in range(nb):
        for ki in range(qi + 1):
            qi_list.append(qi)
            ki_list.append(ki)
    qi_arr = jnp.asarray(qi_list, jnp.int32)
    ki_arr = jnp.asarray(ki_list, jnp.int32)
    hd = 2 * DIFF_HEAD_DIM
    grid_spec = pltpu.PrefetchScalarGridSpec(
        num_scalar_prefetch=2,
        grid=(bsz, DIFF_HEADS, len(qi_list)),
        in_specs=[pl.BlockSpec((None, hd, tb), lambda b, h, p, qi, ki: (b, h, qi[p])),
                  pl.BlockSpec((tb, hd), lambda b, h, p, qi, ki: (b * nb + ki[p], h)),
                  pl.BlockSpec((None, DIFF_V_DIM, tb), lambda b, h, p, qi, ki: (b, h, ki[p])),
                  pl.BlockSpec((4, DIFF_HEAD_DIM), lambda b, h, p, qi, ki: (0, 0)),
                  pl.BlockSpec((1, DIFF_V_DIM), lambda b, h, p, qi, ki: (0, 0))],
        out_specs=pl.BlockSpec((tb, DIFF_V_DIM), lambda b, h, p, qi, ki: (b * nb + qi[p], h)),
        scratch_shapes=[pltpu.VMEM((hd, 2 * tb), BF16),
                        pltpu.VMEM((1, 2 * tb), F32),
                        pltpu.VMEM((DIFF_V_DIM + SUM_ROWS, 2 * tb), F32)],
    )
    return pl.pallas_call(
        functools.partial(_flash_kernel, tb=tb, cq=cq, lambda_init=lambda_init),
        grid_spec=grid_spec,
        out_shape=jax.ShapeDtypeStruct((t, DIFF_HEADS * DIFF_V_DIM), BF16),
        compiler_params=_cparams(("parallel", "parallel", "arbitrary")),
        name="diff_flash",
    )(qi_arr, ki_arr, qt, kr, vt, lam_params, subln_g.reshape(1, DIFF_V_DIM))


def _merge_kernel(h_ref, y0_ref, y1_ref, y2_ref, y3_ref, wg_ref, bg_ref, wb_ref, wo_ref,
                  g_ref, b_ref, wr_ref, br_ref, o_ref, idx_ref, gw_ref, *, alpha):
    h = h_ref[...]
    hb = h.astype(BF16)
    merged = None
    for g, y_ref in enumerate((y0_ref, y1_ref, y2_ref, y3_ref)):
        gate = jax.nn.sigmoid(_dot(hb, wg_ref[g]) + bg_ref[g:g + 1, :])
        term = gate * _dot(y_ref[...], wb_ref[g])
        merged = term if merged is None else merged + term
    t = _dot(merged.astype(BF16), wo_ref[...])
    out = _layer_norm(alpha * h + t, g_ref[...], b_ref[...])
    o_ref[...] = out
    e1, e2, g1, g2 = _route(out, wr_ref[...], br_ref[...])
    idx_ref[0:1, :] = e1
    idx_ref[1:2, :] = e2
    gw_ref[0:1, :] = g1
    gw_ref[1:2, :] = g2


def _merge_call(h, ys, wg, bg, wb, wo, ln_g, ln_b, w_router, b_router, alpha, tm=512):
    t, d = h.shape
    tm = min(tm, t)
    w = BRANCH_WIDTH
    const = dict(pipeline_mode=pl.Buffered(1))
    return pl.pallas_call(
        functools.partial(_merge_kernel, alpha=alpha),
        grid=(t // tm,),
        in_specs=[pl.BlockSpec((tm, d), lambda i: (i, 0))]
        + [pl.BlockSpec((tm, w), lambda i: (i, 0))] * N_BRANCH
        + [pl.BlockSpec((N_BRANCH, d, d), lambda i: (0, 0, 0), **const),
           pl.BlockSpec((N_BRANCH, d), lambda i: (0, 0)),
           pl.BlockSpec((N_BRANCH, w, d), lambda i: (0, 0, 0), **const),
           pl.BlockSpec((d, d), lambda i: (0, 0), **const),
           pl.BlockSpec((1, d), lambda i: (0, 0)),
           pl.BlockSpec((1, d), lambda i: (0, 0)),
           pl.BlockSpec((N_EXPERTS, d), lambda i: (0, 0)),
           pl.BlockSpec((N_EXPERTS, 1), lambda i: (0, 0))],
        out_specs=[pl.BlockSpec((tm, d), lambda i: (i, 0)),
                   pl.BlockSpec((TOP_K, tm), lambda i: (0, i)),
                   pl.BlockSpec((TOP_K, tm), lambda i: (0, i))],
        out_shape=[jax.ShapeDtypeStruct((t, d), F32),
                   jax.ShapeDtypeStruct((TOP_K, t), jnp.int32),
                   jax.ShapeDtypeStruct((TOP_K, t), F32)],
        compiler_params=_cparams(("parallel",)),
        name="merge",
    )(h, *ys, wg, bg, wb, wo, ln_g.reshape(1, d), ln_b.reshape(1, d),
      w_router.T, b_router.reshape(N_EXPERTS, 1))


def _route(h, wr, br):
    logits = lax.dot_general(wr, h, (((1,), (1,)), ((), ())),
                             preferred_element_type=F32, precision=lax.Precision.HIGHEST)
    mx = jnp.max(logits, axis=0, keepdims=True)
    e = jnp.exp(logits - mx)
    scores = e / jnp.sum(e, axis=0, keepdims=True)
    sel = scores + br
    epg = EXPERTS_PER_GROUP
    rows = [sel[i:i + 1, :] for i in range(N_EXPERTS)]
    srows = [scores[i:i + 1, :] for i in range(N_EXPERTS)]
    best_score = None
    best = None
    for g in range(N_GROUPS):
        r = rows[g * epg:(g + 1) * epg]
        gs = None
        for a in range(epg):
            for b in range(a + 1, epg):
                pair = r[a] + r[b]
                gs = pair if gs is None else jnp.maximum(gs, pair)
        if best is None:
            best_score, best = gs, jnp.zeros(gs.shape, jnp.int32)
        else:
            take = gs > best_score
            best_score = jnp.where(take, gs, best_score)
            best = jnp.where(take, g, best)
    cand, cscore = [], []
    for j in range(epg):
        c = rows[j]
        s = srows[j]
        for g in range(1, N_GROUPS):
            c = jnp.where(best == g, rows[g * epg + j], c)
            s = jnp.where(best == g, srows[g * epg + j], s)
        cand.append(c)
        cscore.append(s)
    v1, i1, s1 = cand[0], jnp.zeros(best.shape, jnp.int32), cscore[0]
    for j in range(1, epg):
        take = cand[j] > v1
        v1 = jnp.where(take, cand[j], v1)
        i1 = jnp.where(take, j, i1)
        s1 = jnp.where(take, cscore[j], s1)
    v2 = jnp.full(v1.shape, -jnp.inf, F32)
    i2 = jnp.zeros(best.shape, jnp.int32)
    s2 = jnp.zeros(v1.shape, F32)
    for j in range(epg):
        take = jnp.logical_and(i1 != j, cand[j] > v2)
        v2 = jnp.where(take, cand[j], v2)
        i2 = jnp.where(take, j, i2)
        s2 = jnp.where(take, cscore[j], s2)
    denom = s1 + s2
    return best * epg + i1, best * epg + i2, s1 / denom, s2 / denom


def _rank_kernel(idx_ref, dest_ref, cnt_ref, run_ref, start_ref, *, blk):
    phase = pl.program_id(0)
    i = pl.program_id(1)
    tm = idx_ref.shape[1]
    expert = lax.broadcasted_iota(jnp.int32, (N_EXPERTS, tm), 0)
    oh0 = expert == idx_ref[0:1, :]
    oh1 = expert == idx_ref[1:2, :]
    f0 = jnp.where(oh0, 1.0, 0.0)
    f1 = jnp.where(oh1, 1.0, 0.0)

    @pl.when(jnp.logical_and(phase == 0, i == 0))
    def _():
        run_ref[...] = jnp.zeros_like(run_ref)

    @pl.when(phase == 0)
    def _():
        tot = jnp.sum(f0 + f1, axis=1, keepdims=True)
        run_ref[...] = run_ref[...] + tot

    @pl.when(jnp.logical_and(phase == 1, i == 0))
    def _():
        cnt = run_ref[...]
        cnt_ref[...] = cnt.astype(jnp.int32)
        padded = jnp.ceil(cnt * (1.0 / blk)) * blk
        acc = jnp.zeros((1, LANES), F32)
        for e in range(N_EXPERTS):
            start_ref[e:e + 1, :] = acc
            acc = acc + padded[e:e + 1, :]
        run_ref[...] = jnp.zeros_like(run_ref)

    @pl.when(phase == 1)
    def _():
        s_idx = lax.broadcasted_iota(jnp.int32, (tm, tm), 0)
        t_idx = lax.broadcasted_iota(jnp.int32, (tm, tm), 1)
        tri = jnp.where(s_idx < t_idx, 1.0, 0.0).astype(BF16)
        c0 = _dot(f0.astype(BF16), tri)
        c1 = _dot(f1.astype(BF16), tri)
        tot0 = jnp.sum(f0, axis=1, keepdims=True)
        tot1 = jnp.sum(f1, axis=1, keepdims=True)
        base = start_ref[:, 0:1] + run_ref[:, 0:1]
        d0 = jnp.sum(jnp.where(oh0, base + c0, 0.0), axis=0, keepdims=True)
        d1 = jnp.sum(jnp.where(oh1, base + tot0 + c1, 0.0), axis=0, keepdims=True)
        dest_ref[0:1, :] = d0.astype(jnp.int32)
        dest_ref[1:2, :] = d1.astype(jnp.int32)
        run_ref[...] = run_ref[...] + (tot0 + tot1)


def _rank_call(idx, blk, tm=512):
    t = idx.shape[1]
    tm = min(tm, t)
    return pl.pallas_call(
        functools.partial(_rank_kernel, blk=blk),
        grid=(2, t // tm),
        in_specs=[pl.BlockSpec((TOP_K, tm), lambda p, i: (0, i))],
        out_specs=[pl.BlockSpec((TOP_K, tm), lambda p, i: (0, i * p)),
                   pl.BlockSpec((N_EXPERTS, LANES), lambda p, i: (0, 0))],
        out_shape=[jax.ShapeDtypeStruct((TOP_K, t), jnp.int32),
                   jax.ShapeDtypeStruct((N_EXPERTS, LANES), jnp.int32)],
        scratch_shapes=[pltpu.VMEM((N_EXPERTS, LANES), F32),
                        pltpu.VMEM((N_EXPERTS, LANES), F32)],
        compiler_params=_cparams(("arbitrary", "arbitrary")),
        name="moe_rank",
    )(idx)


def _experts_kernel(dest_ref, be_ref, ve_ref, pe_ref, nb_ref, h_ref, w1_ref, w3_ref, w2_ref, y_ref,
                    inv_ref, xbuf0, xbuf1, obuf0, obuf1, wb1, wb3, wb2, gsem, ssem, *, blk, t, n_blk):
    i = pl.program_id(0)
    nb = nb_ref[0]
    xbuf = (xbuf0, xbuf1)
    obuf = (obuf0, obuf1)
    dump0 = TOP_K * t

    def gather(block, s):
        base = block * blk
        for j in range(blk):
            tok = inv_ref[base + j] & (t - 1)
            pltpu.make_async_copy(h_ref.at[pl.ds(tok, 1), :], xbuf[s].at[pl.ds(j, 1), :],
                                  gsem.at[s]).start(priority=j % 2)

    def wait_gather(s):
        for j in range(blk):
            pltpu.make_async_copy(h_ref.at[pl.ds(0, 1), :], xbuf[s].at[pl.ds(j, 1), :],
                                  gsem.at[s]).wait()

    def scatter(block, s):
        base = block * blk
        for j in range(blk):
            pltpu.make_async_copy(obuf[s].at[pl.ds(j, 1), :], y_ref.at[pl.ds(inv_ref[base + j], 1), :],
                                  ssem.at[s]).start(priority=j % 2)

    def wait_scatter(s):
        for j in range(blk):
            pltpu.make_async_copy(obuf[s].at[pl.ds(j, 1), :], y_ref.at[pl.ds(0, 1), :],
                                  ssem.at[s]).wait()

    @pl.when(i == 0)
    def _():
        def fill(a, carry):
            inv_ref[dest_ref[a]] = a
            return carry

        lax.fori_loop(0, TOP_K * t, fill, 0, unroll=8)

        def pad(r, carry):
            inv_ref[r] = dump0 + (r & (2 * blk - 1))
            return carry

        for e in range(N_EXPERTS):
            lax.fori_loop(ve_ref[e], pe_ref[e], pad, 0)

        def dummy(j, carry):
            inv_ref[n_blk * blk + j] = dump0 + 2 * blk + j
            return carry

        lax.fori_loop(0, blk, dummy, 0)
        obuf1[...] = jnp.zeros_like(obuf1)
        for q in range(3):
            zero_dump = pltpu.make_async_copy(obuf1, y_ref.at[pl.ds(dump0 + q * blk, blk), :], ssem.at[0])
            zero_dump.start()
            zero_dump.wait()
        gather(0, 0)

    def block_step(s):
        wait_gather(s)

        @pl.when(i > 0)
        def _():
            wait_scatter(s)

        gather(jnp.minimum(i + 1, nb - 1), 1 - s)
        scatter(jnp.where(i == 0, n_blk, i - 1), 1 - s)
        xb = xbuf[s][...].astype(BF16)
        a = _dot(xb, wb1[...])
        b = _dot(xb, wb3[...])
        act = (a * jax.nn.sigmoid(a) * b).astype(BF16)
        obuf[s][...] = _dot(act, wb2[...])

        @pl.when(i == nb - 1)
        def _():
            wait_gather(1 - s)
            wait_scatter(1 - s)
            scatter(i, s)
            wait_scatter(s)

    new_expert = jnp.logical_or(i == 0, be_ref[i] != be_ref[jnp.maximum(i - 1, 0)])

    @pl.when(jnp.logical_and(i < nb, new_expert))
    def _():
        wb1[...] = w1_ref[0].astype(BF16)
        wb3[...] = w3_ref[0].astype(BF16)
        wb2[...] = w2_ref[0].astype(BF16)

    for s in range(2):
        @pl.when(jnp.logical_and(i < nb, i % 2 == s))
        def _(s=s):
            block_step(s)


def _experts_call(h, dest_flat, blk_expert, valid_end, pad_end, n_used, w1, w3, w2, layer, blk):
    t, d = h.shape
    de = w1.shape[3]
    n_blk = blk_expert.shape[0]
    assert t & (t - 1) == 0 and blk & (blk - 1) == 0, "token count and block size must be powers of two"
    wmap = lambda i, dest, be, ve, pe, nb: (layer, be[i], 0, 0)
    grid_spec = pltpu.PrefetchScalarGridSpec(
        num_scalar_prefetch=5,
        grid=(n_blk,),
        in_specs=[pl.BlockSpec(memory_space=pl.ANY),
                  pl.BlockSpec((None, 1, d, de), wmap),
                  pl.BlockSpec((None, 1, d, de), wmap),
                  pl.BlockSpec((None, 1, de, d), wmap)],
        out_specs=pl.BlockSpec(memory_space=pl.ANY),
        scratch_shapes=[pltpu.SMEM(((n_blk + 1) * blk,), jnp.int32),
                        pltpu.VMEM((blk, d), F32),
                        pltpu.VMEM((blk, d), F32),
                        pltpu.VMEM((blk, d), F32),
                        pltpu.VMEM((blk, d), F32),
                        pltpu.VMEM((d, de), BF16),
                        pltpu.VMEM((d, de), BF16),
                        pltpu.VMEM((de, d), BF16),
                        pltpu.SemaphoreType.DMA((2,)),
                        pltpu.SemaphoreType.DMA((2,))],
    )
    return pl.pallas_call(
        functools.partial(_experts_kernel, blk=blk, t=t, n_blk=n_blk),
        grid_spec=grid_spec,
        out_shape=jax.ShapeDtypeStruct((TOP_K * t + 3 * blk, d), F32),
        compiler_params=_cparams(("arbitrary",)),
        name="moe_experts",
    )(dest_flat, blk_expert, valid_end, pad_end, n_used, h, w1, w3, w2)


def _combine_kernel(h_ref, gw_ref, y0_ref, y1_ref, g_ref, b_ref, o_ref, *, alpha):
    gw = gw_ref[...]
    y = y0_ref[...] * gw[:, 0:1] + y1_ref[...] * gw[:, 1:2]
    o_ref[...] = _layer_norm(alpha * h_ref[...] + y, g_ref[...], b_ref[...])


def _combine_call(h, gw_t, y, ln_g, ln_b, alpha, tm=512):
    t, d = h.shape
    tm = min(tm, t)
    nt = t // tm
    return pl.pallas_call(
        functools.partial(_combine_kernel, alpha=alpha),
        grid=(nt,),
        in_specs=[pl.BlockSpec((tm, d), lambda i: (i, 0)),
                  pl.BlockSpec((tm, TOP_K), lambda i: (i, 0)),
                  pl.BlockSpec((tm, d), lambda i: (i, 0)),
                  pl.BlockSpec((tm, d), lambda i: (nt + i, 0)),
                  pl.BlockSpec((1, d), lambda i: (0, 0)),
                  pl.BlockSpec((1, d), lambda i: (0, 0))],
        out_specs=pl.BlockSpec((tm, d), lambda i: (i, 0)),
        out_shape=jax.ShapeDtypeStruct((t, d), F32),
        compiler_params=_cparams(("parallel",)),
        name="moe_combine",
    )(h, gw_t, y, y, ln_g.reshape(1, d), ln_b.reshape(1, d))


def _moe_layer(h, idx, gw, w1, w3, w2, layer, ln_g, ln_b, alpha, blk=256):
    t, d = h.shape
    dest, counts = _rank_call(idx, blk)
    n_blk = (TOP_K * t) // blk + N_EXPERTS
    cnt = counts[:, 0]
    padded = ((cnt + blk - 1) // blk) * blk
    pad_end = jnp.cumsum(padded)
    blk_start = jnp.arange(n_blk, dtype=jnp.int32) * blk
    blk_expert = jnp.minimum(jnp.sum(pad_end[None, :] <= blk_start[:, None], axis=1),
                             N_EXPERTS - 1).astype(jnp.int32)
    valid_end = (pad_end - padded + cnt).astype(jnp.int32)
    n_used = (pad_end[N_EXPERTS - 1:] // blk).astype(jnp.int32)
    y = _experts_call(h, dest.reshape(-1), blk_expert, valid_end, pad_end.astype(jnp.int32), n_used,
                      w1, w3, w2, layer, blk)
    return _combine_call(h, gw.T, y, ln_g, ln_b, alpha)


def kernel(x, positions, ln_in_g, ln_in_b, w_in, conv_w, sgu_ln_g, sgu_ln_b, w_s, b_s, lambda_q1, lambda_k1, lambda_q2, lambda_k2, diff_subln_g, w_gate, b_gate, w_branch, w_o, ln1_g, ln1_b, w_router, b_router, w1, w3, w2, ln2_g, ln2_b):
    bsz, seq, d = x.shape
    depth = w_in.shape[0]
    alpha = (2.0 * depth) ** 0.25
    t = bsz * seq
    cos_t, sin_t = _rope_tables(positions)
    h = _ln_call(x.reshape(t, d), ln_in_g, ln_in_b)
    for l in range(depth):
        lambda_init = 0.8 - 0.6 * math.exp(-0.3 * l)
        z = _proj_call(h, w_in[l].astype(BF16))
        y_conv = _conv_call(z, conv_w[l], bsz, seq)
        y_ret = _ret_call(z, cos_t, sin_t, bsz, seq)
        y_sgu = _sgu_call(z, sgu_ln_g[l], sgu_ln_b[l], w_s[l], b_s[l])
        qt, kr, vt = _qkprep_call(z, cos_t, sin_t, bsz, seq)
        lam_params = jnp.stack([lambda_q1[l], lambda_k1[l], lambda_q2[l], lambda_k2[l]])
        y_diff = _flash_call(qt, kr, vt, lam_params, diff_subln_g[l], bsz, seq, lambda_init)
        h, idx, gw = _merge_call(h, (y_conv, y_ret, y_sgu, y_diff), w_gate[l].astype(BF16), b_gate[l],
                                 w_branch[l].astype(BF16), w_o[l].astype(BF16), ln1_g[l], ln1_b[l],
                                 w_router, b_router, alpha)
        h = _moe_layer(h, idx, gw, w1, w3, w2, l, ln2_g[l], ln2_b[l], alpha)
    return h.reshape(bsz, seq, d)
```

```python
import functools
import math

import jax
import jax.numpy as jnp
from jax import lax
from jax.experimental import pallas as pl
from jax.experimental.pallas import tpu as pltpu

D_MODEL = 1024
BRANCH_WIDTH = 512
N_BRANCH = 4
CONV_WIDTH = BRANCH_WIDTH
CONV_K = 3
RET_HEADS = 4
RET_DK = 64
RET_DV = 128
RET_CHUNK = 128
RET_THETA = 10000.0
SGU_GROUPS = 4
SGU_GROUP_DIM = BRANCH_WIDTH // SGU_GROUPS
SGU_CHUNK = 128
SGU_WIDTH = BRANCH_WIDTH
DIFF_HEADS = 4
DIFF_HEAD_DIM = 64
DIFF_V_DIM = 2 * DIFF_HEAD_DIM
ROPE_THETA = 500000.0
ROT_DIM = DIFF_HEAD_DIM // 4
CONV_COLS = 3 * CONV_WIDTH
RET_COLS = 2 * RET_HEADS * RET_DK + 2 * RET_HEADS * RET_DV
SGU_COLS = 2 * SGU_WIDTH
DIFF_COLS = 2 * DIFF_HEADS * 2 * DIFF_HEAD_DIM + DIFF_HEADS * DIFF_V_DIM
IN_COLS = CONV_COLS + RET_COLS + SGU_COLS + DIFF_COLS
N_EXPERTS = 16
N_GROUPS = 4
EXPERTS_PER_GROUP = N_EXPERTS // N_GROUPS
TOP_K = 2
D_EXPERT = 1024
LN_EPS = 1e-5
RMS_EPS = 1e-6

LANES = 128
CONV_OFF = 0
RET_OFF = CONV_COLS
SGU_OFF = RET_OFF + RET_COLS
DIFF_OFF = SGU_OFF + SGU_COLS

NEG_BIG = -1e30
VMEM_LIMIT = 56 * 1024 * 1024

BF16 = jnp.bfloat16
F32 = jnp.float32


def _cparams(sem):
    return pltpu.CompilerParams(dimension_semantics=sem, vmem_limit_bytes=VMEM_LIMIT)


def _layer_norm(xf, g, b):
    mu = jnp.mean(xf, axis=-1, keepdims=True)
    xc = xf - mu
    var = jnp.mean(xc * xc, axis=-1, keepdims=True)
    return xc * lax.rsqrt(var + LN_EPS) * g + b


def _dot(a, b):
    return jnp.dot(a, b, preferred_element_type=F32)


def _dot_nt(a, b):
    return lax.dot_general(a, b, (((1,), (1,)), ((), ())), preferred_element_type=F32)


def _dot_tn(a, b):
    return lax.dot_general(a, b, (((0,), (0,)), ((), ())), preferred_element_type=F32)


def _ln_kernel(x_ref, g_ref, b_ref, o_ref):
    o_ref[...] = _layer_norm(x_ref[...], g_ref[...], b_ref[...])


def _ln_call(x2d, g, b, tm=1024):
    t, d = x2d.shape
    return pl.pallas_call(
        _ln_kernel,
        grid=(t // tm,),
        in_specs=[pl.BlockSpec((tm, d), lambda i: (i, 0)),
                  pl.BlockSpec((1, d), lambda i: (0, 0)),
                  pl.BlockSpec((1, d), lambda i: (0, 0))],
        out_specs=pl.BlockSpec((tm, d), lambda i: (i, 0)),
        out_shape=jax.ShapeDtypeStruct((t, d), F32),
        compiler_params=_cparams(("parallel",)),
        name="ln_in",
    )(x2d, g.reshape(1, d), b.reshape(1, d))


def _rope_kernel(pos_ref, freq_ref, sign_ref, cos_ref, sin_ref):
    ang = pos_ref[...] * freq_ref[...]
    cos_ref[...] = jnp.cos(ang)
    sin_ref[...] = jnp.sin(ang) * sign_ref[...]


def _rope_tables(positions):
    t = positions.size
    pos = positions.reshape(t, 1).astype(F32)
    half_r = RET_DK // 2
    fr = RET_THETA ** (-jnp.arange(half_r, dtype=F32) / half_r)
    half_d = ROT_DIM // 2
    fd = ROPE_THETA ** (-jnp.arange(half_d, dtype=F32) / half_d)
    zeros_d = jnp.zeros((DIFF_HEAD_DIM - ROT_DIM,), F32)
    freq = jnp.concatenate([fr, fr, fd, fd, zeros_d]).reshape(1, LANES)
    sign = jnp.concatenate([-jnp.ones((half_r,), F32), jnp.ones((half_r,), F32),
                            -jnp.ones((half_d,), F32), jnp.ones((half_d,), F32),
                            zeros_d]).reshape(1, LANES)
    tm = min(t, 2048)
    return pl.pallas_call(
        _rope_kernel,
        grid=(t // tm,),
        in_specs=[pl.BlockSpec((tm, 1), lambda i: (i, 0)),
                  pl.BlockSpec((1, LANES), lambda i: (0, 0)),
                  pl.BlockSpec((1, LANES), lambda i: (0, 0))],
        out_specs=[pl.BlockSpec((tm, LANES), lambda i: (i, 0)),
                   pl.BlockSpec((tm, LANES), lambda i: (i, 0))],
        out_shape=[jax.ShapeDtypeStruct((t, LANES), F32)] * 2,
        compiler_params=_cparams(("parallel",)),
        name="rope_tables",
    )(pos, freq, sign)


def _tile_lanes(x, reps):
    return jnp.concatenate([x] * reps, axis=1)


def _rotate_half_split(x, cos, sin_signed, group, half):
    w = x.shape[1]
    lane = lax.broadcasted_iota(jnp.int32, x.shape, 1) % group
    partner = jnp.where(lane < half, pltpu.roll(x, w - half, axis=1), pltpu.roll(x, half, axis=1))
    return x * cos + partner * sin_signed


def _proj_kernel(h_ref, w_ref, z_ref, *, chunk):
    hb = h_ref[...].astype(BF16)
    for n0 in range(0, IN_COLS, chunk):
        z_ref[:, n0:n0 + chunk] = _dot(hb, w_ref[:, n0:n0 + chunk]).astype(BF16)


def _proj_call(h, w_bf16, tm=512):
    t, d = h.shape
    return pl.pallas_call(
        functools.partial(_proj_kernel, chunk=512),
        grid=(t // tm,),
        in_specs=[pl.BlockSpec((tm, d), lambda i: (i, 0)),
                  pl.BlockSpec((d, IN_COLS), lambda i: (0, 0), pipeline_mode=pl.Buffered(1))],
        out_specs=pl.BlockSpec((tm, IN_COLS), lambda i: (i, 0)),
        out_shape=jax.ShapeDtypeStruct((t, IN_COLS), BF16),
        compiler_params=_cparams(("parallel",)),
        name="proj_in",
    )(h, w_bf16)


CONV_HALO = 16


def _conv_kernel(b_ref, c_ref, u_ref, ch_ref, uh_ref, w_ref, o_ref):
    ts = c_ref.shape[0]
    first = pl.program_id(1) == 0
    cu = c_ref[...].astype(F32) * u_ref[...].astype(F32)
    halo = ch_ref[...].astype(F32) * uh_ref[...].astype(F32)
    halo = jnp.where(first, 0.0, halo)
    ext = jnp.concatenate([halo, cu], axis=0)
    n = ext.shape[0]
    prev1 = pltpu.roll(ext, 1, axis=0)[CONV_HALO:n]
    prev2 = pltpu.roll(ext, 2, axis=0)[CONV_HALO:n]
    w = w_ref[...]
    y = prev2 * w[0:1, :] + prev1 * w[1:2, :] + cu * w[2:3, :]
    o_ref[...] = (b_ref[...].astype(F32) * y).astype(BF16)


def _conv_call(z, conv_w, bsz, seq, ts=1024):
    t = bsz * seq
    ts = min(ts, seq)
    nt = seq // ts
    wb = CONV_WIDTH
    hb = ts // CONV_HALO
    col = lambda k: (lambda b, i: (b * nt + i, CONV_OFF // wb + k))
    halo = lambda k: (lambda b, i: (jnp.maximum((b * nt + i) * hb - 1, 0), CONV_OFF // wb + k))
    return pl.pallas_call(
        _conv_kernel,
        grid=(bsz, nt),
        in_specs=[pl.BlockSpec((ts, wb), col(0)),
                  pl.BlockSpec((ts, wb), col(1)),
                  pl.BlockSpec((ts, wb), col(2)),
                  pl.BlockSpec((CONV_HALO, wb), halo(1)),
                  pl.BlockSpec((CONV_HALO, wb), halo(2)),
                  pl.BlockSpec((CONV_K, wb), lambda b, i: (0, 0))],
        out_specs=pl.BlockSpec((ts, wb), lambda b, i: (b * nt + i, 0)),
        out_shape=jax.ShapeDtypeStruct((t, wb), BF16),
        compiler_params=_cparams(("parallel", "parallel")),
        name="conv_mixer",
    )(z, z, z, z, z, conv_w.reshape(CONV_K, wb))


def _sgu_kernel(u_ref, v_ref, g_ref, b_ref, ws_ref, bias_ref, o_ref):
    ts = u_ref.shape[0]
    c = SGU_CHUNK
    v = _layer_norm(v_ref[...].astype(F32), g_ref[...], b_ref[...]).astype(BF16)
    row = lax.broadcasted_iota(jnp.int32, (c, c), 0)
    colm = lax.broadcasted_iota(jnp.int32, (c, c), 1)
    bias = bias_ref[...]
    for g in range(SGU_GROUPS):
        w = jnp.where(row >= colm, ws_ref[g], 0.0).astype(BF16)
        lo = g * SGU_GROUP_DIM
        for n in range(ts // c):
            s = _dot(w, v[n * c:(n + 1) * c, lo:lo + SGU_GROUP_DIM]) + bias[:, lo:lo + SGU_GROUP_DIM]
            u = u_ref[n * c:(n + 1) * c, lo:lo + SGU_GROUP_DIM].astype(F32)
            o_ref[n * c:(n + 1) * c, lo:lo + SGU_GROUP_DIM] = (u * s).astype(BF16)


def _sgu_call(z, ln_g, ln_b, w_s, b_s, ts=512):
    t = z.shape[0]
    ts = min(ts, t)
    wb = SGU_WIDTH
    bias = jnp.repeat(b_s.T, SGU_GROUP_DIM, axis=1)
    return pl.pallas_call(
        _sgu_kernel,
        grid=(t // ts,),
        in_specs=[pl.BlockSpec((ts, wb), lambda i: (i, SGU_OFF // wb)),
                  pl.BlockSpec((ts, wb), lambda i: (i, SGU_OFF // wb + 1)),
                  pl.BlockSpec((1, wb), lambda i: (0, 0)),
                  pl.BlockSpec((1, wb), lambda i: (0, 0)),
                  pl.BlockSpec((SGU_GROUPS, SGU_CHUNK, SGU_CHUNK), lambda i: (0, 0, 0)),
                  pl.BlockSpec((SGU_CHUNK, wb), lambda i: (0, 0))],
        out_specs=pl.BlockSpec((ts, wb), lambda i: (i, 0)),
        out_shape=jax.ShapeDtypeStruct((t, wb), BF16),
        compiler_params=_cparams(("parallel",)),
        name="sgu_mixer",
    )(z, z, ln_g.reshape(1, wb), ln_b.reshape(1, wb), w_s, bias)


def _ret_tables():
    c = RET_CHUNK
    log_gamma = jnp.log1p(-jnp.exp2(-5.0 - jnp.arange(RET_HEADS, dtype=F32)))
    idx = jnp.arange(c, dtype=F32)
    rel = idx[:, None] - idx[None, :]
    decay = jnp.where(rel >= 0, jnp.exp(jnp.maximum(rel, 0.0)[None] * log_gamma[:, None, None]), 0.0)
    k_decay = jnp.exp((c - 1 - idx)[:, None] * log_gamma[None, :])
    q_decay = jnp.exp((idx + 1.0)[:, None] * log_gamma[None, :])
    chunk_decay = jnp.exp(c * log_gamma)
    hk = RET_HEADS * RET_DK
    kd = jnp.repeat(k_decay, RET_DK, axis=1) * (RET_DK ** -0.5)
    qd = jnp.repeat(q_decay, RET_DK, axis=1)
    cd = jnp.broadcast_to(jnp.repeat(chunk_decay, RET_DV)[None, :], (8, RET_HEADS * RET_DV))
    del hk
    return decay, kd, qd, cd


def _ret_kernel(q_ref, k_ref, v_ref, g_ref, cos_ref, sin_ref, decay_ref, kd_ref, qd_ref, cd_ref,
                o_ref, state_ref):
    ts = q_ref.shape[0]
    c = RET_CHUNK
    hk = RET_HEADS * RET_DK

    @pl.when(pl.program_id(1) == 0)
    def _():
        state_ref[...] = jnp.zeros_like(state_ref)

    cos = _tile_lanes(cos_ref[:, 0:RET_DK], RET_HEADS)
    sin = _tile_lanes(sin_ref[:, 0:RET_DK], RET_HEADS)
    q = _rotate_half_split(q_ref[...].astype(F32), cos, sin, RET_DK, RET_DK // 2)
    k = _rotate_half_split(k_ref[...].astype(F32), cos, sin, RET_DK, RET_DK // 2)
    del hk
    for n in range(ts // c):
        r0 = n * c
        qn = q[r0:r0 + c]
        kn = k[r0:r0 + c]
        qb = qn.astype(BF16)
        kb = (kn * (RET_DK ** -0.5)).astype(BF16)
        qdb = (qn * qd_ref[...]).astype(BF16)
        kdb = (kn * kd_ref[...]).astype(BF16)
        for h in range(RET_HEADS):
            ks = slice(h * RET_DK, (h + 1) * RET_DK)
            vs = slice(h * RET_DV, (h + 1) * RET_DV)
            vb = v_ref[r0:r0 + c, vs]
            scores = _dot_nt(qb[:, ks], kb[:, ks]) * decay_ref[h]
            inner = _dot(scores.astype(BF16), vb)
            state = state_ref[h]
            cross = _dot(qdb[:, ks], state.astype(BF16))
            kv = _dot_tn(kdb[:, ks], vb)
            state_ref[h] = state * cd_ref[0:1, vs] + kv
            o = inner + cross
            o = o * lax.rsqrt(jnp.mean(o * o, axis=-1, keepdims=True) + RMS_EPS)
            gate = g_ref[r0:r0 + c, vs].astype(F32)
            gate = gate * jax.nn.sigmoid(gate)
            o_ref[r0:r0 + c, vs] = (gate * o).astype(BF16)


def _ret_call(z, cos_t, sin_t, bsz, seq, ts=512):
    t = bsz * seq
    ts = min(ts, seq)
    nt = seq // ts
    decay, kd, qd, cd = _ret_tables()
    hk = RET_HEADS * RET_DK
    hv = RET_HEADS * RET_DV
    row = lambda b, i: b * nt + i
    return pl.pallas_call(
        _ret_kernel,
        grid=(bsz, nt),
        in_specs=[pl.BlockSpec((ts, hk), lambda b, i: (row(b, i), RET_OFF // hk)),
                  pl.BlockSpec((ts, hk), lambda b, i: (row(b, i), RET_OFF // hk + 1)),
                  pl.BlockSpec((ts, hv), lambda b, i: (row(b, i), (RET_OFF + 2 * hk) // hv)),
                  pl.BlockSpec((ts, hv), lambda b, i: (row(b, i), (RET_OFF + 2 * hk) // hv + 1)),
                  pl.BlockSpec((ts, LANES), lambda b, i: (row(b, i), 0)),
                  pl.BlockSpec((ts, LANES), lambda b, i: (row(b, i), 0)),
                  pl.BlockSpec((RET_HEADS, RET_CHUNK, RET_CHUNK), lambda b, i: (0, 0, 0)),
                  pl.BlockSpec((RET_CHUNK, hk), lambda b, i: (0, 0)),
                  pl.BlockSpec((RET_CHUNK, hk), lambda b, i: (0, 0)),
                  pl.BlockSpec((8, hv), lambda b, i: (0, 0))],
        out_specs=pl.BlockSpec((ts, hv), lambda b, i: (row(b, i), 0)),
        out_shape=jax.ShapeDtypeStruct((t, hv), BF16),
        scratch_shapes=[pltpu.VMEM((RET_HEADS, RET_DK, RET_DV), F32)],
        compiler_params=_cparams(("parallel", "arbitrary")),
        name="ret_mixer",
    )(z, z, z, z, cos_t, sin_t, decay, kd, qd, cd)


LOG2E = 1.4426950408889634


def _qkprep_kernel(q_ref, k_ref, v_ref, cos_ref, sin_ref, qt_ref, ko_ref, vt_ref):
    reps = q_ref.shape[1] // DIFF_HEAD_DIM
    cos = _tile_lanes(cos_ref[:, DIFF_HEAD_DIM:2 * DIFF_HEAD_DIM], reps)
    sin = _tile_lanes(sin_ref[:, DIFF_HEAD_DIM:2 * DIFF_HEAD_DIM], reps)
    q = _rotate_half_split(q_ref[...].astype(F32), cos, sin, DIFF_HEAD_DIM, ROT_DIM // 2)
    k = _rotate_half_split(k_ref[...].astype(F32), cos, sin, DIFF_HEAD_DIM, ROT_DIM // 2)
    qt_ref[...] = (q * (DIFF_HEAD_DIM ** -0.5 * LOG2E)).T.astype(BF16)
    ko_ref[...] = k.astype(BF16)
    vt_ref[...] = v_ref[...].astype(F32).T.astype(BF16)


def _qkprep_call(z, cos_t, sin_t, bsz, seq, tm=512):
    t = bsz * seq
    tm = min(tm, seq)
    nt = seq // tm
    hq = DIFF_HEADS * 2 * DIFF_HEAD_DIM
    row = lambda b, i: b * nt + i
    return pl.pallas_call(
        _qkprep_kernel,
        grid=(bsz, nt),
        in_specs=[pl.BlockSpec((tm, hq), lambda b, i: (row(b, i), DIFF_OFF // hq)),
                  pl.BlockSpec((tm, hq), lambda b, i: (row(b, i), DIFF_OFF // hq + 1)),
                  pl.BlockSpec((tm, hq), lambda b, i: (row(b, i), DIFF_OFF // hq + 2)),
                  pl.BlockSpec((tm, LANES), lambda b, i: (row(b, i), 0)),
                  pl.BlockSpec((tm, LANES), lambda b, i: (row(b, i), 0))],
        out_specs=[pl.BlockSpec((None, hq, tm), lambda b, i: (b, 0, i)),
                   pl.BlockSpec((tm, hq), lambda b, i: (row(b, i), 0)),
                   pl.BlockSpec((None, hq, tm), lambda b, i: (b, 0, i))],
        out_shape=[jax.ShapeDtypeStruct((bsz, hq, seq), BF16),
                   jax.ShapeDtypeStruct((t, hq), BF16),
                   jax.ShapeDtypeStruct((bsz, hq, seq), BF16)],
        compiler_params=_cparams(("parallel", "parallel")),
        name="diff_qkprep",
    )(z, z, z, cos_t, sin_t)


SUM_ROWS = 16


def _flash_kernel(qi_ref, ki_ref, qt_ref, k_ref, vt_ref, lam_ref, g_ref, o_ref,
                  qd_ref, m_ref, acc_ref, *, tb, cq, lambda_init):
    p = pl.program_id(2)
    qi = qi_ref[p]
    ki = ki_ref[p]
    hd = DIFF_HEAD_DIM
    dv = DIFF_V_DIM

    @pl.when(ki == 0)
    def _():
        qd_ref[...] = jnp.zeros_like(qd_ref)
        qd_ref[0:hd, 0:tb] = qt_ref[0:hd, :]
        qd_ref[hd:2 * hd, tb:2 * tb] = qt_ref[hd:2 * hd, :]
        m_ref[...] = jnp.full_like(m_ref, NEG_BIG)
        acc_ref[...] = jnp.zeros_like(acc_ref)

    def step(diagonal):
        k = k_ref[...]
        vta = jnp.concatenate([vt_ref[...], jnp.ones((SUM_ROWS, tb), BF16)], axis=0)
        nc = 2 * tb // cq
        cols = [slice(c * cq, (c + 1) * cq) for c in range(nc)]
        nkey = [((c * cq) % tb + cq) if diagonal else tb for c in range(nc)]
        m_prev = m_ref[...]
        s = [_dot(k[0:nkey[c]], qd_ref[:, cols[c]]) for c in range(nc)]
        if diagonal:
            for c in range(nc):
                key = lax.broadcasted_iota(jnp.int32, (nkey[c], cq), 0)
                qpos = lax.broadcasted_iota(jnp.int32, (nkey[c], cq), 1) + (c * cq) % tb
                s[c] = jnp.where(key <= qpos, s[c], NEG_BIG)
        m_new = [jnp.maximum(m_prev[:, cols[c]], jnp.max(s[c], axis=0, keepdims=True)) for c in range(nc)]
        pexp = [jnp.exp2(s[c] - m_new[c]).astype(BF16) for c in range(nc)]
        alpha = [jnp.exp2(m_prev[:, cols[c]] - m_new[c]) for c in range(nc)]
        for c in range(nc):
            cs = cols[c]
            m_ref[:, cs] = m_new[c]
            acc_ref[:, cs] = alpha[c] * acc_ref[:, cs] + _dot(vta[:, 0:nkey[c]], pexp[c])

    @pl.when(ki == qi)
    def _():
        step(True)

    @pl.when(ki != qi)
    def _():
        step(False)

    @pl.when(ki == qi)
    def _():
        lam_p = lam_ref[...]
        lam = (jnp.exp(jnp.sum(lam_p[0:1] * lam_p[1:2], axis=-1, keepdims=True))
               - jnp.exp(jnp.sum(lam_p[2:3] * lam_p[3:4], axis=-1, keepdims=True)) + lambda_init)
        o = acc_ref[0:dv, :] / acc_ref[dv:dv + 1, :]
        o = (o[:, 0:tb] - lam * o[:, tb:2 * tb]).T
        o = o * lax.rsqrt(jnp.mean(o * o, axis=-1, keepdims=True) + RMS_EPS)
        o_ref[...] = (o * g_ref[...] * (1.0 - lambda_init)).astype(BF16)


def _flash_call(qt, kr, vt, lam_params, subln_g, bsz, seq, lambda_init, tb=1024, cq=256):
    t = bsz * seq
    tb = min(tb, seq)
    cq = min(cq, tb)
    nb = seq // tb
    qi_list, ki_list = [], []
    for qi in range(nb):
        for ki in range(qi + 1):
            qi_list.append(qi)
            ki_list.append(ki)
    qi_arr = jnp.asarray(qi_list, jnp.int32)
    ki_arr = jnp.asarray(ki_list, jnp.int32)
    hd = 2 * DIFF_HEAD_DIM
    grid_spec = pltpu.PrefetchScalarGridSpec(
        num_scalar_prefetch=2,
        grid=(bsz, DIFF_HEADS, len(qi_list)),
        in_specs=[pl.BlockSpec((None, hd, tb), lambda b, h, p, qi, ki: (b, h, qi[p])),
                  pl.BlockSpec((tb, hd), lambda b, h, p, qi, ki: (b * nb + ki[p], h)),
                  pl.BlockSpec((None, DIFF_V_DIM, tb), lambda b, h, p, qi, ki: (b, h, ki[p])),
                  pl.BlockSpec((4, DIFF_HEAD_DIM), lambda b, h, p, qi, ki: (0, 0)),
                  pl.BlockSpec((1, DIFF_V_DIM), lambda b, h, p, qi, ki: (0, 0))],
        out_specs=pl.BlockSpec((tb, DIFF_V_DIM), lambda b, h, p, qi, ki: (b * nb + qi[p], h)),
        scratch_shapes=[pltpu.VMEM((hd, 2 * tb), BF16),
                        pltpu.VMEM((1, 2 * tb), F32),
                        pltpu.VMEM((DIFF_V_DIM + SUM_ROWS, 2 * tb), F32)],
    )
    return pl.pallas_call(
        functools.partial(_flash_kernel, tb=tb, cq=cq, lambda_init=lambda_init),
        grid_spec=grid_spec,
        out_shape=jax.ShapeDtypeStruct((t, DIFF_HEADS * DIFF_V_DIM), BF16),
        compiler_params=_cparams(("parallel", "parallel", "arbitrary")),
        name="diff_flash",
    )(qi_arr, ki_arr, qt, kr, vt, lam_params, subln_g.reshape(1, DIFF_V_DIM))


def _merge_kernel(h_ref, y0_ref, y1_ref, y2_ref, y3_ref, wg_ref, bg_ref, wb_ref, wo_ref,
                  g_ref, b_ref, wr_ref, br_ref, o_ref, o3_ref, idx_ref, gw_ref, *, alpha):
    h = h_ref[...]
    hb = h.astype(BF16)
    merged = None
    for g, y_ref in enumerate((y0_ref, y1_ref, y2_ref, y3_ref)):
        gate = jax.nn.sigmoid(_dot(hb, wg_ref[g]) + bg_ref[g:g + 1, :])
        term = gate * _dot(y_ref[...], wb_ref[g])
        merged = term if merged is None else merged + term
    t = _dot(merged.astype(BF16), wo_ref[...])
    out = _layer_norm(alpha * h + t, g_ref[...], b_ref[...])
    o_ref[...] = out
    for c in range(o3_ref.shape[1]):
        o3_ref[:, c, :] = out[:, c * LANES:(c + 1) * LANES]
    e1, e2, g1, g2 = _route(out, wr_ref[...], br_ref[...])
    idx_ref[0:1, :] = e1
    idx_ref[1:2, :] = e2
    gw_ref[0:1, :] = g1
    gw_ref[1:2, :] = g2


def _merge_call(h, ys, wg, bg, wb, wo, ln_g, ln_b, w_router, b_router, alpha, tm=512):
    t, d = h.shape
    tm = min(tm, t)
    w = BRANCH_WIDTH
    const = dict(pipeline_mode=pl.Buffered(1))
    return pl.pallas_call(
        functools.partial(_merge_kernel, alpha=alpha),
        grid=(t // tm,),
        in_specs=[pl.BlockSpec((tm, d), lambda i: (i, 0))]
        + [pl.BlockSpec((tm, w), lambda i: (i, 0))] * N_BRANCH
        + [pl.BlockSpec((N_BRANCH, d, d), lambda i: (0, 0, 0), **const),
           pl.BlockSpec((N_BRANCH, d), lambda i: (0, 0)),
           pl.BlockSpec((N_BRANCH, w, d), lambda i: (0, 0, 0), **const),
           pl.BlockSpec((d, d), lambda i: (0, 0), **const),
           pl.BlockSpec((1, d), lambda i: (0, 0)),
           pl.BlockSpec((1, d), lambda i: (0, 0)),
           pl.BlockSpec((N_EXPERTS, d), lambda i: (0, 0)),
           pl.BlockSpec((N_EXPERTS, 1), lambda i: (0, 0))],
        out_specs=[pl.BlockSpec((tm, d), lambda i: (i, 0)),
                   pl.BlockSpec((tm, d // LANES, LANES), lambda i: (i, 0, 0)),
                   pl.BlockSpec((TOP_K, tm), lambda i: (0, i)),
                   pl.BlockSpec((TOP_K, tm), lambda i: (0, i))],
        out_shape=[jax.ShapeDtypeStruct((t, d), F32),
                   jax.ShapeDtypeStruct((t, d // LANES, LANES), F32),
                   jax.ShapeDtypeStruct((TOP_K, t), jnp.int32),
                   jax.ShapeDtypeStruct((TOP_K, t), F32)],
        compiler_params=_cparams(("parallel",)),
        name="merge",
    )(h, *ys, wg, bg, wb, wo, ln_g.reshape(1, d), ln_b.reshape(1, d),
      w_router.T, b_router.reshape(N_EXPERTS, 1))


def _route(h, wr, br):
    logits = lax.dot_general(wr, h, (((1,), (1,)), ((), ())),
                             preferred_element_type=F32, precision=lax.Precision.HIGHEST)
    mx = jnp.max(logits, axis=0, keepdims=True)
    e = jnp.exp(logits - mx)
    scores = e / jnp.sum(e, axis=0, keepdims=True)
    sel = scores + br
    epg = EXPERTS_PER_GROUP
    rows = [sel[i:i + 1, :] for i in range(N_EXPERTS)]
    srows = [scores[i:i + 1, :] for i in range(N_EXPERTS)]
    best_score = None
    best = None
    for g in range(N_GROUPS):
        r = rows[g * epg:(g + 1) * epg]
        gs = None
        for a in range(epg):
            for b in range(a + 1, epg):
                pair = r[a] + r[b]
                gs = pair if gs is None else jnp.maximum(gs, pair)
        if best is None:
            best_score, best = gs, jnp.zeros(gs.shape, jnp.int32)
        else:
            take = gs > best_score
            best_score = jnp.where(take, gs, best_score)
            best = jnp.where(take, g, best)
    cand, cscore = [], []
    for j in range(epg):
        c = rows[j]
        s = srows[j]
        for g in range(1, N_GROUPS):
            c = jnp.where(best == g, rows[g * epg + j], c)
            s = jnp.where(best == g, srows[g * epg + j], s)
        cand.append(c)
        cscore.append(s)
    v1, i1, s1 = cand[0], jnp.zeros(best.shape, jnp.int32), cscore[0]
    for j in range(1, epg):
        take = cand[j] > v1
        v1 = jnp.where(take, cand[j], v1)
        i1 = jnp.where(take, j, i1)
        s1 = jnp.where(take, cscore[j], s1)
    v2 = jnp.full(v1.shape, -jnp.inf, F32)
    i2 = jnp.zeros(best.shape, jnp.int32)
    s2 = jnp.zeros(v1.shape, F32)
    for j in range(epg):
        take = jnp.logical_and(i1 != j, cand[j] > v2)
        v2 = jnp.where(take, cand[j], v2)
        i2 = jnp.where(take, j, i2)
        s2 = jnp.where(take, cscore[j], s2)
    denom = s1 + s2
    return best * epg + i1, best * epg + i2, s1 / denom, s2 / denom


def _rank_kernel(idx_ref, dest_ref, cnt_ref, run_ref, start_ref, *, blk):
    phase = pl.program_id(0)
    i = pl.program_id(1)
    tm = idx_ref.shape[1]
    expert = lax.broadcasted_iota(jnp.int32, (N_EXPERTS, tm), 0)
    oh0 = expert == idx_ref[0:1, :]
    oh1 = expert == idx_ref[1:2, :]
    f0 = jnp.where(oh0, 1.0, 0.0)
    f1 = jnp.where(oh1, 1.0, 0.0)

    @pl.when(jnp.logical_and(phase == 0, i == 0))
    def _():
        run_ref[...] = jnp.zeros_like(run_ref)

    @pl.when(phase == 0)
    def _():
        tot = jnp.sum(f0 + f1, axis=1, keepdims=True)
        run_ref[...] = run_ref[...] + tot

    @pl.when(jnp.logical_and(phase == 1, i == 0))
    def _():
        cnt = run_ref[...]
        cnt_ref[...] = cnt.astype(jnp.int32)
        padded = jnp.ceil(cnt * (1.0 / blk)) * blk
        acc = jnp.zeros((1, LANES), F32)
        for e in range(N_EXPERTS):
            start_ref[e:e + 1, :] = acc
            acc = acc + padded[e:e + 1, :]
        run_ref[...] = jnp.zeros_like(run_ref)

    @pl.when(phase == 1)
    def _():
        s_idx = lax.broadcasted_iota(jnp.int32, (tm, tm), 0)
        t_idx = lax.broadcasted_iota(jnp.int32, (tm, tm), 1)
        tri = jnp.where(s_idx < t_idx, 1.0, 0.0).astype(BF16)
        c0 = _dot(f0.astype(BF16), tri)
        c1 = _dot(f1.astype(BF16), tri)
        tot0 = jnp.sum(f0, axis=1, keepdims=True)
        tot1 = jnp.sum(f1, axis=1, keepdims=True)
        base = start_ref[:, 0:1] + run_ref[:, 0:1]
        d0 = jnp.sum(jnp.where(oh0, base + c0, 0.0), axis=0, keepdims=True)
        d1 = jnp.sum(jnp.where(oh1, base + tot0 + c1, 0.0), axis=0, keepdims=True)
        dest_ref[0:1, :] = d0.astype(jnp.int32)
        dest_ref[1:2, :] = d1.astype(jnp.int32)
        run_ref[...] = run_ref[...] + (tot0 + tot1)


def _rank_call(idx, blk, tm=512):
    t = idx.shape[1]
    tm = min(tm, t)
    return pl.pallas_call(
        functools.partial(_rank_kernel, blk=blk),
        grid=(2, t // tm),
        in_specs=[pl.BlockSpec((TOP_K, tm), lambda p, i: (0, i))],
        out_specs=[pl.BlockSpec((TOP_K, tm), lambda p, i: (0, i * p)),
                   pl.BlockSpec((N_EXPERTS, LANES), lambda p, i: (0, 0))],
        out_shape=[jax.ShapeDtypeStruct((TOP_K, t), jnp.int32),
                   jax.ShapeDtypeStruct((N_EXPERTS, LANES), jnp.int32)],
        scratch_shapes=[pltpu.VMEM((N_EXPERTS, LANES), F32),
                        pltpu.VMEM((N_EXPERTS, LANES), F32)],
        compiler_params=_cparams(("arbitrary", "arbitrary")),
        name="moe_rank",
    )(idx)


def _experts_kernel(dest_ref, be_ref, ve_ref, pe_ref, nb_ref, h_ref, w1_ref, w3_ref, w2_ref, y_ref,
                    inv_ref, xbuf0, xbuf1, obuf0, obuf1, wb1, wb3, wb2, gsem, ssem, *, blk, t, n_blk):
    i = pl.program_id(0)
    nb = nb_ref[0]
    xbuf = (xbuf0, xbuf1)
    obuf = (obuf0, obuf1)
    nl = xbuf0.shape[1]
    dump0 = TOP_K * t

    def gather(block, s):
        base = block * blk
        for j in range(blk):
            tok = inv_ref[base + j] & (t - 1)
            pltpu.make_async_copy(h_ref.at[tok], xbuf[s].at[j], gsem.at[s]).start(priority=j % 2)

    def wait_gather(s):
        for j in range(blk):
            pltpu.make_async_copy(h_ref.at[0], xbuf[s].at[j], gsem.at[s]).wait()

    def scatter(block, s):
        base = block * blk
        for j in range(blk):
            pltpu.make_async_copy(obuf[s].at[j], y_ref.at[inv_ref[base + j]],
                                  ssem.at[s]).start(priority=j % 2)

    def wait_scatter(s):
        for j in range(blk):
            pltpu.make_async_copy(obuf[s].at[j], y_ref.at[0], ssem.at[s]).wait()

    @pl.when(i == 0)
    def _():
        def fill(a, carry):
            inv_ref[dest_ref[a]] = a
            return carry

        lax.fori_loop(0, TOP_K * t, fill, 0, unroll=8)

        def pad(r, carry):
            inv_ref[r] = dump0 + (r & (2 * blk - 1))
            return carry

        for e in range(N_EXPERTS):
            lax.fori_loop(ve_ref[e], pe_ref[e], pad, 0)

        def dummy(j, carry):
            inv_ref[n_blk * blk + j] = dump0 + 2 * blk + j
            return carry

        lax.fori_loop(0, blk, dummy, 0)
        obuf1[...] = jnp.zeros_like(obuf1)
        for q in range(3):
            zero_dump = pltpu.make_async_copy(obuf1, y_ref.at[pl.ds(dump0 + q * blk, blk)], ssem.at[0])
            zero_dump.start()
            zero_dump.wait()
        gather(0, 0)

    def block_step(s):
        wait_gather(s)

        @pl.when(i > 0)
        def _():
            wait_scatter(s)

        gather(jnp.minimum(i + 1, nb - 1), 1 - s)
        scatter(jnp.where(i == 0, n_blk, i - 1), 1 - s)
        xb = jnp.concatenate([xbuf[s][:, c, :] for c in range(nl)], axis=1).astype(BF16)
        a = _dot(xb, wb1[...])
        b = _dot(xb, wb3[...])
        act = (a * jax.nn.sigmoid(a) * b).astype(BF16)
        res = _dot(act, wb2[...])
        for c in range(nl):
            obuf[s][:, c, :] = res[:, c * LANES:(c + 1) * LANES]

        @pl.when(i == nb - 1)
        def _():
            wait_gather(1 - s)
            wait_scatter(1 - s)
            scatter(i, s)
            wait_scatter(s)

    new_expert = jnp.logical_or(i == 0, be_ref[i] != be_ref[jnp.maximum(i - 1, 0)])

    @pl.when(jnp.logical_and(i < nb, new_expert))
    def _():
        wb1[...] = w1_ref[0].astype(BF16)
        wb3[...] = w3_ref[0].astype(BF16)
        wb2[...] = w2_ref[0].astype(BF16)

    for s in range(2):
        @pl.when(jnp.logical_and(i < nb, i % 2 == s))
        def _(s=s):
            block_step(s)


def _experts_call(h3, dest_flat, blk_expert, valid_end, pad_end, n_used, w1, w3, w2, layer, blk):
    t, nl, _ = h3.shape
    d = nl * LANES
    de = w1.shape[3]
    n_blk = blk_expert.shape[0]
    assert t & (t - 1) == 0 and blk & (blk - 1) == 0, "token count and block size must be powers of two"
    wmap = lambda i, dest, be, ve, pe, nb: (layer, be[i], 0, 0)
    grid_spec = pltpu.PrefetchScalarGridSpec(
        num_scalar_prefetch=5,
        grid=(n_blk,),
        in_specs=[pl.BlockSpec(memory_space=pl.ANY),
                  pl.BlockSpec((None, 1, d, de), wmap),
                  pl.BlockSpec((None, 1, d, de), wmap),
                  pl.BlockSpec((None, 1, de, d), wmap)],
        out_specs=pl.BlockSpec(memory_space=pl.ANY),
        scratch_shapes=[pltpu.SMEM(((n_blk + 1) * blk,), jnp.int32),
                        pltpu.VMEM((blk, nl, LANES), F32),
                        pltpu.VMEM((blk, nl, LANES), F32),
                        pltpu.VMEM((blk, nl, LANES), F32),
                        pltpu.VMEM((blk, nl, LANES), F32),
                        pltpu.VMEM((d, de), BF16),
                        pltpu.VMEM((d, de), BF16),
                        pltpu.VMEM((de, d), BF16),
                        pltpu.SemaphoreType.DMA((2,)),
                        pltpu.SemaphoreType.DMA((2,))],
    )
    return pl.pallas_call(
        functools.partial(_experts_kernel, blk=blk, t=t, n_blk=n_blk),
        grid_spec=grid_spec,
        out_shape=jax.ShapeDtypeStruct((TOP_K * t + 3 * blk, nl, LANES), F32),
        compiler_params=_cparams(("arbitrary",)),
        name="moe_experts",
    )(dest_flat, blk_expert, valid_end, pad_end, n_used, h3, w1, w3, w2)


def _combine_kernel(h_ref, gw_ref, y0_ref, y1_ref, g_ref, b_ref, o_ref, *, alpha):
    gw = gw_ref[...]
    g0 = gw[:, 0:1]
    g1 = gw[:, 1:2]
    y = jnp.concatenate([y0_ref[:, c, :] * g0 + y1_ref[:, c, :] * g1 for c in range(y0_ref.shape[1])],
                        axis=1)
    o_ref[...] = _layer_norm(alpha * h_ref[...] + y, g_ref[...], b_ref[...])


def _combine_call(h, gw_t, y, ln_g, ln_b, alpha, tm=512):
    t, d = h.shape
    tm = min(tm, t)
    nt = t // tm
    nl = y.shape[1]
    return pl.pallas_call(
        functools.partial(_combine_kernel, alpha=alpha),
        grid=(nt,),
        in_specs=[pl.BlockSpec((tm, d), lambda i: (i, 0)),
                  pl.BlockSpec((tm, TOP_K), lambda i: (i, 0)),
                  pl.BlockSpec((tm, nl, LANES), lambda i: (i, 0, 0)),
                  pl.BlockSpec((tm, nl, LANES), lambda i: (nt + i, 0, 0)),
                  pl.BlockSpec((1, d), lambda i: (0, 0)),
                  pl.BlockSpec((1, d), lambda i: (0, 0))],
        out_specs=pl.BlockSpec((tm, d), lambda i: (i, 0)),
        out_shape=jax.ShapeDtypeStruct((t, d), F32),
        compiler_params=_cparams(("parallel",)),
        name="moe_combine",
    )(h, gw_t, y, y, ln_g.reshape(1, d), ln_b.reshape(1, d))


def _moe_layer(h, h3, idx, gw, w1, w3, w2, layer, ln_g, ln_b, alpha, blk=256):
    t, d = h.shape
    dest, counts = _rank_call(idx, blk)
    n_blk = (TOP_K * t) // blk + N_EXPERTS
    cnt = counts[:, 0]
    padded = ((cnt + blk - 1) // blk) * blk
    pad_end = jnp.cumsum(padded)
    blk_start = jnp.arange(n_blk, dtype=jnp.int32) * blk
    blk_expert = jnp.minimum(jnp.sum(pad_end[None, :] <= blk_start[:, None], axis=1),
                             N_EXPERTS - 1).astype(jnp.int32)
    valid_end = (pad_end - padded + cnt).astype(jnp.int32)
    n_used = (pad_end[N_EXPERTS - 1:] // blk).astype(jnp.int32)
    y = _experts_call(h3, dest.reshape(-1), blk_expert, valid_end, pad_end.astype(jnp.int32), n_used,
                      w1, w3, w2, layer, blk)
    return _combine_call(h, gw.T, y, ln_g, ln_b, alpha)


def kernel(x, positions, ln_in_g, ln_in_b, w_in, conv_w, sgu_ln_g, sgu_ln_b, w_s, b_s, lambda_q1, lambda_k1, lambda_q2, lambda_k2, diff_subln_g, w_gate, b_gate, w_branch, w_o, ln1_g, ln1_b, w_router, b_router, w1, w3, w2, ln2_g, ln2_b):
    bsz, seq, d = x.shape
    depth = w_in.shape[0]
    alpha = (2.0 * depth) ** 0.25
    t = bsz * seq
    cos_t, sin_t = _rope_tables(positions)
    h = _ln_call(x.reshape(t, d), ln_in_g, ln_in_b)
    for l in range(depth):
        lambda_init = 0.8 - 0.6 * math.exp(-0.3 * l)
        z = _proj_call(h, w_in[l].astype(BF16))
        y_conv = _conv_call(z, conv_w[l], bsz, seq)
        y_ret = _ret_call(z, cos_t, sin_t, bsz, seq)
        y_sgu = _sgu_call(z, sgu_ln_g[l], sgu_ln_b[l], w_s[l], b_s[l])
        qt, kr, vt = _qkprep_call(z, cos_t, sin_t, bsz, seq)
        lam_params = jnp.stack([lambda_q1[l], lambda_k1[l], lambda_q2[l], lambda_k2[l]])
        y_diff = _flash_call(qt, kr, vt, lam_params, diff_subln_g[l], bsz, seq, lambda_init)
        h, h3, idx, gw = _merge_call(h, (y_conv, y_ret, y_sgu, y_diff), w_gate[l].astype(BF16), b_gate[l],
                                     w_branch[l].astype(BF16), w_o[l].astype(BF16), ln1_g[l], ln1_b[l],
                                     w_router, b_router, alpha)
        h = _moe_layer(h, h3, idx, gw, w1, w3, w2, l, ln2_g[l], ln2_b[l], alpha)
    return h.reshape(bsz, seq, d)
```

```python
import functools
import math

import jax
import jax.numpy as jnp
from jax import lax
from jax.experimental import pallas as pl
from jax.experimental.pallas import tpu as pltpu

D_MODEL = 1024
BRANCH_WIDTH = 512
N_BRANCH = 4
CONV_WIDTH = BRANCH_WIDTH
CONV_K = 3
RET_HEADS = 4
RET_DK = 64
RET_DV = 128
RET_CHUNK = 128
RET_THETA = 10000.0
SGU_GROUPS = 4
SGU_GROUP_DIM = BRANCH_WIDTH // SGU_GROUPS
SGU_CHUNK = 128
SGU_WIDTH = BRANCH_WIDTH
DIFF_HEADS = 4
DIFF_HEAD_DIM = 64
DIFF_V_DIM = 2 * DIFF_HEAD_DIM
ROPE_THETA = 500000.0
ROT_DIM = DIFF_HEAD_DIM // 4
CONV_COLS = 3 * CONV_WIDTH
RET_COLS = 2 * RET_HEADS * RET_DK + 2 * RET_HEADS * RET_DV
SGU_COLS = 2 * SGU_WIDTH
DIFF_COLS = 2 * DIFF_HEADS * 2 * DIFF_HEAD_DIM + DIFF_HEADS * DIFF_V_DIM
IN_COLS = CONV_COLS + RET_COLS + SGU_COLS + DIFF_COLS
N_EXPERTS = 16
N_GROUPS = 4
EXPERTS_PER_GROUP = N_EXPERTS // N_GROUPS
TOP_K = 2
D_EXPERT = 1024
LN_EPS = 1e-5
RMS_EPS = 1e-6

LANES = 128
CONV_OFF = 0
RET_OFF = CONV_COLS
SGU_OFF = RET_OFF + RET_COLS
DIFF_OFF = SGU_OFF + SGU_COLS

NEG_BIG = -1e30
VMEM_LIMIT = 56 * 1024 * 1024

BF16 = jnp.bfloat16
F32 = jnp.float32


def _cparams(sem):
    return pltpu.CompilerParams(dimension_semantics=sem, vmem_limit_bytes=VMEM_LIMIT)


def _layer_norm(xf, g, b):
    mu = jnp.mean(xf, axis=-1, keepdims=True)
    xc = xf - mu
    var = jnp.mean(xc * xc, axis=-1, keepdims=True)
    return xc * lax.rsqrt(var + LN_EPS) * g + b


def _dot(a, b):
    return jnp.dot(a, b, preferred_element_type=F32)


def _dot_nt(a, b):
    return lax.dot_general(a, b, (((1,), (1,)), ((), ())), preferred_element_type=F32)


def _dot_tn(a, b):
    return lax.dot_general(a, b, (((0,), (0,)), ((), ())), preferred_element_type=F32)


def _ln_kernel(x_ref, g_ref, b_ref, o_ref):
    o_ref[...] = _layer_norm(x_ref[...], g_ref[...], b_ref[...])


def _ln_call(x2d, g, b, tm=1024):
    t, d = x2d.shape
    return pl.pallas_call(
        _ln_kernel,
        grid=(t // tm,),
        in_specs=[pl.BlockSpec((tm, d), lambda i: (i, 0)),
                  pl.BlockSpec((1, d), lambda i: (0, 0)),
                  pl.BlockSpec((1, d), lambda i: (0, 0))],
        out_specs=pl.BlockSpec((tm, d), lambda i: (i, 0)),
        out_shape=jax.ShapeDtypeStruct((t, d), F32),
        compiler_params=_cparams(("parallel",)),
        name="ln_in",
    )(x2d, g.reshape(1, d), b.reshape(1, d))


def _rope_kernel(pos_ref, freq_ref, sign_ref, cos_ref, sin_ref):
    ang = pos_ref[...] * freq_ref[...]
    cos_ref[...] = jnp.cos(ang)
    sin_ref[...] = jnp.sin(ang) * sign_ref[...]


def _rope_tables(positions):
    t = positions.size
    pos = positions.reshape(t, 1).astype(F32)
    half_r = RET_DK // 2
    fr = RET_THETA ** (-jnp.arange(half_r, dtype=F32) / half_r)
    half_d = ROT_DIM // 2
    fd = ROPE_THETA ** (-jnp.arange(half_d, dtype=F32) / half_d)
    zeros_d = jnp.zeros((DIFF_HEAD_DIM - ROT_DIM,), F32)
    freq = jnp.concatenate([fr, fr, fd, fd, zeros_d]).reshape(1, LANES)
    sign = jnp.concatenate([-jnp.ones((half_r,), F32), jnp.ones((half_r,), F32),
                            -jnp.ones((half_d,), F32), jnp.ones((half_d,), F32),
                            zeros_d]).reshape(1, LANES)
    tm = min(t, 2048)
    return pl.pallas_call(
        _rope_kernel,
        grid=(t // tm,),
        in_specs=[pl.BlockSpec((tm, 1), lambda i: (i, 0)),
                  pl.BlockSpec((1, LANES), lambda i: (0, 0)),
                  pl.BlockSpec((1, LANES), lambda i: (0, 0))],
        out_specs=[pl.BlockSpec((tm, LANES), lambda i: (i, 0)),
                   pl.BlockSpec((tm, LANES), lambda i: (i, 0))],
        out_shape=[jax.ShapeDtypeStruct((t, LANES), F32)] * 2,
        compiler_params=_cparams(("parallel",)),
        name="rope_tables",
    )(pos, freq, sign)


def _tile_lanes(x, reps):
    return jnp.concatenate([x] * reps, axis=1)


def _rotate_half_split(x, cos, sin_signed, group, half):
    w = x.shape[1]
    lane = lax.broadcasted_iota(jnp.int32, x.shape, 1) % group
    partner = jnp.where(lane < half, pltpu.roll(x, w - half, axis=1), pltpu.roll(x, half, axis=1))
    return x * cos + partner * sin_signed


def _proj_kernel(h_ref, w_ref, z_ref, *, chunk):
    hb = h_ref[...].astype(BF16)
    for n0 in range(0, IN_COLS, chunk):
        z_ref[:, n0:n0 + chunk] = _dot(hb, w_ref[:, n0:n0 + chunk]).astype(BF16)


def _proj_call(h, w_bf16, tm=512):
    t, d = h.shape
    return pl.pallas_call(
        functools.partial(_proj_kernel, chunk=512),
        grid=(t // tm,),
        in_specs=[pl.BlockSpec((tm, d), lambda i: (i, 0)),
                  pl.BlockSpec((d, IN_COLS), lambda i: (0, 0), pipeline_mode=pl.Buffered(1))],
        out_specs=pl.BlockSpec((tm, IN_COLS), lambda i: (i, 0)),
        out_shape=jax.ShapeDtypeStruct((t, IN_COLS), BF16),
        compiler_params=_cparams(("parallel",)),
        name="proj_in",
    )(h, w_bf16)


CONV_HALO = 16


def _conv_kernel(b_ref, c_ref, u_ref, ch_ref, uh_ref, w_ref, o_ref):
    ts = c_ref.shape[0]
    first = pl.program_id(1) == 0
    cu = c_ref[...].astype(F32) * u_ref[...].astype(F32)
    halo = ch_ref[...].astype(F32) * uh_ref[...].astype(F32)
    halo = jnp.where(first, 0.0, halo)
    ext = jnp.concatenate([halo, cu], axis=0)
    n = ext.shape[0]
    prev1 = pltpu.roll(ext, 1, axis=0)[CONV_HALO:n]
    prev2 = pltpu.roll(ext, 2, axis=0)[CONV_HALO:n]
    w = w_ref[...]
    y = prev2 * w[0:1, :] + prev1 * w[1:2, :] + cu * w[2:3, :]
    o_ref[...] = (b_ref[...].astype(F32) * y).astype(BF16)


def _conv_call(z, conv_w, bsz, seq, ts=1024):
    t = bsz * seq
    ts = min(ts, seq)
    nt = seq // ts
    wb = CONV_WIDTH
    hb = ts // CONV_HALO
    col = lambda k: (lambda b, i: (b * nt + i, CONV_OFF // wb + k))
    halo = lambda k: (lambda b, i: (jnp.maximum((b * nt + i) * hb - 1, 0), CONV_OFF // wb + k))
    return pl.pallas_call(
        _conv_kernel,
        grid=(bsz, nt),
        in_specs=[pl.BlockSpec((ts, wb), col(0)),
                  pl.BlockSpec((ts, wb), col(1)),
                  pl.BlockSpec((ts, wb), col(2)),
                  pl.BlockSpec((CONV_HALO, wb), halo(1)),
                  pl.BlockSpec((CONV_HALO, wb), halo(2)),
                  pl.BlockSpec((CONV_K, wb), lambda b, i: (0, 0))],
        out_specs=pl.BlockSpec((ts, wb), lambda b, i: (b * nt + i, 0)),
        out_shape=jax.ShapeDtypeStruct((t, wb), BF16),
        compiler_params=_cparams(("parallel", "parallel")),
        name="conv_mixer",
    )(z, z, z, z, z, conv_w.reshape(CONV_K, wb))


def _sgu_kernel(u_ref, v_ref, g_ref, b_ref, ws_ref, bias_ref, o_ref):
    ts = u_ref.shape[0]
    c = SGU_CHUNK
    v = _layer_norm(v_ref[...].astype(F32), g_ref[...], b_ref[...]).astype(BF16)
    row = lax.broadcasted_iota(jnp.int32, (c, c), 0)
    colm = lax.broadcasted_iota(jnp.int32, (c, c), 1)
    bias = bias_ref[...]
    for g in range(SGU_GROUPS):
        w = jnp.where(row >= colm, ws_ref[g], 0.0).astype(BF16)
        lo = g * SGU_GROUP_DIM
        for n in range(ts // c):
            s = _dot(w, v[n * c:(n + 1) * c, lo:lo + SGU_GROUP_DIM]) + bias[:, lo:lo + SGU_GROUP_DIM]
            u = u_ref[n * c:(n + 1) * c, lo:lo + SGU_GROUP_DIM].astype(F32)
            o_ref[n * c:(n + 1) * c, lo:lo + SGU_GROUP_DIM] = (u * s).astype(BF16)


def _sgu_call(z, ln_g, ln_b, w_s, b_s, ts=512):
    t = z.shape[0]
    ts = min(ts, t)
    wb = SGU_WIDTH
    bias = jnp.repeat(b_s.T, SGU_GROUP_DIM, axis=1)
    return pl.pallas_call(
        _sgu_kernel,
        grid=(t // ts,),
        in_specs=[pl.BlockSpec((ts, wb), lambda i: (i, SGU_OFF // wb)),
                  pl.BlockSpec((ts, wb), lambda i: (i, SGU_OFF // wb + 1)),
                  pl.BlockSpec((1, wb), lambda i: (0, 0)),
                  pl.BlockSpec((1, wb), lambda i: (0, 0)),
                  pl.BlockSpec((SGU_GROUPS, SGU_CHUNK, SGU_CHUNK), lambda i: (0, 0, 0)),
                  pl.BlockSpec((SGU_CHUNK, wb), lambda i: (0, 0))],
        out_specs=pl.BlockSpec((ts, wb), lambda i: (i, 0)),
        out_shape=jax.ShapeDtypeStruct((t, wb), BF16),
        compiler_params=_cparams(("parallel",)),
        name="sgu_mixer",
    )(z, z, ln_g.reshape(1, wb), ln_b.reshape(1, wb), w_s, bias)


def _ret_tables():
    c = RET_CHUNK
    log_gamma = jnp.log1p(-jnp.exp2(-5.0 - jnp.arange(RET_HEADS, dtype=F32)))
    idx = jnp.arange(c, dtype=F32)
    rel = idx[:, None] - idx[None, :]
    decay = jnp.where(rel >= 0, jnp.exp(jnp.maximum(rel, 0.0)[None] * log_gamma[:, None, None]), 0.0)
    k_decay = jnp.exp((c - 1 - idx)[:, None] * log_gamma[None, :])
    q_decay = jnp.exp((idx + 1.0)[:, None] * log_gamma[None, :])
    chunk_decay = jnp.exp(c * log_gamma)
    hk = RET_HEADS * RET_DK
    kd = jnp.repeat(k_decay, RET_DK, axis=1) * (RET_DK ** -0.5)
    qd = jnp.repeat(q_decay, RET_DK, axis=1)
    cd = jnp.broadcast_to(jnp.repeat(chunk_decay, RET_DV)[None, :], (8, RET_HEADS * RET_DV))
    del hk
    return decay, kd, qd, cd


def _ret_kernel(q_ref, k_ref, v_ref, g_ref, cos_ref, sin_ref, decay_ref, kd_ref, qd_ref, cd_ref,
                o_ref, state_ref):
    ts = q_ref.shape[0]
    c = RET_CHUNK
    hk = RET_HEADS * RET_DK

    @pl.when(pl.program_id(1) == 0)
    def _():
        state_ref[...] = jnp.zeros_like(state_ref)

    cos = _tile_lanes(cos_ref[:, 0:RET_DK], RET_HEADS)
    sin = _tile_lanes(sin_ref[:, 0:RET_DK], RET_HEADS)
    q = _rotate_half_split(q_ref[...].astype(F32), cos, sin, RET_DK, RET_DK // 2)
    k = _rotate_half_split(k_ref[...].astype(F32), cos, sin, RET_DK, RET_DK // 2)
    del hk
    for n in range(ts // c):
        r0 = n * c
        qn = q[r0:r0 + c]
        kn = k[r0:r0 + c]
        qb = qn.astype(BF16)
        kb = (kn * (RET_DK ** -0.5)).astype(BF16)
        qdb = (qn * qd_ref[...]).astype(BF16)
        kdb = (kn * kd_ref[...]).astype(BF16)
        for h in range(RET_HEADS):
            ks = slice(h * RET_DK, (h + 1) * RET_DK)
            vs = slice(h * RET_DV, (h + 1) * RET_DV)
            vb = v_ref[r0:r0 + c, vs]
            scores = _dot_nt(qb[:, ks], kb[:, ks]) * decay_ref[h]
            inner = _dot(scores.astype(BF16), vb)
            state = state_ref[h]
            cross = _dot(qdb[:, ks], state.astype(BF16))
            kv = _dot_tn(kdb[:, ks], vb)
            state_ref[h] = state * cd_ref[0:1, vs] + kv
            o = inner + cross
            o = o * lax.rsqrt(jnp.mean(o * o, axis=-1, keepdims=True) + RMS_EPS)
            gate = g_ref[r0:r0 + c, vs].astype(F32)
            gate = gate * jax.nn.sigmoid(gate)
            o_ref[r0:r0 + c, vs] = (gate * o).astype(BF16)


def _ret_call(z, cos_t, sin_t, bsz, seq, ts=512):
    t = bsz * seq
    ts = min(ts, seq)
    nt = seq // ts
    decay, kd, qd, cd = _ret_tables()
    hk = RET_HEADS * RET_DK
    hv = RET_HEADS * RET_DV
    row = lambda b, i: b * nt + i
    return pl.pallas_call(
        _ret_kernel,
        grid=(bsz, nt),
        in_specs=[pl.BlockSpec((ts, hk), lambda b, i: (row(b, i), RET_OFF // hk)),
                  pl.BlockSpec((ts, hk), lambda b, i: (row(b, i), RET_OFF // hk + 1)),
                  pl.BlockSpec((ts, hv), lambda b, i: (row(b, i), (RET_OFF + 2 * hk) // hv)),
                  pl.BlockSpec((ts, hv), lambda b, i: (row(b, i), (RET_OFF + 2 * hk) // hv + 1)),
                  pl.BlockSpec((ts, LANES), lambda b, i: (row(b, i), 0)),
                  pl.BlockSpec((ts, LANES), lambda b, i: (row(b, i), 0)),
                  pl.BlockSpec((RET_HEADS, RET_CHUNK, RET_CHUNK), lambda b, i: (0, 0, 0)),
                  pl.BlockSpec((RET_CHUNK, hk), lambda b, i: (0, 0)),
                  pl.BlockSpec((RET_CHUNK, hk), lambda b, i: (0, 0)),
                  pl.BlockSpec((8, hv), lambda b, i: (0, 0))],
        out_specs=pl.BlockSpec((ts, hv), lambda b, i: (row(b, i), 0)),
        out_shape=jax.ShapeDtypeStruct((t, hv), BF16),
        scratch_shapes=[pltpu.VMEM((RET_HEADS, RET_DK, RET_DV), F32)],
        compiler_params=_cparams(("parallel", "arbitrary")),
        name="ret_mixer",
    )(z, z, z, z, cos_t, sin_t, decay, kd, qd, cd)


LOG2E = 1.4426950408889634


def _qkprep_kernel(q_ref, k_ref, v_ref, cos_ref, sin_ref, qt_ref, ko_ref, vt_ref):
    reps = q_ref.shape[1] // DIFF_HEAD_DIM
    cos = _tile_lanes(cos_ref[:, DIFF_HEAD_DIM:2 * DIFF_HEAD_DIM], reps)
    sin = _tile_lanes(sin_ref[:, DIFF_HEAD_DIM:2 * DIFF_HEAD_DIM], reps)
    q = _rotate_half_split(q_ref[...].astype(F32), cos, sin, DIFF_HEAD_DIM, ROT_DIM // 2)
    k = _rotate_half_split(k_ref[...].astype(F32), cos, sin, DIFF_HEAD_DIM, ROT_DIM // 2)
    qt_ref[...] = (q * (DIFF_HEAD_DIM ** -0.5 * LOG2E)).T.astype(BF16)
    ko_ref[...] = k.astype(BF16)
    vt_ref[...] = v_ref[...].astype(F32).T.astype(BF16)


def _qkprep_call(z, cos_t, sin_t, bsz, seq, tm=512):
    t = bsz * seq
    tm = min(tm, seq)
    nt = seq // tm
    hq = DIFF_HEADS * 2 * DIFF_HEAD_DIM
    row = lambda b, i: b * nt + i
    return pl.pallas_call(
        _qkprep_kernel,
        grid=(bsz, nt),
        in_specs=[pl.BlockSpec((tm, hq), lambda b, i: (row(b, i), DIFF_OFF // hq)),
                  pl.BlockSpec((tm, hq), lambda b, i: (row(b, i), DIFF_OFF // hq + 1)),
                  pl.BlockSpec((tm, hq), lambda b, i: (row(b, i), DIFF_OFF // hq + 2)),
                  pl.BlockSpec((tm, LANES), lambda b, i: (row(b, i), 0)),
                  pl.BlockSpec((tm, LANES), lambda b, i: (row(b, i), 0))],
        out_specs=[pl.BlockSpec((None, hq, tm), lambda b, i: (b, 0, i)),
                   pl.BlockSpec((tm, hq), lambda b, i: (row(b, i), 0)),
                   pl.BlockSpec((None, hq, tm), lambda b, i: (b, 0, i))],
        out_shape=[jax.ShapeDtypeStruct((bsz, hq, seq), BF16),
                   jax.ShapeDtypeStruct((t, hq), BF16),
                   jax.ShapeDtypeStruct((bsz, hq, seq), BF16)],
        compiler_params=_cparams(("parallel", "parallel")),
        name="diff_qkprep",
    )(z, z, z, cos_t, sin_t)


SUM_ROWS = 16


def _flash_kernel(qi_ref, ki_ref, qt_ref, k_ref, vt_ref, lam_ref, g_ref, o_ref,
                  qd_ref, m_ref, acc_ref, *, tb, cq, lambda_init):
    p = pl.program_id(2)
    qi = qi_ref[p]
    ki = ki_ref[p]
    hd = DIFF_HEAD_DIM
    dv = DIFF_V_DIM

    @pl.when(ki == 0)
    def _():
        qd_ref[...] = jnp.zeros_like(qd_ref)
        qd_ref[0:hd, 0:tb] = qt_ref[0:hd, :]
        qd_ref[hd:2 * hd, tb:2 * tb] = qt_ref[hd:2 * hd, :]
        m_ref[...] = jnp.full_like(m_ref, NEG_BIG)
        acc_ref[...] = jnp.zeros_like(acc_ref)

    def step(diagonal):
        k = k_ref[...]
        vta = jnp.concatenate([vt_ref[...], jnp.ones((SUM_ROWS, tb), BF16)], axis=0)
        nc = 2 * tb // cq
        cols = [slice(c * cq, (c + 1) * cq) for c in range(nc)]
        nkey = [((c * cq) % tb + cq) if diagonal else tb for c in range(nc)]
        m_prev = m_ref[...]
        s = [_dot(k[0:nkey[c]], qd_ref[:, cols[c]]) for c in range(nc)]
        if diagonal:
            for c in range(nc):
                key = lax.broadcasted_iota(jnp.int32, (nkey[c], cq), 0)
                qpos = lax.broadcasted_iota(jnp.int32, (nkey[c], cq), 1) + (c * cq) % tb
                s[c] = jnp.where(key <= qpos, s[c], NEG_BIG)
        m_new = [jnp.maximum(m_prev[:, cols[c]], jnp.max(s[c], axis=0, keepdims=True)) for c in range(nc)]
        pexp = [jnp.exp2(s[c] - m_new[c]).astype(BF16) for c in range(nc)]
        alpha = [jnp.exp2(m_prev[:, cols[c]] - m_new[c]) for c in range(nc)]
        for c in range(nc):
            cs = cols[c]
            m_ref[:, cs] = m_new[c]
            acc_ref[:, cs] = alpha[c] * acc_ref[:, cs] + _dot(vta[:, 0:nkey[c]], pexp[c])

    @pl.when(ki == qi)
    def _():
        step(True)

    @pl.when(ki != qi)
    def _():
        step(False)

    @pl.when(ki == qi)
    def _():
        lam_p = lam_ref[...]
        lam = (jnp.exp(jnp.sum(lam_p[0:1] * lam_p[1:2], axis=-1, keepdims=True))
               - jnp.exp(jnp.sum(lam_p[2:3] * lam_p[3:4], axis=-1, keepdims=True)) + lambda_init)
        o = acc_ref[0:dv, :] / acc_ref[dv:dv + 1, :]
        o = (o[:, 0:tb] - lam * o[:, tb:2 * tb]).T
        o = o * lax.rsqrt(jnp.mean(o * o, axis=-1, keepdims=True) + RMS_EPS)
        o_ref[...] = (o * g_ref[...] * (1.0 - lambda_init)).astype(BF16)


def _flash_call(qt, kr, vt, lam_params, subln_g, bsz, seq, lambda_init, tb=1024, cq=256):
    t = bsz * seq
    tb = min(tb, seq)
    cq = min(cq, tb)
    nb = seq // tb
    qi_list, ki_list = [], []
    for qi in range(nb):
        for ki in range(qi + 1):
            qi_list.append(qi)
            ki_list.append(ki)
    qi_arr = jnp.asarray(qi_list, jnp.int32)
    ki_arr = jnp.asarray(ki_list, jnp.int32)
    hd = 2 * DIFF_HEAD_DIM
    grid_spec = pltpu.PrefetchScalarGridSpec(
        num_scalar_prefetch=2,
        grid=(bsz, DIFF_HEADS, len(qi_list)),
        in_specs=[pl.BlockSpec((None, hd, tb), lambda b, h, p, qi, ki: (b, h, qi[p])),
                  pl.BlockSpec((tb, hd), lambda b, h, p, qi, ki: (b * nb + ki[p], h)),
                  pl.BlockSpec((None, DIFF_V_DIM, tb), lambda b, h, p, qi, ki: (b, h, ki[p])),
                  pl.BlockSpec((4, DIFF_HEAD_DIM), lambda b, h, p, qi, ki: (0, 0)),
                  pl.BlockSpec((1, DIFF_V_DIM), lambda b, h, p, qi, ki: (0, 0))],
        out_specs=pl.BlockSpec((tb, DIFF_V_DIM), lambda b, h, p, qi, ki: (b * nb + qi[p], h)),
        scratch_shapes=[pltpu.VMEM((hd, 2 * tb), BF16),
                        pltpu.VMEM((1, 2 * tb), F32),
                        pltpu.VMEM((DIFF_V_DIM + SUM_ROWS, 2 * tb), F32)],
    )
    return pl.pallas_call(
        functools.partial(_flash_kernel, tb=tb, cq=cq, lambda_init=lambda_init),
        grid_spec=grid_spec,
        out_shape=jax.ShapeDtypeStruct((t, DIFF_HEADS * DIFF_V_DIM), BF16),
        compiler_params=_cparams(("parallel", "parallel", "arbitrary")),
        name="diff_flash",
    )(qi_arr, ki_arr, qt, kr, vt, lam_params, subln_g.reshape(1, DIFF_V_DIM))


def _merge_kernel(h_ref, y0_ref, y1_ref, y2_ref, y3_ref, wg_ref, bg_ref, wb_ref, wo_ref,
                  g_ref, b_ref, wr_ref, br_ref, o_ref, cls_ref, *, alpha):
    d = h_ref.shape[1]
    h = h_ref[...]
    hb = h.astype(BF16)
    merged = None
    for g, y_ref in enumerate((y0_ref, y1_ref, y2_ref, y3_ref)):
        gate = jax.nn.sigmoid(_dot(hb, wg_ref[g]) + bg_ref[g:g + 1, :])
        term = gate * _dot(y_ref[...], wb_ref[g])
        merged = term if merged is None else merged + term
    t = _dot(merged.astype(BF16), wo_ref[...])
    out = _layer_norm(alpha * h + t, g_ref[...], b_ref[...])
    o_ref[:, 0:d] = out
    cls, ga, gb = _route(out, wr_ref[...], br_ref[...])
    cls_ref[...] = cls
    g8 = jnp.concatenate([ga, gb, jnp.zeros((6, ga.shape[1]), F32)], axis=0)
    sel = (lax.broadcasted_iota(jnp.int32, (8, LANES), 0)
           == lax.broadcasted_iota(jnp.int32, (8, LANES), 1)).astype(F32)
    o_ref[:, d:d + LANES] = lax.dot_general(g8, sel, (((0,), (0,)), ((), ())),
                                            preferred_element_type=F32,
                                            precision=lax.Precision.HIGHEST)


def _merge_call(h, ys, wg, bg, wb, wo, ln_g, ln_b, w_router, b_router, alpha, tm=512):
    t, d = h.shape
    tm = min(tm, t)
    w = BRANCH_WIDTH
    const = dict(pipeline_mode=pl.Buffered(1))
    return pl.pallas_call(
        functools.partial(_merge_kernel, alpha=alpha),
        grid=(t // tm,),
        in_specs=[pl.BlockSpec((tm, d), lambda i: (i, 0))]
        + [pl.BlockSpec((tm, w), lambda i: (i, 0))] * N_BRANCH
        + [pl.BlockSpec((N_BRANCH, d, d), lambda i: (0, 0, 0), **const),
           pl.BlockSpec((N_BRANCH, d), lambda i: (0, 0)),
           pl.BlockSpec((N_BRANCH, w, d), lambda i: (0, 0, 0), **const),
           pl.BlockSpec((d, d), lambda i: (0, 0), **const),
           pl.BlockSpec((1, d), lambda i: (0, 0)),
           pl.BlockSpec((1, d), lambda i: (0, 0)),
           pl.BlockSpec((N_EXPERTS, d), lambda i: (0, 0)),
           pl.BlockSpec((N_EXPERTS, 1), lambda i: (0, 0))],
        out_specs=[pl.BlockSpec((tm, d + LANES), lambda i: (i, 0)),
                   pl.BlockSpec((1, tm), lambda i: (0, i))],
        out_shape=[jax.ShapeDtypeStruct((t, d + LANES), F32),
                   jax.ShapeDtypeStruct((1, t), jnp.int32)],
        compiler_params=_cparams(("parallel",)),
        name="merge",
    )(h, *ys, wg, bg, wb, wo, ln_g.reshape(1, d), ln_b.reshape(1, d),
      w_router.T, b_router.reshape(N_EXPERTS, 1))


N_PAIRS = 6
PAIR_A = (0, 0, 0, 1, 1, 3)
PAIR_B = (1, 2, 3, 3, 2, 2)
N_CLASSES = N_GROUPS * N_PAIRS
CLASS_ROWS = 32


def _route(h, wr, br):
    logits = lax.dot_general(wr, h, (((1,), (1,)), ((), ())),
                             preferred_element_type=F32, precision=lax.Precision.HIGHEST)
    mx = jnp.max(logits, axis=0, keepdims=True)
    e = jnp.exp(logits - mx)
    scores = e / jnp.sum(e, axis=0, keepdims=True)
    sel = scores + br
    epg = EXPERTS_PER_GROUP
    rows = [sel[i:i + 1, :] for i in range(N_EXPERTS)]
    srows = [scores[i:i + 1, :] for i in range(N_EXPERTS)]
    best_score = None
    best = None
    for g in range(N_GROUPS):
        r = rows[g * epg:(g + 1) * epg]
        gs = None
        for a in range(epg):
            for b in range(a + 1, epg):
                pair = r[a] + r[b]
                gs = pair if gs is None else jnp.maximum(gs, pair)
        if best is None:
            best_score, best = gs, jnp.zeros(gs.shape, jnp.int32)
        else:
            take = gs > best_score
            best_score = jnp.where(take, gs, best_score)
            best = jnp.where(take, g, best)
    cand, cscore = [], []
    for j in range(epg):
        c = rows[j]
        s = srows[j]
        for g in range(1, N_GROUPS):
            c = jnp.where(best == g, rows[g * epg + j], c)
            s = jnp.where(best == g, srows[g * epg + j], s)
        cand.append(c)
        cscore.append(s)
    v1, i1, s1 = cand[0], jnp.zeros(best.shape, jnp.int32), cscore[0]
    for j in range(1, epg):
        take = cand[j] > v1
        v1 = jnp.where(take, cand[j], v1)
        i1 = jnp.where(take, j, i1)
        s1 = jnp.where(take, cscore[j], s1)
    v2 = jnp.full(v1.shape, -jnp.inf, F32)
    i2 = jnp.zeros(best.shape, jnp.int32)
    s2 = jnp.zeros(v1.shape, F32)
    for j in range(epg):
        take = jnp.logical_and(i1 != j, cand[j] > v2)
        v2 = jnp.where(take, cand[j], v2)
        i2 = jnp.where(take, j, i2)
        s2 = jnp.where(take, cscore[j], s2)
    denom = s1 + s2
    g1 = s1 / denom
    g2 = s2 / denom
    lo = jnp.minimum(i1, i2)
    hi = jnp.maximum(i1, i2)
    pair = jnp.where(lo == 0, hi - 1, jnp.where(lo == 1, jnp.where(hi == 3, 3, 4), 5))
    a_loc = jnp.where(pair < 3, 0, jnp.where(pair < 5, 1, 3))
    first_is_a = i1 == a_loc
    return (best * N_PAIRS + pair, jnp.where(first_is_a, g1, g2), jnp.where(first_is_a, g2, g1))


def _rank_kernel(cls_ref, dest_ref, cnt_ref, run_ref, start_ref, *, blk):
    phase = pl.program_id(0)
    i = pl.program_id(1)
    tm = cls_ref.shape[1]
    onehot = lax.broadcasted_iota(jnp.int32, (CLASS_ROWS, tm), 0) == cls_ref[...]
    f = jnp.where(onehot, 1.0, 0.0)
    tot = jnp.sum(f, axis=1, keepdims=True)

    @pl.when(jnp.logical_and(phase == 0, i == 0))
    def _():
        run_ref[...] = jnp.zeros_like(run_ref)

    @pl.when(phase == 0)
    def _():
        run_ref[...] = run_ref[...] + tot

    @pl.when(jnp.logical_and(phase == 1, i == 0))
    def _():
        cnt = run_ref[...]
        cnt_ref[...] = cnt.astype(jnp.int32)
        padded = jnp.ceil(cnt * (1.0 / blk)) * blk
        acc = jnp.zeros((1, LANES), F32)
        for c in range(CLASS_ROWS):
            start_ref[c:c + 1, :] = acc
            acc = acc + padded[c:c + 1, :]
        run_ref[...] = jnp.zeros_like(run_ref)

    @pl.when(phase == 1)
    def _():
        s_idx = lax.broadcasted_iota(jnp.int32, (tm, tm), 0)
        t_idx = lax.broadcasted_iota(jnp.int32, (tm, tm), 1)
        tri = jnp.where(s_idx < t_idx, 1.0, 0.0).astype(BF16)
        before = _dot(f.astype(BF16), tri)
        base = start_ref[:, 0:1] + run_ref[:, 0:1]
        dest = jnp.sum(jnp.where(onehot, base + before, 0.0), axis=0, keepdims=True)
        dest_ref[...] = dest.astype(jnp.int32)
        run_ref[...] = run_ref[...] + tot


def _rank_call(cls, blk, tm=512):
    t = cls.shape[1]
    tm = min(tm, t)
    return pl.pallas_call(
        functools.partial(_rank_kernel, blk=blk),
        grid=(2, t // tm),
        in_specs=[pl.BlockSpec((1, tm), lambda p, i: (0, i))],
        out_specs=[pl.BlockSpec((1, tm), lambda p, i: (0, i * p)),
                   pl.BlockSpec((CLASS_ROWS, LANES), lambda p, i: (0, 0))],
        out_shape=[jax.ShapeDtypeStruct((1, t), jnp.int32),
                   jax.ShapeDtypeStruct((CLASS_ROWS, LANES), jnp.int32)],
        scratch_shapes=[pltpu.VMEM((CLASS_ROWS, LANES), F32),
                        pltpu.VMEM((CLASS_ROWS, LANES), F32)],
        compiler_params=_cparams(("arbitrary", "arbitrary")),
        name="moe_rank",
    )(cls)


def _swiglu(xb, wb1, wb3, wb2):
    a = _dot(xb, wb1[...])
    b = _dot(xb, wb3[...])
    return _dot((a * jax.nn.sigmoid(a) * b).astype(BF16), wb2[...])


def _cast_expert_weights(i, nb, be_ref, w1_ref, w3_ref, w2_ref, wb1, wb3, wb2):
    new_expert = jnp.logical_or(i == 0, be_ref[i] != be_ref[jnp.maximum(i - 1, 0)])

    @pl.when(jnp.logical_and(i < nb, new_expert))
    def _():
        wb1[...] = w1_ref[0].astype(BF16)
        wb3[...] = w3_ref[0].astype(BF16)
        wb2[...] = w2_ref[0].astype(BF16)


def _fill_row_map(inv_ref, dest_ref, ve_ref, pe_ref, t, pad_value):
    def fill(tok, carry):
        inv_ref[dest_ref[tok]] = tok
        return carry

    lax.fori_loop(0, t, fill, 0, unroll=8)

    def pad(r, carry):
        inv_ref[r] = pad_value(r)
        return carry

    for c in range(N_CLASSES):
        lax.fori_loop(ve_ref[c], pe_ref[c], pad, 0)


def _experts_a_kernel(dest_ref, be_ref, ve_ref, pe_ref, nb_ref, h_ref, w1_ref, w3_ref, w2_ref,
                      xs_ref, part_ref, inv_ref, xbuf0, xbuf1, wb1, wb3, wb2, gsem, *, blk, t, d):
    i = pl.program_id(0)
    nb = nb_ref[0]
    xbuf = (xbuf0, xbuf1)

    def gather(block, s):
        base = block * blk
        for j in range(blk):
            pltpu.make_async_copy(h_ref.at[pl.ds(inv_ref[base + j], 1), :], xbuf[s].at[pl.ds(j, 1), :],
                                  gsem.at[s]).start(priority=j % 2)

    def wait_gather(s):
        for j in range(blk):
            pltpu.make_async_copy(h_ref.at[pl.ds(0, 1), :], xbuf[s].at[pl.ds(j, 1), :],
                                  gsem.at[s]).wait()

    @pl.when(i == 0)
    def _():
        _fill_row_map(inv_ref, dest_ref, ve_ref, pe_ref, t, lambda r: 0)
        gather(0, 0)

    _cast_expert_weights(i, nb, be_ref, w1_ref, w3_ref, w2_ref, wb1, wb3, wb2)

    def block_step(s):
        wait_gather(s)
        gather(jnp.minimum(i + 1, nb - 1), 1 - s)
        x = xbuf[s][...]
        xs_ref[...] = x
        part_ref[...] = x[:, d:d + 1] * _swiglu(x[:, 0:d].astype(BF16), wb1, wb3, wb2)

        @pl.when(i == nb - 1)
        def _():
            wait_gather(1 - s)

    for s in range(2):
        @pl.when(jnp.logical_and(i < nb, i % 2 == s))
        def _(s=s):
            block_step(s)

    @pl.when(i >= nb)
    def _():
        xs_ref[...] = jnp.zeros_like(xs_ref)
        part_ref[...] = jnp.zeros_like(part_ref)


def _experts_b_kernel(dest_ref, be_ref, ve_ref, pe_ref, nb_ref, xs_ref, part_ref, w1_ref, w3_ref, w2_ref,
                      y_ref, inv_ref, obuf0, obuf1, wb1, wb3, wb2, ssem, *, blk, t, d, n_blk):
    i = pl.program_id(0)
    nb = nb_ref[0]
    obuf = (obuf0, obuf1)

    def scatter(block, s):
        base = block * blk
        for j in range(blk):
            pltpu.make_async_copy(obuf[s].at[pl.ds(j, 1), :], y_ref.at[pl.ds(inv_ref[base + j], 1), :],
                                  ssem.at[s]).start(priority=j % 2)

    def wait_scatter(s):
        for j in range(blk):
            pltpu.make_async_copy(obuf[s].at[pl.ds(j, 1), :], y_ref.at[pl.ds(0, 1), :],
                                  ssem.at[s]).wait()

    @pl.when(i == 0)
    def _():
        _fill_row_map(inv_ref, dest_ref, ve_ref, pe_ref, t, lambda r: t + (r & (2 * blk - 1)))

        def dummy(j, carry):
            inv_ref[n_blk * blk + j] = t + 2 * blk + j
            return carry

        lax.fori_loop(0, blk, dummy, 0)
        obuf1[...] = jnp.zeros_like(obuf1)
        for q in range(3):
            zero_dump = pltpu.make_async_copy(obuf1, y_ref.at[pl.ds(t + q * blk, blk), :], ssem.at[0])
            zero_dump.start()
            zero_dump.wait()

    _cast_expert_weights(i, nb, be_ref, w1_ref, w3_ref, w2_ref, wb1, wb3, wb2)

    def block_step(s):
        @pl.when(i > 0)
        def _():
            wait_scatter(s)

        scatter(jnp.where(i == 0, n_blk, i - 1), 1 - s)
        x = xs_ref[...]
        obuf[s][...] = part_ref[...] + x[:, d + 1:d + 2] * _swiglu(x[:, 0:d].astype(BF16), wb1, wb3, wb2)

        @pl.when(i == nb - 1)
        def _():
            wait_scatter(1 - s)
            scatter(i, s)
            wait_scatter(s)

    for s in range(2):
        @pl.when(jnp.logical_and(i < nb, i % 2 == s))
        def _(s=s):
            block_step(s)


def _experts_calls(haug, dest, blk_a, blk_b, valid_end, pad_end, n_used, w1, w3, w2, layer, blk):
    t, da = haug.shape
    d = da - LANES
    de = w1.shape[3]
    n_blk = blk_a.shape[0]
    assert blk & (blk - 1) == 0, "block size must be a power of two"
    wmap = lambda i, dest, be, ve, pe, nb: (layer, be[i], 0, 0)
    rows = lambda i, dest, be, ve, pe, nb: (i, 0)
    wspecs = [pl.BlockSpec((None, 1, d, de), wmap),
              pl.BlockSpec((None, 1, d, de), wmap),
              pl.BlockSpec((None, 1, de, d), wmap)]
    wscratch = [pltpu.VMEM((d, de), BF16), pltpu.VMEM((d, de), BF16), pltpu.VMEM((de, d), BF16)]
    xs, part = pl.pallas_call(
        functools.partial(_experts_a_kernel, blk=blk, t=t, d=d),
        grid_spec=pltpu.PrefetchScalarGridSpec(
            num_scalar_prefetch=5,
            grid=(n_blk,),
            in_specs=[pl.BlockSpec(memory_space=pl.ANY)] + wspecs,
            out_specs=[pl.BlockSpec((blk, da), rows), pl.BlockSpec((blk, d), rows)],
            scratch_shapes=[pltpu.SMEM((n_blk * blk,), jnp.int32),
                            pltpu.VMEM((blk, da), F32),
                            pltpu.VMEM((blk, da), F32)] + wscratch
            + [pltpu.SemaphoreType.DMA((2,))],
        ),
        out_shape=[jax.ShapeDtypeStruct((n_blk * blk, da), F32),
                   jax.ShapeDtypeStruct((n_blk * blk, d), F32)],
        compiler_params=_cparams(("arbitrary",)),
        name="moe_experts_a",
    )(dest, blk_a, valid_end, pad_end, n_used, haug, w1, w3, w2)
    return pl.pallas_call(
        functools.partial(_experts_b_kernel, blk=blk, t=t, d=d, n_blk=n_blk),
        grid_spec=pltpu.PrefetchScalarGridSpec(
            num_scalar_prefetch=5,
            grid=(n_blk,),
            in_specs=[pl.BlockSpec((blk, da), rows), pl.BlockSpec((blk, d), rows)] + wspecs,
            out_specs=pl.BlockSpec(memory_space=pl.ANY),
            scratch_shapes=[pltpu.SMEM(((n_blk + 1) * blk,), jnp.int32),
                            pltpu.VMEM((blk, d), F32),
                            pltpu.VMEM((blk, d), F32)] + wscratch
            + [pltpu.SemaphoreType.DMA((2,))],
        ),
        out_shape=jax.ShapeDtypeStruct((t + 3 * blk, d), F32),
        compiler_params=_cparams(("arbitrary",)),
        name="moe_experts_b",
    )(dest, blk_b, valid_end, pad_end, n_used, xs, part, w1, w3, w2)


def _combine_kernel(h_ref, y_ref, g_ref, b_ref, o_ref, *, alpha):
    o_ref[...] = _layer_norm(alpha * h_ref[...] + y_ref[...], g_ref[...], b_ref[...])


def _combine_call(haug, y, ln_g, ln_b, alpha, tm=512):
    t = haug.shape[0]
    d = y.shape[1]
    tm = min(tm, t)
    return pl.pallas_call(
        functools.partial(_combine_kernel, alpha=alpha),
        grid=(t // tm,),
        in_specs=[pl.BlockSpec((tm, d), lambda i: (i, 0)),
                  pl.BlockSpec((tm, d), lambda i: (i, 0)),
                  pl.BlockSpec((1, d), lambda i: (0, 0)),
                  pl.BlockSpec((1, d), lambda i: (0, 0))],
        out_specs=pl.BlockSpec((tm, d), lambda i: (i, 0)),
        out_shape=jax.ShapeDtypeStruct((t, d), F32),
        compiler_params=_cparams(("parallel",)),
        name="moe_combine",
    )(haug, y, ln_g.reshape(1, d), ln_b.reshape(1, d))


def _moe_layer(haug, cls, w1, w3, w2, layer, ln_g, ln_b, alpha, blk=256):
    t = haug.shape[0]
    dest, counts = _rank_call(cls, blk)
    n_blk = t // blk + N_CLASSES
    cnt = counts[:, 0]
    padded = ((cnt + blk - 1) // blk) * blk
    pad_end = jnp.cumsum(padded)
    blk_start = jnp.arange(n_blk, dtype=jnp.int32) * blk
    blk_cls = jnp.minimum(jnp.sum(pad_end[None, :] <= blk_start[:, None], axis=1), N_CLASSES - 1)
    group = blk_cls // N_PAIRS
    pair = blk_cls % N_PAIRS
    blk_a = (group * EXPERTS_PER_GROUP + jnp.asarray(PAIR_A, jnp.int32)[pair]).astype(jnp.int32)
    blk_b = (group * EXPERTS_PER_GROUP + jnp.asarray(PAIR_B, jnp.int32)[pair]).astype(jnp.int32)
    valid_end = (pad_end - padded + cnt).astype(jnp.int32)
    n_used = (pad_end[N_CLASSES - 1:N_CLASSES] // blk).astype(jnp.int32)
    y = _experts_calls(haug, dest.reshape(-1), blk_a, blk_b, valid_end, pad_end.astype(jnp.int32), n_used,
                       w1, w3, w2, layer, blk)
    return _combine_call(haug, y, ln_g, ln_b, alpha)


def kernel(x, positions, ln_in_g, ln_in_b, w_in, conv_w, sgu_ln_g, sgu_ln_b, w_s, b_s, lambda_q1, lambda_k1, lambda_q2, lambda_k2, diff_subln_g, w_gate, b_gate, w_branch, w_o, ln1_g, ln1_b, w_router, b_router, w1, w3, w2, ln2_g, ln2_b):
    bsz, seq, d = x.shape
    depth = w_in.shape[0]
    alpha = (2.0 * depth) ** 0.25
    t = bsz * seq
    cos_t, sin_t = _rope_tables(positions)
    h = _ln_call(x.reshape(t, d), ln_in_g, ln_in_b)
    for l in range(depth):
        lambda_init = 0.8 - 0.6 * math.exp(-0.3 * l)
        z = _proj_call(h, w_in[l].astype(BF16))
        y_conv = _conv_call(z, conv_w[l], bsz, seq)
        y_ret = _ret_call(z, cos_t, sin_t, bsz, seq)
        y_sgu = _sgu_call(z, sgu_ln_g[l], sgu_ln_b[l], w_s[l], b_s[l])
        qt, kr, vt = _qkprep_call(z, cos_t, sin_t, bsz, seq)
        lam_params = jnp.stack([lambda_q1[l], lambda_k1[l], lambda_q2[l], lambda_k2[l]])
        y_diff = _flash_call(qt, kr, vt, lam_params, diff_subln_g[l], bsz, seq, lambda_init)
        haug, cls = _merge_call(h, (y_conv, y_ret, y_sgu, y_diff), w_gate[l].astype(BF16), b_gate[l],
                                w_branch[l].astype(BF16), w_o[l].astype(BF16), ln1_g[l], ln1_b[l],
                                w_router, b_router, alpha)
        h = _moe_layer(haug, cls, w1, w3, w2, l, ln2_g[l], ln2_b[l], alpha)
    return h.reshape(bsz, seq, d)
```

```python
import functools
import math

import jax
import jax.numpy as jnp
from jax import lax
from jax.experimental import pallas as pl
from jax.experimental.pallas import tpu as pltpu

D_MODEL = 1024
BRANCH_WIDTH = 512
N_BRANCH = 4
CONV_WIDTH = BRANCH_WIDTH
CONV_K = 3
RET_HEADS = 4
RET_DK = 64
RET_DV = 128
RET_CHUNK = 128
RET_THETA = 10000.0
SGU_GROUPS = 4
SGU_GROUP_DIM = BRANCH_WIDTH // SGU_GROUPS
SGU_CHUNK = 128
SGU_WIDTH = BRANCH_WIDTH
DIFF_HEADS = 4
DIFF_HEAD_DIM = 64
DIFF_V_DIM = 2 * DIFF_HEAD_DIM
ROPE_THETA = 500000.0
ROT_DIM = DIFF_HEAD_DIM // 4
CONV_COLS = 3 * CONV_WIDTH
RET_COLS = 2 * RET_HEADS * RET_DK + 2 * RET_HEADS * RET_DV
SGU_COLS = 2 * SGU_WIDTH
DIFF_COLS = 2 * DIFF_HEADS * 2 * DIFF_HEAD_DIM + DIFF_HEADS * DIFF_V_DIM
IN_COLS = CONV_COLS + RET_COLS + SGU_COLS + DIFF_COLS
N_EXPERTS = 16
N_GROUPS = 4
EXPERTS_PER_GROUP = N_EXPERTS // N_GROUPS
TOP_K = 2
D_EXPERT = 1024
LN_EPS = 1e-5
RMS_EPS = 1e-6

LANES = 128
CONV_OFF = 0
RET_OFF = CONV_COLS
SGU_OFF = RET_OFF + RET_COLS
DIFF_OFF = SGU_OFF + SGU_COLS

NEG_BIG = -1e30
VMEM_LIMIT = 56 * 1024 * 1024

BF16 = jnp.bfloat16
F32 = jnp.float32


def _cparams(sem):
    return pltpu.CompilerParams(dimension_semantics=sem, vmem_limit_bytes=VMEM_LIMIT)


def _layer_norm(xf, g, b):
    mu = jnp.mean(xf, axis=-1, keepdims=True)
    xc = xf - mu
    var = jnp.mean(xc * xc, axis=-1, keepdims=True)
    return xc * lax.rsqrt(var + LN_EPS) * g + b


def _dot(a, b):
    return jnp.dot(a, b, preferred_element_type=F32)


def _dot_nt(a, b):
    return lax.dot_general(a, b, (((1,), (1,)), ((), ())), preferred_element_type=F32)


def _dot_tn(a, b):
    return lax.dot_general(a, b, (((0,), (0,)), ((), ())), preferred_element_type=F32)


def _ln_kernel(x_ref, g_ref, b_ref, o_ref):
    o_ref[...] = _layer_norm(x_ref[...], g_ref[...], b_ref[...])


def _ln_call(x2d, g, b, tm=1024):
    t, d = x2d.shape
    return pl.pallas_call(
        _ln_kernel,
        grid=(t // tm,),
        in_specs=[pl.BlockSpec((tm, d), lambda i: (i, 0)),
                  pl.BlockSpec((1, d), lambda i: (0, 0)),
                  pl.BlockSpec((1, d), lambda i: (0, 0))],
        out_specs=pl.BlockSpec((tm, d), lambda i: (i, 0)),
        out_shape=jax.ShapeDtypeStruct((t, d), F32),
        compiler_params=_cparams(("parallel",)),
        name="ln_in",
    )(x2d, g.reshape(1, d), b.reshape(1, d))


def _rope_kernel(pos_ref, freq_ref, sign_ref, cos_ref, sin_ref):
    ang = pos_ref[...] * freq_ref[...]
    cos_ref[...] = jnp.cos(ang)
    sin_ref[...] = jnp.sin(ang) * sign_ref[...]


def _rope_tables(positions):
    t = positions.size
    pos = positions.reshape(t, 1).astype(F32)
    half_r = RET_DK // 2
    fr = RET_THETA ** (-jnp.arange(half_r, dtype=F32) / half_r)
    half_d = ROT_DIM // 2
    fd = ROPE_THETA ** (-jnp.arange(half_d, dtype=F32) / half_d)
    zeros_d = jnp.zeros((DIFF_HEAD_DIM - ROT_DIM,), F32)
    freq = jnp.concatenate([fr, fr, fd, fd, zeros_d]).reshape(1, LANES)
    sign = jnp.concatenate([-jnp.ones((half_r,), F32), jnp.ones((half_r,), F32),
                            -jnp.ones((half_d,), F32), jnp.ones((half_d,), F32),
                            zeros_d]).reshape(1, LANES)
    tm = min(t, 2048)
    return pl.pallas_call(
        _rope_kernel,
        grid=(t // tm,),
        in_specs=[pl.BlockSpec((tm, 1), lambda i: (i, 0)),
                  pl.BlockSpec((1, LANES), lambda i: (0, 0)),
                  pl.BlockSpec((1, LANES), lambda i: (0, 0))],
        out_specs=[pl.BlockSpec((tm, LANES), lambda i: (i, 0)),
                   pl.BlockSpec((tm, LANES), lambda i: (i, 0))],
        out_shape=[jax.ShapeDtypeStruct((t, LANES), F32)] * 2,
        compiler_params=_cparams(("parallel",)),
        name="rope_tables",
    )(pos, freq, sign)


def _tile_lanes(x, reps):
    return jnp.concatenate([x] * reps, axis=1)


def _rotate_half_split(x, cos, sin_signed, group, half):
    w = x.shape[1]
    lane = lax.broadcasted_iota(jnp.int32, x.shape, 1) % group
    partner = jnp.where(lane < half, pltpu.roll(x, w - half, axis=1), pltpu.roll(x, half, axis=1))
    return x * cos + partner * sin_signed


def _proj_kernel(h_ref, w_ref, z_ref, *, chunk):
    hb = h_ref[...].astype(BF16)
    for n0 in range(0, IN_COLS, chunk):
        z_ref[:, n0:n0 + chunk] = _dot(hb, w_ref[:, n0:n0 + chunk]).astype(BF16)


def _proj_call(h, w_bf16, tm=512):
    t, d = h.shape
    return pl.pallas_call(
        functools.partial(_proj_kernel, chunk=512),
        grid=(t // tm,),
        in_specs=[pl.BlockSpec((tm, d), lambda i: (i, 0)),
                  pl.BlockSpec((d, IN_COLS), lambda i: (0, 0), pipeline_mode=pl.Buffered(1))],
        out_specs=pl.BlockSpec((tm, IN_COLS), lambda i: (i, 0)),
        out_shape=jax.ShapeDtypeStruct((t, IN_COLS), BF16),
        compiler_params=_cparams(("parallel",)),
        name="proj_in",
    )(h, w_bf16)


CONV_HALO = 16


def _conv_kernel(b_ref, c_ref, u_ref, ch_ref, uh_ref, w_ref, o_ref):
    ts = c_ref.shape[0]
    first = pl.program_id(1) == 0
    cu = c_ref[...].astype(F32) * u_ref[...].astype(F32)
    halo = ch_ref[...].astype(F32) * uh_ref[...].astype(F32)
    halo = jnp.where(first, 0.0, halo)
    ext = jnp.concatenate([halo, cu], axis=0)
    n = ext.shape[0]
    prev1 = pltpu.roll(ext, 1, axis=0)[CONV_HALO:n]
    prev2 = pltpu.roll(ext, 2, axis=0)[CONV_HALO:n]
    w = w_ref[...]
    y = prev2 * w[0:1, :] + prev1 * w[1:2, :] + cu * w[2:3, :]
    o_ref[...] = (b_ref[...].astype(F32) * y).astype(BF16)


def _conv_call(z, conv_w, bsz, seq, ts=1024):
    t = bsz * seq
    ts = min(ts, seq)
    nt = seq // ts
    wb = CONV_WIDTH
    hb = ts // CONV_HALO
    col = lambda k: (lambda b, i: (b * nt + i, CONV_OFF // wb + k))
    halo = lambda k: (lambda b, i: (jnp.maximum((b * nt + i) * hb - 1, 0), CONV_OFF // wb + k))
    return pl.pallas_call(
        _conv_kernel,
        grid=(bsz, nt),
        in_specs=[pl.BlockSpec((ts, wb), col(0)),
                  pl.BlockSpec((ts, wb), col(1)),
                  pl.BlockSpec((ts, wb), col(2)),
                  pl.BlockSpec((CONV_HALO, wb), halo(1)),
                  pl.BlockSpec((CONV_HALO, wb), halo(2)),
                  pl.BlockSpec((CONV_K, wb), lambda b, i: (0, 0))],
        out_specs=pl.BlockSpec((ts, wb), lambda b, i: (b * nt + i, 0)),
        out_shape=jax.ShapeDtypeStruct((t, wb), BF16),
        compiler_params=_cparams(("parallel", "parallel")),
        name="conv_mixer",
    )(z, z, z, z, z, conv_w.reshape(CONV_K, wb))


def _sgu_kernel(u_ref, v_ref, g_ref, b_ref, ws_ref, bias_ref, o_ref):
    ts = u_ref.shape[0]
    c = SGU_CHUNK
    v = _layer_norm(v_ref[...].astype(F32), g_ref[...], b_ref[...]).astype(BF16)
    row = lax.broadcasted_iota(jnp.int32, (c, c), 0)
    colm = lax.broadcasted_iota(jnp.int32, (c, c), 1)
    bias = bias_ref[...]
    for g in range(SGU_GROUPS):
        w = jnp.where(row >= colm, ws_ref[g], 0.0).astype(BF16)
        lo = g * SGU_GROUP_DIM
        for n in range(ts // c):
            s = _dot(w, v[n * c:(n + 1) * c, lo:lo + SGU_GROUP_DIM]) + bias[:, lo:lo + SGU_GROUP_DIM]
            u = u_ref[n * c:(n + 1) * c, lo:lo + SGU_GROUP_DIM].astype(F32)
            o_ref[n * c:(n + 1) * c, lo:lo + SGU_GROUP_DIM] = (u * s).astype(BF16)


def _sgu_call(z, ln_g, ln_b, w_s, b_s, ts=512):
    t = z.shape[0]
    ts = min(ts, t)
    wb = SGU_WIDTH
    bias = jnp.repeat(b_s.T, SGU_GROUP_DIM, axis=1)
    return pl.pallas_call(
        _sgu_kernel,
        grid=(t // ts,),
        in_specs=[pl.BlockSpec((ts, wb), lambda i: (i, SGU_OFF // wb)),
                  pl.BlockSpec((ts, wb), lambda i: (i, SGU_OFF // wb + 1)),
                  pl.BlockSpec((1, wb), lambda i: (0, 0)),
                  pl.BlockSpec((1, wb), lambda i: (0, 0)),
                  pl.BlockSpec((SGU_GROUPS, SGU_CHUNK, SGU_CHUNK), lambda i: (0, 0, 0)),
                  pl.BlockSpec((SGU_CHUNK, wb), lambda i: (0, 0))],
        out_specs=pl.BlockSpec((ts, wb), lambda i: (i, 0)),
        out_shape=jax.ShapeDtypeStruct((t, wb), BF16),
        compiler_params=_cparams(("parallel",)),
        name="sgu_mixer",
    )(z, z, ln_g.reshape(1, wb), ln_b.reshape(1, wb), w_s, bias)


def _ret_tables():
    c = RET_CHUNK
    log_gamma = jnp.log1p(-jnp.exp2(-5.0 - jnp.arange(RET_HEADS, dtype=F32)))
    idx = jnp.arange(c, dtype=F32)
    rel = idx[:, None] - idx[None, :]
    decay = jnp.where(rel >= 0, jnp.exp(jnp.maximum(rel, 0.0)[None] * log_gamma[:, None, None]), 0.0)
    k_decay = jnp.exp((c - 1 - idx)[:, None] * log_gamma[None, :])
    q_decay = jnp.exp((idx + 1.0)[:, None] * log_gamma[None, :])
    chunk_decay = jnp.exp(c * log_gamma)
    hk = RET_HEADS * RET_DK
    kd = jnp.repeat(k_decay, RET_DK, axis=1) * (RET_DK ** -0.5)
    qd = jnp.repeat(q_decay, RET_DK, axis=1)
    cd = jnp.broadcast_to(jnp.repeat(chunk_decay, RET_DV)[None, :], (8, RET_HEADS * RET_DV))
    del hk
    return decay, kd, qd, cd


def _ret_kernel(q_ref, k_ref, v_ref, g_ref, cos_ref, sin_ref, decay_ref, kd_ref, qd_ref, cd_ref,
                o_ref, state_ref):
    ts = q_ref.shape[0]
    c = RET_CHUNK
    hk = RET_HEADS * RET_DK

    @pl.when(pl.program_id(1) == 0)
    def _():
        state_ref[...] = jnp.zeros_like(state_ref)

    cos = _tile_lanes(cos_ref[:, 0:RET_DK], RET_HEADS)
    sin = _tile_lanes(sin_ref[:, 0:RET_DK], RET_HEADS)
    q = _rotate_half_split(q_ref[...].astype(F32), cos, sin, RET_DK, RET_DK // 2)
    k = _rotate_half_split(k_ref[...].astype(F32), cos, sin, RET_DK, RET_DK // 2)
    del hk
    for n in range(ts // c):
        r0 = n * c
        qn = q[r0:r0 + c]
        kn = k[r0:r0 + c]
        qb = qn.astype(BF16)
        kb = (kn * (RET_DK ** -0.5)).astype(BF16)
        qdb = (qn * qd_ref[...]).astype(BF16)
        kdb = (kn * kd_ref[...]).astype(BF16)
        for h in range(RET_HEADS):
            ks = slice(h * RET_DK, (h + 1) * RET_DK)
            vs = slice(h * RET_DV, (h + 1) * RET_DV)
            vb = v_ref[r0:r0 + c, vs]
            scores = _dot_nt(qb[:, ks], kb[:, ks]) * decay_ref[h]
            inner = _dot(scores.astype(BF16), vb)
            state = state_ref[h]
            cross = _dot(qdb[:, ks], state.astype(BF16))
            kv = _dot_tn(kdb[:, ks], vb)
            state_ref[h] = state * cd_ref[0:1, vs] + kv
            o = inner + cross
            o = o * lax.rsqrt(jnp.mean(o * o, axis=-1, keepdims=True) + RMS_EPS)
            gate = g_ref[r0:r0 + c, vs].astype(F32)
            gate = gate * jax.nn.sigmoid(gate)
            o_ref[r0:r0 + c, vs] = (gate * o).astype(BF16)


def _ret_call(z, cos_t, sin_t, bsz, seq, ts=512):
    t = bsz * seq
    ts = min(ts, seq)
    nt = seq // ts
    decay, kd, qd, cd = _ret_tables()
    hk = RET_HEADS * RET_DK
    hv = RET_HEADS * RET_DV
    row = lambda b, i: b * nt + i
    return pl.pallas_call(
        _ret_kernel,
        grid=(bsz, nt),
        in_specs=[pl.BlockSpec((ts, hk), lambda b, i: (row(b, i), RET_OFF // hk)),
                  pl.BlockSpec((ts, hk), lambda b, i: (row(b, i), RET_OFF // hk + 1)),
                  pl.BlockSpec((ts, hv), lambda b, i: (row(b, i), (RET_OFF + 2 * hk) // hv)),
                  pl.BlockSpec((ts, hv), lambda b, i: (row(b, i), (RET_OFF + 2 * hk) // hv + 1)),
                  pl.BlockSpec((ts, LANES), lambda b, i: (row(b, i), 0)),
                  pl.BlockSpec((ts, LANES), lambda b, i: (row(b, i), 0)),
                  pl.BlockSpec((RET_HEADS, RET_CHUNK, RET_CHUNK), lambda b, i: (0, 0, 0)),
                  pl.BlockSpec((RET_CHUNK, hk), lambda b, i: (0, 0)),
                  pl.BlockSpec((RET_CHUNK, hk), lambda b, i: (0, 0)),
                  pl.BlockSpec((8, hv), lambda b, i: (0, 0))],
        out_specs=pl.BlockSpec((ts, hv), lambda b, i: (row(b, i), 0)),
        out_shape=jax.ShapeDtypeStruct((t, hv), BF16),
        scratch_shapes=[pltpu.VMEM((RET_HEADS, RET_DK, RET_DV), F32)],
        compiler_params=_cparams(("parallel", "arbitrary")),
        name="ret_mixer",
    )(z, z, z, z, cos_t, sin_t, decay, kd, qd, cd)


LOG2E = 1.4426950408889634


def _qkprep_kernel(q_ref, k_ref, v_ref, cos_ref, sin_ref, qt_ref, ko_ref, vt_ref):
    reps = q_ref.shape[1] // DIFF_HEAD_DIM
    cos = _tile_lanes(cos_ref[:, DIFF_HEAD_DIM:2 * DIFF_HEAD_DIM], reps)
    sin = _tile_lanes(sin_ref[:, DIFF_HEAD_DIM:2 * DIFF_HEAD_DIM], reps)
    q = _rotate_half_split(q_ref[...].astype(F32), cos, sin, DIFF_HEAD_DIM, ROT_DIM // 2)
    k = _rotate_half_split(k_ref[...].astype(F32), cos, sin, DIFF_HEAD_DIM, ROT_DIM // 2)
    qt_ref[...] = (q * (DIFF_HEAD_DIM ** -0.5 * LOG2E)).T.astype(BF16)
    ko_ref[...] = k.astype(BF16)
    vt_ref[...] = v_ref[...].astype(F32).T.astype(BF16)


def _qkprep_call(z, cos_t, sin_t, bsz, seq, tm=512):
    t = bsz * seq
    tm = min(tm, seq)
    nt = seq // tm
    hq = DIFF_HEADS * 2 * DIFF_HEAD_DIM
    row = lambda b, i: b * nt + i
    return pl.pallas_call(
        _qkprep_kernel,
        grid=(bsz, nt),
        in_specs=[pl.BlockSpec((tm, hq), lambda b, i: (row(b, i), DIFF_OFF // hq)),
                  pl.BlockSpec((tm, hq), lambda b, i: (row(b, i), DIFF_OFF // hq + 1)),
                  pl.BlockSpec((tm, hq), lambda b, i: (row(b, i), DIFF_OFF // hq + 2)),
                  pl.BlockSpec((tm, LANES), lambda b, i: (row(b, i), 0)),
                  pl.BlockSpec((tm, LANES), lambda b, i: (row(b, i), 0))],
        out_specs=[pl.BlockSpec((None, hq, tm), lambda b, i: (b, 0, i)),
                   pl.BlockSpec((tm, hq), lambda b, i: (row(b, i), 0)),
                   pl.BlockSpec((None, hq, tm), lambda b, i: (b, 0, i))],
        out_shape=[jax.ShapeDtypeStruct((bsz, hq, seq), BF16),
                   jax.ShapeDtypeStruct((t, hq), BF16),
                   jax.ShapeDtypeStruct((bsz, hq, seq), BF16)],
        compiler_params=_cparams(("parallel", "parallel")),
        name="diff_qkprep",
    )(z, z, z, cos_t, sin_t)


SUM_ROWS = 16


def _flash_kernel(qi_ref, ki_ref, qt_ref, k_ref, vt_ref, lam_ref, g_ref, o_ref,
                  qd_ref, m_ref, acc_ref, *, tb, cq, lambda_init):
    p = pl.program_id(2)
    qi = qi_ref[p]
    ki = ki_ref[p]
    hd = DIFF_HEAD_DIM
    dv = DIFF_V_DIM

    @pl.when(ki == 0)
    def _():
        qd_ref[...] = jnp.zeros_like(qd_ref)
        qd_ref[0:hd, 0:tb] = qt_ref[0:hd, :]
        qd_ref[hd:2 * hd, tb:2 * tb] = qt_ref[hd:2 * hd, :]
        m_ref[...] = jnp.full_like(m_ref, NEG_BIG)
        acc_ref[...] = jnp.zeros_like(acc_ref)

    def step(diagonal):
        k = k_ref[...]
        vta = jnp.concatenate([vt_ref[...], jnp.ones((SUM_ROWS, tb), BF16)], axis=0)
        nc = 2 * tb // cq
        cols = [slice(c * cq, (c + 1) * cq) for c in range(nc)]
        nkey = [((c * cq) % tb + cq) if diagonal else tb for c in range(nc)]
        m_prev = m_ref[...]
        s = [_dot(k[0:nkey[c]], qd_ref[:, cols[c]]) for c in range(nc)]
        if diagonal:
            for c in range(nc):
                key = lax.broadcasted_iota(jnp.int32, (nkey[c], cq), 0)
                qpos = lax.broadcasted_iota(jnp.int32, (nkey[c], cq), 1) + (c * cq) % tb
                s[c] = jnp.where(key <= qpos, s[c], NEG_BIG)
        m_new = [jnp.maximum(m_prev[:, cols[c]], jnp.max(s[c], axis=0, keepdims=True)) for c in range(nc)]
        pexp = [jnp.exp2(s[c] - m_new[c]).astype(BF16) for c in range(nc)]
        alpha = [jnp.exp2(m_prev[:, cols[c]] - m_new[c]) for c in range(nc)]
        for c in range(nc):
            cs = cols[c]
            m_ref[:, cs] = m_new[c]
            acc_ref[:, cs] = alpha[c] * acc_ref[:, cs] + _dot(vta[:, 0:nkey[c]], pexp[c])

    @pl.when(ki == qi)
    def _():
        step(True)

    @pl.when(ki != qi)
    def _():
        step(False)

    @pl.when(ki == qi)
    def _():
        lam_p = lam_ref[...]
        lam = (jnp.exp(jnp.sum(lam_p[0:1] * lam_p[1:2], axis=-1, keepdims=True))
               - jnp.exp(jnp.sum(lam_p[2:3] * lam_p[3:4], axis=-1, keepdims=True)) + lambda_init)
        o = acc_ref[0:dv, :] / acc_ref[dv:dv + 1, :]
        o = (o[:, 0:tb] - lam * o[:, tb:2 * tb]).T
        o = o * lax.rsqrt(jnp.mean(o * o, axis=-1, keepdims=True) + RMS_EPS)
        o_ref[...] = (o * g_ref[...] * (1.0 - lambda_init)).astype(BF16)


def _flash_call(qt, kr, vt, lam_params, subln_g, bsz, seq, lambda_init, tb=1024, cq=256):
    t = bsz * seq
    tb = min(tb, seq)
    cq = min(cq, tb)
    nb = seq // tb
    qi_list, ki_list = [], []
    for qi in range(nb):
        for ki in range(qi + 1):
            qi_list.append(qi)
            ki_list.append(ki)
    qi_arr = jnp.asarray(qi_list, jnp.int32)
    ki_arr = jnp.asarray(ki_list, jnp.int32)
    hd = 2 * DIFF_HEAD_DIM
    grid_spec = pltpu.PrefetchScalarGridSpec(
        num_scalar_prefetch=2,
        grid=(bsz, DIFF_HEADS, len(qi_list)),
        in_specs=[pl.BlockSpec((None, hd, tb), lambda b, h, p, qi, ki: (b, h, qi[p])),
                  pl.BlockSpec((tb, hd), lambda b, h, p, qi, ki: (b * nb + ki[p], h)),
                  pl.BlockSpec((None, DIFF_V_DIM, tb), lambda b, h, p, qi, ki: (b, h, ki[p])),
                  pl.BlockSpec((4, DIFF_HEAD_DIM), lambda b, h, p, qi, ki: (0, 0)),
                  pl.BlockSpec((1, DIFF_V_DIM), lambda b, h, p, qi, ki: (0, 0))],
        out_specs=pl.BlockSpec((tb, DIFF_V_DIM), lambda b, h, p, qi, ki: (b * nb + qi[p], h)),
        scratch_shapes=[pltpu.VMEM((hd, 2 * tb), BF16),
                        pltpu.VMEM((1, 2 * tb), F32),
                        pltpu.VMEM((DIFF_V_DIM + SUM_ROWS, 2 * tb), F32)],
    )
    return pl.pallas_call(
        functools.partial(_flash_kernel, tb=tb, cq=cq, lambda_init=lambda_init),
        grid_spec=grid_spec,
        out_shape=jax.ShapeDtypeStruct((t, DIFF_HEADS * DIFF_V_DIM), BF16),
        compiler_params=_cparams(("parallel", "parallel", "arbitrary")),
        name="diff_flash",
    )(qi_arr, ki_arr, qt, kr, vt, lam_params, subln_g.reshape(1, DIFF_V_DIM))


def _merge_kernel(h_ref, y0_ref, y1_ref, y2_ref, y3_ref, wg_ref, bg_ref, wb_ref, wo_ref,
                  g_ref, b_ref, wr_ref, br_ref, o_ref, cls_ref, *, alpha):
    d = h_ref.shape[1]
    h = h_ref[...]
    hb = h.astype(BF16)
    merged = None
    for g, y_ref in enumerate((y0_ref, y1_ref, y2_ref, y3_ref)):
        gate = jax.nn.sigmoid(_dot(hb, wg_ref[g]) + bg_ref[g:g + 1, :])
        term = gate * _dot(y_ref[...], wb_ref[g])
        merged = term if merged is None else merged + term
    t = _dot(merged.astype(BF16), wo_ref[...])
    out = _layer_norm(alpha * h + t, g_ref[...], b_ref[...])
    o_ref[:, 0:d] = out
    cls, ga, gb = _route(out, wr_ref[...], br_ref[...])
    cls_ref[...] = cls
    g8 = jnp.concatenate([ga, gb, jnp.zeros((6, ga.shape[1]), F32)], axis=0)
    sel = (lax.broadcasted_iota(jnp.int32, (8, LANES), 0)
           == lax.broadcasted_iota(jnp.int32, (8, LANES), 1)).astype(F32)
    o_ref[:, d:d + LANES] = lax.dot_general(g8, sel, (((0,), (0,)), ((), ())),
                                            preferred_element_type=F32,
                                            precision=lax.Precision.HIGHEST)


def _merge_call(h, ys, wg, bg, wb, wo, ln_g, ln_b, w_router, b_router, alpha, tm=512):
    t, d = h.shape
    tm = min(tm, t)
    w = BRANCH_WIDTH
    const = dict(pipeline_mode=pl.Buffered(1))
    return pl.pallas_call(
        functools.partial(_merge_kernel, alpha=alpha),
        grid=(t // tm,),
        in_specs=[pl.BlockSpec((tm, d), lambda i: (i, 0))]
        + [pl.BlockSpec((tm, w), lambda i: (i, 0))] * N_BRANCH
        + [pl.BlockSpec((N_BRANCH, d, d), lambda i: (0, 0, 0), **const),
           pl.BlockSpec((N_BRANCH, d), lambda i: (0, 0)),
           pl.BlockSpec((N_BRANCH, w, d), lambda i: (0, 0, 0), **const),
           pl.BlockSpec((d, d), lambda i: (0, 0), **const),
           pl.BlockSpec((1, d), lambda i: (0, 0)),
           pl.BlockSpec((1, d), lambda i: (0, 0)),
           pl.BlockSpec((N_EXPERTS, d), lambda i: (0, 0)),
           pl.BlockSpec((N_EXPERTS, 1), lambda i: (0, 0))],
        out_specs=[pl.BlockSpec((tm, d + LANES), lambda i: (i, 0)),
                   pl.BlockSpec((1, tm), lambda i: (0, i))],
        out_shape=[jax.ShapeDtypeStruct((t, d + LANES), F32),
                   jax.ShapeDtypeStruct((1, t), jnp.int32)],
        compiler_params=_cparams(("parallel",)),
        name="merge",
    )(h, *ys, wg, bg, wb, wo, ln_g.reshape(1, d), ln_b.reshape(1, d),
      w_router.T, b_router.reshape(N_EXPERTS, 1))


N_PAIRS = 6
PAIR_A = (0, 0, 0, 1, 1, 3)
PAIR_B = (1, 2, 3, 3, 2, 2)
N_CLASSES = N_GROUPS * N_PAIRS
CLASS_ROWS = 32


def _route(h, wr, br):
    logits = lax.dot_general(wr, h, (((1,), (1,)), ((), ())),
                             preferred_element_type=F32, precision=lax.Precision.HIGHEST)
    mx = jnp.max(logits, axis=0, keepdims=True)
    e = jnp.exp(logits - mx)
    scores = e / jnp.sum(e, axis=0, keepdims=True)
    sel = scores + br
    epg = EXPERTS_PER_GROUP
    rows = [sel[i:i + 1, :] for i in range(N_EXPERTS)]
    srows = [scores[i:i + 1, :] for i in range(N_EXPERTS)]
    best_score = None
    best = None
    for g in range(N_GROUPS):
        r = rows[g * epg:(g + 1) * epg]
        gs = None
        for a in range(epg):
            for b in range(a + 1, epg):
                pair = r[a] + r[b]
                gs = pair if gs is None else jnp.maximum(gs, pair)
        if best is None:
            best_score, best = gs, jnp.zeros(gs.shape, jnp.int32)
        else:
            take = gs > best_score
            best_score = jnp.where(take, gs, best_score)
            best = jnp.where(take, g, best)
    cand, cscore = [], []
    for j in range(epg):
        c = rows[j]
        s = srows[j]
        for g in range(1, N_GROUPS):
            c = jnp.where(best == g, rows[g * epg + j], c)
            s = jnp.where(best == g, srows[g * epg + j], s)
        cand.append(c)
        cscore.append(s)
    v1, i1, s1 = cand[0], jnp.zeros(best.shape, jnp.int32), cscore[0]
    for j in range(1, epg):
        take = cand[j] > v1
        v1 = jnp.where(take, cand[j], v1)
        i1 = jnp.where(take, j, i1)
        s1 = jnp.where(take, cscore[j], s1)
    v2 = jnp.full(v1.shape, -jnp.inf, F32)
    i2 = jnp.zeros(best.shape, jnp.int32)
    s2 = jnp.zeros(v1.shape, F32)
    for j in range(epg):
        take = jnp.logical_and(i1 != j, cand[j] > v2)
        v2 = jnp.where(take, cand[j], v2)
        i2 = jnp.where(take, j, i2)
        s2 = jnp.where(take, cscore[j], s2)
    denom = s1 + s2
    g1 = s1 / denom
    g2 = s2 / denom
    lo = jnp.minimum(i1, i2)
    hi = jnp.maximum(i1, i2)
    pair = jnp.where(lo == 0, hi - 1, jnp.where(lo == 1, jnp.where(hi == 3, 3, 4), 5))
    a_loc = jnp.where(pair < 3, 0, jnp.where(pair < 5, 1, 3))
    first_is_a = i1 == a_loc
    return (best * N_PAIRS + pair, jnp.where(first_is_a, g1, g2), jnp.where(first_is_a, g2, g1))


def _rank_kernel(cls_ref, dest_ref, cnt_ref, run_ref, start_ref, *, blk):
    phase = pl.program_id(0)
    i = pl.program_id(1)
    tm = cls_ref.shape[1]
    onehot = lax.broadcasted_iota(jnp.int32, (CLASS_ROWS, tm), 0) == cls_ref[...]
    f = jnp.where(onehot, 1.0, 0.0)
    tot = jnp.sum(f, axis=1, keepdims=True)

    @pl.when(jnp.logical_and(phase == 0, i == 0))
    def _():
        run_ref[...] = jnp.zeros_like(run_ref)

    @pl.when(phase == 0)
    def _():
        run_ref[...] = run_ref[...] + tot

    @pl.when(jnp.logical_and(phase == 1, i == 0))
    def _():
        cnt = run_ref[...]
        cnt_ref[...] = cnt.astype(jnp.int32)
        padded = jnp.ceil(cnt * (1.0 / blk)) * blk
        acc = jnp.zeros((1, LANES), F32)
        for c in range(CLASS_ROWS):
            start_ref[c:c + 1, :] = acc
            acc = acc + padded[c:c + 1, :]
        run_ref[...] = jnp.zeros_like(run_ref)

    @pl.when(phase == 1)
    def _():
        s_idx = lax.broadcasted_iota(jnp.int32, (tm, tm), 0)
        t_idx = lax.broadcasted_iota(jnp.int32, (tm, tm), 1)
        tri = jnp.where(s_idx < t_idx, 1.0, 0.0).astype(BF16)
        before = _dot(f.astype(BF16), tri)
        base = start_ref[:, 0:1] + run_ref[:, 0:1]
        dest = jnp.sum(jnp.where(onehot, base + before, 0.0), axis=0, keepdims=True)
        dest_ref[...] = dest.astype(jnp.int32)
        run_ref[...] = run_ref[...] + tot


def _rank_call(cls, blk, tm=512):
    t = cls.shape[1]
    tm = min(tm, t)
    return pl.pallas_call(
        functools.partial(_rank_kernel, blk=blk),
        grid=(2, t // tm),
        in_specs=[pl.BlockSpec((1, tm), lambda p, i: (0, i))],
        out_specs=[pl.BlockSpec((1, tm), lambda p, i: (0, i * p)),
                   pl.BlockSpec((CLASS_ROWS, LANES), lambda p, i: (0, 0))],
        out_shape=[jax.ShapeDtypeStruct((1, t), jnp.int32),
                   jax.ShapeDtypeStruct((CLASS_ROWS, LANES), jnp.int32)],
        scratch_shapes=[pltpu.VMEM((CLASS_ROWS, LANES), F32),
                        pltpu.VMEM((CLASS_ROWS, LANES), F32)],
        compiler_params=_cparams(("arbitrary", "arbitrary")),
        name="moe_rank",
    )(cls)


def _swiglu(xb, wb1, wb3, wb2):
    a = _dot(xb, wb1[...])
    b = _dot(xb, wb3[...])
    return _dot((a * jax.nn.sigmoid(a) * b).astype(BF16), wb2[...])


def _experts_kernel(dest_ref, ba_ref, bb_ref, ve_ref, pe_ref, nb_ref, h_ref,
                    w1a, w3a, w2a, w1b, w3b, w2b, y_ref,
                    inv_ref, xbuf0, xbuf1, obuf0, obuf1, sa1, sa3, sa2, sb1, sb3, sb2, gsem, ssem,
                    *, blk, t, d, n_blk):
    i = pl.program_id(0)
    nb = nb_ref[0]
    xbuf = (xbuf0, xbuf1)
    obuf = (obuf0, obuf1)

    prev = jnp.maximum(i - 1, 0)
    for be_ref, srcs, dsts in ((ba_ref, (w1a, w3a, w2a), (sa1, sa3, sa2)),
                               (bb_ref, (w1b, w3b, w2b), (sb1, sb3, sb2))):
        @pl.when(jnp.logical_and(i < nb, jnp.logical_or(i == 0, be_ref[i] != be_ref[prev])))
        def _(srcs=srcs, dsts=dsts):
            for src, dst in zip(srcs, dsts):
                dst[...] = src[0]

    def gather(block, s):
        base = block * blk
        for j in range(blk):
            tok = inv_ref[base + j] & (t - 1)
            pltpu.make_async_copy(h_ref.at[pl.ds(tok, 1), :], xbuf[s].at[pl.ds(j, 1), :],
                                  gsem.at[s]).start(priority=j % 2)

    def wait_gather(s):
        for j in range(blk):
            pltpu.make_async_copy(h_ref.at[pl.ds(0, 1), :], xbuf[s].at[pl.ds(j, 1), :],
                                  gsem.at[s]).wait()

    def scatter(block, s):
        base = block * blk
        for j in range(blk):
            pltpu.make_async_copy(obuf[s].at[pl.ds(j, 1), :], y_ref.at[pl.ds(inv_ref[base + j], 1), :],
                                  ssem.at[s]).start(priority=j % 2)

    def wait_scatter(s):
        for j in range(blk):
            pltpu.make_async_copy(obuf[s].at[pl.ds(j, 1), :], y_ref.at[pl.ds(0, 1), :],
                                  ssem.at[s]).wait()

    @pl.when(i == 0)
    def _():
        def fill(tok, carry):
            inv_ref[dest_ref[tok]] = tok
            return carry

        lax.fori_loop(0, t, fill, 0, unroll=8)

        def pad(r, carry):
            inv_ref[r] = t + (r & (2 * blk - 1))
            return carry

        for c in range(N_CLASSES):
            lax.fori_loop(ve_ref[c], pe_ref[c], pad, 0)

        def dummy(j, carry):
            inv_ref[n_blk * blk + j] = t + 2 * blk + j
            return carry

        lax.fori_loop(0, blk, dummy, 0)
        obuf1[...] = jnp.zeros_like(obuf1)
        for q in range(3):
            zero_dump = pltpu.make_async_copy(obuf1, y_ref.at[pl.ds(t + q * blk, blk), :], ssem.at[0])
            zero_dump.start()
            zero_dump.wait()
        gather(0, 0)

    def block_step(s):
        wait_gather(s)

        @pl.when(i > 0)
        def _():
            wait_scatter(s)

        gather(jnp.minimum(i + 1, nb - 1), 1 - s)
        scatter(jnp.where(i == 0, n_blk, i - 1), 1 - s)
        x = xbuf[s][...]
        xb = x[:, 0:d].astype(BF16)
        obuf[s][...] = (x[:, d:d + 1] * _swiglu(xb, sa1, sa3, sa2)
                        + x[:, d + 1:d + 2] * _swiglu(xb, sb1, sb3, sb2))

        @pl.when(i == nb - 1)
        def _():
            wait_gather(1 - s)
            wait_scatter(1 - s)
            scatter(i, s)
            wait_scatter(s)

    for s in range(2):
        @pl.when(jnp.logical_and(i < nb, i % 2 == s))
        def _(s=s):
            block_step(s)


def _experts_call(haug, dest, blk_a, blk_b, valid_end, pad_end, n_used, w1, w3, w2, blk):
    t, da = haug.shape
    d = da - LANES
    de = w1.shape[2]
    n_blk = blk_a.shape[0]
    assert t & (t - 1) == 0 and blk & (blk - 1) == 0, "token count and block size must be powers of two"
    amap = lambda i, dest, ba, bb, ve, pe, nb: (ba[i], 0, 0)
    bmap = lambda i, dest, ba, bb, ve, pe, nb: (bb[i], 0, 0)
    wspecs = [pl.BlockSpec((1, d, de), amap), pl.BlockSpec((1, d, de), amap), pl.BlockSpec((1, de, d), amap),
              pl.BlockSpec((1, d, de), bmap), pl.BlockSpec((1, d, de), bmap), pl.BlockSpec((1, de, d), bmap)]
    wscratch = [pltpu.VMEM((d, de), BF16), pltpu.VMEM((d, de), BF16), pltpu.VMEM((de, d), BF16)] * 2
    return pl.pallas_call(
        functools.partial(_experts_kernel, blk=blk, t=t, d=d, n_blk=n_blk),
        grid_spec=pltpu.PrefetchScalarGridSpec(
            num_scalar_prefetch=6,
            grid=(n_blk,),
            in_specs=[pl.BlockSpec(memory_space=pl.ANY)] + wspecs,
            out_specs=pl.BlockSpec(memory_space=pl.ANY),
            scratch_shapes=[pltpu.SMEM(((n_blk + 1) * blk,), jnp.int32),
                            pltpu.VMEM((blk, da), F32),
                            pltpu.VMEM((blk, da), F32),
                            pltpu.VMEM((blk, d), F32),
                            pltpu.VMEM((blk, d), F32)] + wscratch
            + [pltpu.SemaphoreType.DMA((2,)),
               pltpu.SemaphoreType.DMA((2,))],
        ),
        out_shape=jax.ShapeDtypeStruct((t + 3 * blk, d), F32),
        compiler_params=_cparams(("arbitrary",)),
        name="moe_experts",
    )(dest, blk_a, blk_b, valid_end, pad_end, n_used, haug, w1, w3, w2, w1, w3, w2)


def _combine_kernel(h_ref, y_ref, g_ref, b_ref, o_ref, *, alpha):
    o_ref[...] = _layer_norm(alpha * h_ref[...] + y_ref[...], g_ref[...], b_ref[...])


def _combine_call(haug, y, ln_g, ln_b, alpha, tm=512):
    t = haug.shape[0]
    d = y.shape[1]
    tm = min(tm, t)
    return pl.pallas_call(
        functools.partial(_combine_kernel, alpha=alpha),
        grid=(t // tm,),
        in_specs=[pl.BlockSpec((tm, d), lambda i: (i, 0)),
                  pl.BlockSpec((tm, d), lambda i: (i, 0)),
                  pl.BlockSpec((1, d), lambda i: (0, 0)),
                  pl.BlockSpec((1, d), lambda i: (0, 0))],
        out_specs=pl.BlockSpec((tm, d), lambda i: (i, 0)),
        out_shape=jax.ShapeDtypeStruct((t, d), F32),
        compiler_params=_cparams(("parallel",)),
        name="moe_combine",
    )(haug, y, ln_g.reshape(1, d), ln_b.reshape(1, d))


def _moe_layer(haug, cls, w1, w3, w2, ln_g, ln_b, alpha, blk=256):
    t = haug.shape[0]
    dest, counts = _rank_call(cls, blk)
    n_blk = t // blk + N_CLASSES
    cnt = counts[:, 0]
    padded = ((cnt + blk - 1) // blk) * blk
    pad_end = jnp.cumsum(padded)
    blk_start = jnp.arange(n_blk, dtype=jnp.int32) * blk
    blk_cls = jnp.minimum(jnp.sum(pad_end[None, :] <= blk_start[:, None], axis=1), N_CLASSES - 1)
    group = blk_cls // N_PAIRS
    pair = blk_cls % N_PAIRS
    blk_a = (group * EXPERTS_PER_GROUP + jnp.asarray(PAIR_A, jnp.int32)[pair]).astype(jnp.int32)
    blk_b = (group * EXPERTS_PER_GROUP + jnp.asarray(PAIR_B, jnp.int32)[pair]).astype(jnp.int32)
    valid_end = (pad_end - padded + cnt).astype(jnp.int32)
    n_used = (pad_end[N_CLASSES - 1:N_CLASSES] // blk).astype(jnp.int32)
    y = _experts_call(haug, dest.reshape(-1), blk_a, blk_b, valid_end, pad_end.astype(jnp.int32), n_used,
                      w1, w3, w2, blk)
    return _combine_call(haug, y, ln_g, ln_b, alpha)


def kernel(x, positions, ln_in_g, ln_in_b, w_in, conv_w, sgu_ln_g, sgu_ln_b, w_s, b_s, lambda_q1, lambda_k1, lambda_q2, lambda_k2, diff_subln_g, w_gate, b_gate, w_branch, w_o, ln1_g, ln1_b, w_router, b_router, w1, w3, w2, ln2_g, ln2_b):
    bsz, seq, d = x.shape
    depth = w_in.shape[0]
    alpha = (2.0 * depth) ** 0.25
    t = bsz * seq
    cos_t, sin_t = _rope_tables(positions)
    h = _ln_call(x.reshape(t, d), ln_in_g, ln_in_b)
    for l in range(depth):
        lambda_init = 0.8 - 0.6 * math.exp(-0.3 * l)
        z = _proj_call(h, w_in[l].astype(BF16))
        y_conv = _conv_call(z, conv_w[l], bsz, seq)
        y_ret = _ret_call(z, cos_t, sin_t, bsz, seq)
        y_sgu = _sgu_call(z, sgu_ln_g[l], sgu_ln_b[l], w_s[l], b_s[l])
        qt, kr, vt = _qkprep_call(z, cos_t, sin_t, bsz, seq)
        lam_params = jnp.stack([lambda_q1[l], lambda_k1[l], lambda_q2[l], lambda_k2[l]])
        y_diff = _flash_call(qt, kr, vt, lam_params, diff_subln_g[l], bsz, seq, lambda_init)
        haug, cls = _merge_call(h, (y_conv, y_ret, y_sgu, y_diff), w_gate[l].astype(BF16), b_gate[l],
                                w_branch[l].astype(BF16), w_o[l].astype(BF16), ln1_g[l], ln1_b[l],
                                w_router, b_router, alpha)
        h = _moe_layer(haug, cls, w1[l].astype(BF16), w3[l].astype(BF16), w2[l].astype(BF16),
                       ln2_g[l], ln2_b[l], alpha)
    return h.reshape(bsz, seq, d)
```

```python
import functools
import math

import jax
import jax.numpy as jnp
from jax import lax
from jax.experimental import pallas as pl
from jax.experimental.pallas import tpu as pltpu

D_MODEL = 1024
BRANCH_WIDTH = 512
N_BRANCH = 4
CONV_WIDTH = BRANCH_WIDTH
CONV_K = 3
RET_HEADS = 4
RET_DK = 64
RET_DV = 128
RET_CHUNK = 128
RET_THETA = 10000.0
SGU_GROUPS = 4
SGU_GROUP_DIM = BRANCH_WIDTH // SGU_GROUPS
SGU_CHUNK = 128
SGU_WIDTH = BRANCH_WIDTH
DIFF_HEADS = 4
DIFF_HEAD_DIM = 64
DIFF_V_DIM = 2 * DIFF_HEAD_DIM
ROPE_THETA = 500000.0
ROT_DIM = DIFF_HEAD_DIM // 4
CONV_COLS = 3 * CONV_WIDTH
RET_COLS = 2 * RET_HEADS * RET_DK + 2 * RET_HEADS * RET_DV
SGU_COLS = 2 * SGU_WIDTH
DIFF_COLS = 2 * DIFF_HEADS * 2 * DIFF_HEAD_DIM + DIFF_HEADS * DIFF_V_DIM
IN_COLS = CONV_COLS + RET_COLS + SGU_COLS + DIFF_COLS
N_EXPERTS = 16
N_GROUPS = 4
EXPERTS_PER_GROUP = N_EXPERTS // N_GROUPS
TOP_K = 2
D_EXPERT = 1024
LN_EPS = 1e-5
RMS_EPS = 1e-6

LANES = 128
CONV_OFF = 0
RET_OFF = CONV_COLS
SGU_OFF = RET_OFF + RET_COLS
DIFF_OFF = SGU_OFF + SGU_COLS

NEG_BIG = -1e30
VMEM_LIMIT = 56 * 1024 * 1024

BF16 = jnp.bfloat16
F32 = jnp.float32


def _cparams(sem):
    return pltpu.CompilerParams(dimension_semantics=sem, vmem_limit_bytes=VMEM_LIMIT)


def _layer_norm(xf, g, b):
    mu = jnp.mean(xf, axis=-1, keepdims=True)
    xc = xf - mu
    var = jnp.mean(xc * xc, axis=-1, keepdims=True)
    return xc * lax.rsqrt(var + LN_EPS) * g + b


def _dot(a, b):
    return jnp.dot(a, b, preferred_element_type=F32)


def _dot_nt(a, b):
    return lax.dot_general(a, b, (((1,), (1,)), ((), ())), preferred_element_type=F32)


def _dot_tn(a, b):
    return lax.dot_general(a, b, (((0,), (0,)), ((), ())), preferred_element_type=F32)


def _rope_kernel(pos_ref, freq_ref, sign_ref, cos_ref, sin_ref):
    ang = pos_ref[...] * freq_ref[...]
    cos_ref[...] = jnp.cos(ang)
    sin_ref[...] = jnp.sin(ang) * sign_ref[...]


def _rope_tables(positions):
    t = positions.size
    pos = positions.reshape(t, 1).astype(F32)
    half_r = RET_DK // 2
    fr = RET_THETA ** (-jnp.arange(half_r, dtype=F32) / half_r)
    half_d = ROT_DIM // 2
    fd = ROPE_THETA ** (-jnp.arange(half_d, dtype=F32) / half_d)
    zeros_d = jnp.zeros((DIFF_HEAD_DIM - ROT_DIM,), F32)
    freq = jnp.concatenate([fr, fr, fd, fd, zeros_d]).reshape(1, LANES)
    sign = jnp.concatenate([-jnp.ones((half_r,), F32), jnp.ones((half_r,), F32),
                            -jnp.ones((half_d,), F32), jnp.ones((half_d,), F32),
                            zeros_d]).reshape(1, LANES)
    tm = min(t, 2048)
    return pl.pallas_call(
        _rope_kernel,
        grid=(t // tm,),
        in_specs=[pl.BlockSpec((tm, 1), lambda i: (i, 0)),
                  pl.BlockSpec((1, LANES), lambda i: (0, 0)),
                  pl.BlockSpec((1, LANES), lambda i: (0, 0))],
        out_specs=[pl.BlockSpec((tm, LANES), lambda i: (i, 0)),
                   pl.BlockSpec((tm, LANES), lambda i: (i, 0))],
        out_shape=[jax.ShapeDtypeStruct((t, LANES), F32)] * 2,
        compiler_params=_cparams(("parallel",)),
        name="rope_tables",
    )(pos, freq, sign)


def _tile_lanes(x, reps):
    return jnp.concatenate([x] * reps, axis=1)


def _rotate_half_split(x, cos, sin_signed, group, half):
    w = x.shape[1]
    lane = lax.broadcasted_iota(jnp.int32, x.shape, 1) % group
    partner = jnp.where(lane < half, pltpu.roll(x, w - half, axis=1), pltpu.roll(x, half, axis=1))
    return x * cos + partner * sin_signed


PROJ_CHUNK = 512


def _project(hb, w_ref, z_ref):
    for n0 in range(0, IN_COLS, PROJ_CHUNK):
        z_ref[:, n0:n0 + PROJ_CHUNK] = _dot(hb, w_ref[:, n0:n0 + PROJ_CHUNK]).astype(BF16)


def _proj_kernel(h_ref, w_ref, z_ref):
    _project(h_ref[...].astype(BF16), w_ref, z_ref)


def _ln_proj_kernel(x_ref, g_ref, b_ref, w_ref, h_ref, z_ref):
    h = _layer_norm(x_ref[...], g_ref[...], b_ref[...])
    h_ref[...] = h
    _project(h.astype(BF16), w_ref, z_ref)


def _proj_call(h, w_all, layer, ln=None, tm=512):
    t, d = h.shape
    row = pl.BlockSpec((tm, d), lambda i: (i, 0))
    vec = pl.BlockSpec((1, d), lambda i: (0, 0))
    wspec = pl.BlockSpec((None, d, IN_COLS), lambda i: (layer, 0, 0), pipeline_mode=pl.Buffered(1))
    zspec = pl.BlockSpec((tm, IN_COLS), lambda i: (i, 0))
    zshape = jax.ShapeDtypeStruct((t, IN_COLS), BF16)
    if ln is None:
        return pl.pallas_call(
            _proj_kernel, grid=(t // tm,), in_specs=[row, wspec], out_specs=zspec, out_shape=zshape,
            compiler_params=_cparams(("parallel",)), name="proj_in",
        )(h, w_all)
    return pl.pallas_call(
        _ln_proj_kernel, grid=(t // tm,), in_specs=[row, vec, vec, wspec], out_specs=[row, zspec],
        out_shape=[jax.ShapeDtypeStruct((t, d), F32), zshape],
        compiler_params=_cparams(("parallel",)), name="ln_proj_in",
    )(h, ln[0].reshape(1, d), ln[1].reshape(1, d), w_all)


CONV_HALO = 16


def _conv_kernel(b_ref, c_ref, u_ref, ch_ref, uh_ref, w_ref, o_ref):
    ts = c_ref.shape[0]
    first = pl.program_id(1) == 0
    cu = c_ref[...].astype(F32) * u_ref[...].astype(F32)
    halo = ch_ref[...].astype(F32) * uh_ref[...].astype(F32)
    halo = jnp.where(first, 0.0, halo)
    ext = jnp.concatenate([halo, cu], axis=0)
    n = ext.shape[0]
    prev1 = pltpu.roll(ext, 1, axis=0)[CONV_HALO:n]
    prev2 = pltpu.roll(ext, 2, axis=0)[CONV_HALO:n]
    w = w_ref[...]
    y = prev2 * w[0:1, :] + prev1 * w[1:2, :] + cu * w[2:3, :]
    o_ref[...] = (b_ref[...].astype(F32) * y).astype(BF16)


def _conv_call(z, conv_w, bsz, seq, ts=1024):
    t = bsz * seq
    ts = min(ts, seq)
    nt = seq // ts
    wb = CONV_WIDTH
    hb = ts // CONV_HALO
    col = lambda k: (lambda b, i: (b * nt + i, CONV_OFF // wb + k))
    halo = lambda k: (lambda b, i: (jnp.maximum((b * nt + i) * hb - 1, 0), CONV_OFF // wb + k))
    return pl.pallas_call(
        _conv_kernel,
        grid=(bsz, nt),
        in_specs=[pl.BlockSpec((ts, wb), col(0)),
                  pl.BlockSpec((ts, wb), col(1)),
                  pl.BlockSpec((ts, wb), col(2)),
                  pl.BlockSpec((CONV_HALO, wb), halo(1)),
                  pl.BlockSpec((CONV_HALO, wb), halo(2)),
                  pl.BlockSpec((CONV_K, wb), lambda b, i: (0, 0))],
        out_specs=pl.BlockSpec((ts, wb), lambda b, i: (b * nt + i, 0)),
        out_shape=jax.ShapeDtypeStruct((t, wb), BF16),
        compiler_params=_cparams(("parallel", "parallel")),
        name="conv_mixer",
    )(z, z, z, z, z, conv_w.reshape(CONV_K, wb))


def _sgu_kernel(u_ref, v_ref, g_ref, b_ref, ws_ref, bias_ref, o_ref):
    ts = u_ref.shape[0]
    c = SGU_CHUNK
    v = _layer_norm(v_ref[...].astype(F32), g_ref[...], b_ref[...]).astype(BF16)
    row = lax.broadcasted_iota(jnp.int32, (c, c), 0)
    colm = lax.broadcasted_iota(jnp.int32, (c, c), 1)
    bias = bias_ref[...]
    for g in range(SGU_GROUPS):
        w = jnp.where(row >= colm, ws_ref[g], 0.0).astype(BF16)
        lo = g * SGU_GROUP_DIM
        for n in range(ts // c):
            s = _dot(w, v[n * c:(n + 1) * c, lo:lo + SGU_GROUP_DIM]) + bias[:, lo:lo + SGU_GROUP_DIM]
            u = u_ref[n * c:(n + 1) * c, lo:lo + SGU_GROUP_DIM].astype(F32)
            o_ref[n * c:(n + 1) * c, lo:lo + SGU_GROUP_DIM] = (u * s).astype(BF16)


def _sgu_call(z, ln_g, ln_b, w_s, b_s, ts=512):
    t = z.shape[0]
    ts = min(ts, t)
    wb = SGU_WIDTH
    bias = jnp.repeat(b_s.T, SGU_GROUP_DIM, axis=1)
    return pl.pallas_call(
        _sgu_kernel,
        grid=(t // ts,),
        in_specs=[pl.BlockSpec((ts, wb), lambda i: (i, SGU_OFF // wb)),
                  pl.BlockSpec((ts, wb), lambda i: (i, SGU_OFF // wb + 1)),
                  pl.BlockSpec((1, wb), lambda i: (0, 0)),
                  pl.BlockSpec((1, wb), lambda i: (0, 0)),
                  pl.BlockSpec((SGU_GROUPS, SGU_CHUNK, SGU_CHUNK), lambda i: (0, 0, 0)),
                  pl.BlockSpec((SGU_CHUNK, wb), lambda i: (0, 0))],
        out_specs=pl.BlockSpec((ts, wb), lambda i: (i, 0)),
        out_shape=jax.ShapeDtypeStruct((t, wb), BF16),
        compiler_params=_cparams(("parallel",)),
        name="sgu_mixer",
    )(z, z, ln_g.reshape(1, wb), ln_b.reshape(1, wb), w_s, bias)


def _ret_tables():
    c = RET_CHUNK
    log_gamma = jnp.log1p(-jnp.exp2(-5.0 - jnp.arange(RET_HEADS, dtype=F32)))
    idx = jnp.arange(c, dtype=F32)
    rel = idx[:, None] - idx[None, :]
    decay = jnp.where(rel >= 0, jnp.exp(jnp.maximum(rel, 0.0)[None] * log_gamma[:, None, None]), 0.0)
    k_decay = jnp.exp((c - 1 - idx)[:, None] * log_gamma[None, :])
    q_decay = jnp.exp((idx + 1.0)[:, None] * log_gamma[None, :])
    chunk_decay = jnp.exp(c * log_gamma)
    hk = RET_HEADS * RET_DK
    kd = jnp.repeat(k_decay, RET_DK, axis=1) * (RET_DK ** -0.5)
    qd = jnp.repeat(q_decay, RET_DK, axis=1)
    cd = jnp.broadcast_to(jnp.repeat(chunk_decay, RET_DV)[None, :], (8, RET_HEADS * RET_DV))
    del hk
    return decay, kd, qd, cd


def _ret_kernel(q_ref, k_ref, v_ref, g_ref, cos_ref, sin_ref, decay_ref, kd_ref, qd_ref, cd_ref,
                o_ref, state_ref):
    ts = q_ref.shape[0]
    c = RET_CHUNK
    hk = RET_HEADS * RET_DK

    @pl.when(pl.program_id(1) == 0)
    def _():
        state_ref[...] = jnp.zeros_like(state_ref)

    cos = _tile_lanes(cos_ref[:, 0:RET_DK], RET_HEADS)
    sin = _tile_lanes(sin_ref[:, 0:RET_DK], RET_HEADS)
    q = _rotate_half_split(q_ref[...].astype(F32), cos, sin, RET_DK, RET_DK // 2)
    k = _rotate_half_split(k_ref[...].astype(F32), cos, sin, RET_DK, RET_DK // 2)
    del hk
    for n in range(ts // c):
        r0 = n * c
        qn = q[r0:r0 + c]
        kn = k[r0:r0 + c]
        qb = qn.astype(BF16)
        kb = (kn * (RET_DK ** -0.5)).astype(BF16)
        qdb = (qn * qd_ref[...]).astype(BF16)
        kdb = (kn * kd_ref[...]).astype(BF16)
        for h in range(RET_HEADS):
            ks = slice(h * RET_DK, (h + 1) * RET_DK)
            vs = slice(h * RET_DV, (h + 1) * RET_DV)
            vb = v_ref[r0:r0 + c, vs]
            scores = _dot_nt(qb[:, ks], kb[:, ks]) * decay_ref[h]
            inner = _dot(scores.astype(BF16), vb)
            state = state_ref[h]
            cross = _dot(qdb[:, ks], state.astype(BF16))
            kv = _dot_tn(kdb[:, ks], vb)
            state_ref[h] = state * cd_ref[0:1, vs] + kv
            o = inner + cross
            o = o * lax.rsqrt(jnp.mean(o * o, axis=-1, keepdims=True) + RMS_EPS)
            gate = g_ref[r0:r0 + c, vs].astype(F32)
            gate = gate * jax.nn.sigmoid(gate)
            o_ref[r0:r0 + c, vs] = (gate * o).astype(BF16)


def _ret_call(z, cos_t, sin_t, bsz, seq, ts=512):
    t = bsz * seq
    ts = min(ts, seq)
    nt = seq // ts
    decay, kd, qd, cd = _ret_tables()
    hk = RET_HEADS * RET_DK
    hv = RET_HEADS * RET_DV
    row = lambda b, i: b * nt + i
    return pl.pallas_call(
        _ret_kernel,
        grid=(bsz, nt),
        in_specs=[pl.BlockSpec((ts, hk), lambda b, i: (row(b, i), RET_OFF // hk)),
                  pl.BlockSpec((ts, hk), lambda b, i: (row(b, i), RET_OFF // hk + 1)),
                  pl.BlockSpec((ts, hv), lambda b, i: (row(b, i), (RET_OFF + 2 * hk) // hv)),
                  pl.BlockSpec((ts, hv), lambda b, i: (row(b, i), (RET_OFF + 2 * hk) // hv + 1)),
                  pl.BlockSpec((ts, LANES), lambda b, i: (row(b, i), 0)),
                  pl.BlockSpec((ts, LANES), lambda b, i: (row(b, i), 0)),
                  pl.BlockSpec((RET_HEADS, RET_CHUNK, RET_CHUNK), lambda b, i: (0, 0, 0)),
                  pl.BlockSpec((RET_CHUNK, hk), lambda b, i: (0, 0)),
                  pl.BlockSpec((RET_CHUNK, hk), lambda b, i: (0, 0)),
                  pl.BlockSpec((8, hv), lambda b, i: (0, 0))],
        out_specs=pl.BlockSpec((ts, hv), lambda b, i: (row(b, i), 0)),
        out_shape=jax.ShapeDtypeStruct((t, hv), BF16),
        scratch_shapes=[pltpu.VMEM((RET_HEADS, RET_DK, RET_DV), F32)],
        compiler_params=_cparams(("parallel", "arbitrary")),
        name="ret_mixer",
    )(z, z, z, z, cos_t, sin_t, decay, kd, qd, cd)


LOG2E = 1.4426950408889634


def _qkprep_kernel(q_ref, k_ref, v_ref, cos_ref, sin_ref, qt_ref, ko_ref, vt_ref):
    reps = q_ref.shape[1] // DIFF_HEAD_DIM
    cos = _tile_lanes(cos_ref[:, DIFF_HEAD_DIM:2 * DIFF_HEAD_DIM], reps)
    sin = _tile_lanes(sin_ref[:, DIFF_HEAD_DIM:2 * DIFF_HEAD_DIM], reps)
    q = _rotate_half_split(q_ref[...].astype(F32), cos, sin, DIFF_HEAD_DIM, ROT_DIM // 2)
    k = _rotate_half_split(k_ref[...].astype(F32), cos, sin, DIFF_HEAD_DIM, ROT_DIM // 2)
    qt_ref[...] = (q * (DIFF_HEAD_DIM ** -0.5 * LOG2E)).T.astype(BF16)
    ko_ref[...] = k.astype(BF16)
    vt_ref[...] = v_ref[...].astype(F32).T.astype(BF16)


def _qkprep_call(z, cos_t, sin_t, bsz, seq, tm=512):
    t = bsz * seq
    tm = min(tm, seq)
    nt = seq // tm
    hq = DIFF_HEADS * 2 * DIFF_HEAD_DIM
    row = lambda b, i: b * nt + i
    return pl.pallas_call(
        _qkprep_kernel,
        grid=(bsz, nt),
        in_specs=[pl.BlockSpec((tm, hq), lambda b, i: (row(b, i), DIFF_OFF // hq)),
                  pl.BlockSpec((tm, hq), lambda b, i: (row(b, i), DIFF_OFF // hq + 1)),
                  pl.BlockSpec((tm, hq), lambda b, i: (row(b, i), DIFF_OFF // hq + 2)),
                  pl.BlockSpec((tm, LANES), lambda b, i: (row(b, i), 0)),
                  pl.BlockSpec((tm, LANES), lambda b, i: (row(b, i), 0))],
        out_specs=[pl.BlockSpec((None, hq, tm), lambda b, i: (b, 0, i)),
                   pl.BlockSpec((tm, hq), lambda b, i: (row(b, i), 0)),
                   pl.BlockSpec((None, hq, tm), lambda b, i: (b, 0, i))],
        out_shape=[jax.ShapeDtypeStruct((bsz, hq, seq), BF16),
                   jax.ShapeDtypeStruct((t, hq), BF16),
                   jax.ShapeDtypeStruct((bsz, hq, seq), BF16)],
        compiler_params=_cparams(("parallel", "parallel")),
        name="diff_qkprep",
    )(z, z, z, cos_t, sin_t)


SUM_ROWS = 16


def _flash_kernel(qi_ref, ki_ref, qt_ref, k_ref, vt_ref, lam_ref, g_ref, o_ref,
                  qd_ref, m_ref, acc_ref, *, tb, cq, lambda_init):
    p = pl.program_id(2)
    qi = qi_ref[p]
    ki = ki_ref[p]
    hd = DIFF_HEAD_DIM
    dv = DIFF_V_DIM

    @pl.when(ki == 0)
    def _():
        qd_ref[...] = jnp.zeros_like(qd_ref)
        qd_ref[0:hd, 0:tb] = qt_ref[0:hd, :]
        qd_ref[hd:2 * hd, tb:2 * tb] = qt_ref[hd:2 * hd, :]
        m_ref[...] = jnp.full_like(m_ref, NEG_BIG)
        acc_ref[...] = jnp.zeros_like(acc_ref)

    def step(diagonal):
        k = k_ref[...]
        vta = jnp.concatenate([vt_ref[...], jnp.ones((SUM_ROWS, tb), BF16)], axis=0)
        nc = 2 * tb // cq
        cols = [slice(c * cq, (c + 1) * cq) for c in range(nc)]
        nkey = [((c * cq) % tb + cq) if diagonal else tb for c in range(nc)]
        m_prev = m_ref[...]
        s = [_dot(k[0:nkey[c]], qd_ref[:, cols[c]]) for c in range(nc)]
        if diagonal:
            for c in range(nc):
                key = lax.broadcasted_iota(jnp.int32, (nkey[c], cq), 0)
                qpos = lax.broadcasted_iota(jnp.int32, (nkey[c], cq), 1) + (c * cq) % tb
                s[c] = jnp.where(key <= qpos, s[c], NEG_BIG)
        m_new = [jnp.maximum(m_prev[:, cols[c]], jnp.max(s[c], axis=0, keepdims=True)) for c in range(nc)]
        pexp = [jnp.exp2(s[c] - m_new[c]).astype(BF16) for c in range(nc)]
        alpha = [jnp.exp2(m_prev[:, cols[c]] - m_new[c]) for c in range(nc)]
        for c in range(nc):
            cs = cols[c]
            m_ref[:, cs] = m_new[c]
            acc_ref[:, cs] = alpha[c] * acc_ref[:, cs] + _dot(vta[:, 0:nkey[c]], pexp[c])

    @pl.when(ki == qi)
    def _():
        step(True)

    @pl.when(ki != qi)
    def _():
        step(False)

    @pl.when(ki == qi)
    def _():
        lam_p = lam_ref[...]
        lam = (jnp.exp(jnp.sum(lam_p[0:1] * lam_p[1:2], axis=-1, keepdims=True))
               - jnp.exp(jnp.sum(lam_p[2:3] * lam_p[3:4], axis=-1, keepdims=True)) + lambda_init)
        o = acc_ref[0:dv, :] / acc_ref[dv:dv + 1, :]
        o = (o[:, 0:tb] - lam * o[:, tb:2 * tb]).T
        o = o * lax.rsqrt(jnp.mean(o * o, axis=-1, keepdims=True) + RMS_EPS)
        o_ref[...] = (o * g_ref[...] * (1.0 - lambda_init)).astype(BF16)


def _flash_call(qt, kr, vt, lam_params, subln_g, bsz, seq, lambda_init, tb=1024, cq=256):
    t = bsz * seq
    tb = min(tb, seq)
    cq = min(cq, tb)
    nb = seq // tb
    qi_list, ki_list = [], []
    for qi in range(nb):
        for ki in range(qi + 1):
            qi_list.append(qi)
            ki_list.append(ki)
    qi_arr = jnp.asarray(qi_list, jnp.int32)
    ki_arr = jnp.asarray(ki_list, jnp.int32)
    hd = 2 * DIFF_HEAD_DIM
    grid_spec = pltpu.PrefetchScalarGridSpec(
        num_scalar_prefetch=2,
        grid=(bsz, DIFF_HEADS, len(qi_list)),
        in_specs=[pl.BlockSpec((None, hd, tb), lambda b, h, p, qi, ki: (b, h, qi[p])),
                  pl.BlockSpec((tb, hd), lambda b, h, p, qi, ki: (b * nb + ki[p], h)),
                  pl.BlockSpec((None, DIFF_V_DIM, tb), lambda b, h, p, qi, ki: (b, h, ki[p])),
                  pl.BlockSpec((4, DIFF_HEAD_DIM), lambda b, h, p, qi, ki: (0, 0)),
                  pl.BlockSpec((1, DIFF_V_DIM), lambda b, h, p, qi, ki: (0, 0))],
        out_specs=pl.BlockSpec((tb, DIFF_V_DIM), lambda b, h, p, qi, ki: (b * nb + qi[p], h)),
        scratch_shapes=[pltpu.VMEM((hd, 2 * tb), BF16),
                        pltpu.VMEM((1, 2 * tb), F32),
                        pltpu.VMEM((DIFF_V_DIM + SUM_ROWS, 2 * tb), F32)],
    )
    return pl.pallas_call(
        functools.partial(_flash_kernel, tb=tb, cq=cq, lambda_init=lambda_init),
        grid_spec=grid_spec,
        out_shape=jax.ShapeDtypeStruct((t, DIFF_HEADS * DIFF_V_DIM), BF16),
        compiler_params=_cparams(("parallel", "parallel", "arbitrary")),
        name="diff_flash",
    )(qi_arr, ki_arr, qt, kr, vt, lam_params, subln_g.reshape(1, DIFF_V_DIM))


def _merge_kernel(h_ref, y0_ref, y1_ref, y2_ref, y3_ref, wg_ref, bg_ref, wb_ref, wo_ref,
                  g_ref, b_ref, wr_ref, br_ref, o_ref, cls_ref, *, alpha):
    d = h_ref.shape[1]
    h = h_ref[...]
    hb = h.astype(BF16)
    merged = None
    for g, y_ref in enumerate((y0_ref, y1_ref, y2_ref, y3_ref)):
        gate = jax.nn.sigmoid(_dot(hb, wg_ref[g]) + bg_ref[g:g + 1, :])
        term = gate * _dot(y_ref[...], wb_ref[g])
        merged = term if merged is None else merged + term
    t = _dot(merged.astype(BF16), wo_ref[...])
    out = _layer_norm(alpha * h + t, g_ref[...], b_ref[...])
    o_ref[:, 0:d] = out
    cls, ga, gb = _route(out, wr_ref[...], br_ref[...])
    cls_ref[...] = cls
    g8 = jnp.concatenate([ga, gb, jnp.zeros((6, ga.shape[1]), F32)], axis=0)
    sel = (lax.broadcasted_iota(jnp.int32, (8, LANES), 0)
           == lax.broadcasted_iota(jnp.int32, (8, LANES), 1)).astype(F32)
    o_ref[:, d:d + LANES] = lax.dot_general(g8, sel, (((0,), (0,)), ((), ())),
                                            preferred_element_type=F32,
                                            precision=lax.Precision.HIGHEST)


def _merge_call(h, ys, wg, bg, wb, wo, layer, ln_g, ln_b, w_router, b_router, alpha, tm=512):
    t, d = h.shape
    tm = min(tm, t)
    w = BRANCH_WIDTH
    const = dict(pipeline_mode=pl.Buffered(1))
    return pl.pallas_call(
        functools.partial(_merge_kernel, alpha=alpha),
        grid=(t // tm,),
        in_specs=[pl.BlockSpec((tm, d), lambda i: (i, 0))]
        + [pl.BlockSpec((tm, w), lambda i: (i, 0))] * N_BRANCH
        + [pl.BlockSpec((None, N_BRANCH, d, d), lambda i: (layer, 0, 0, 0), **const),
           pl.BlockSpec((N_BRANCH, d), lambda i: (0, 0)),
           pl.BlockSpec((None, N_BRANCH, w, d), lambda i: (layer, 0, 0, 0), **const),
           pl.BlockSpec((None, d, d), lambda i: (layer, 0, 0), **const),
           pl.BlockSpec((1, d), lambda i: (0, 0)),
           pl.BlockSpec((1, d), lambda i: (0, 0)),
           pl.BlockSpec((N_EXPERTS, d), lambda i: (0, 0)),
           pl.BlockSpec((N_EXPERTS, 1), lambda i: (0, 0))],
        out_specs=[pl.BlockSpec((tm, d + LANES), lambda i: (i, 0)),
                   pl.BlockSpec((1, tm), lambda i: (0, i))],
        out_shape=[jax.ShapeDtypeStruct((t, d + LANES), F32),
                   jax.ShapeDtypeStruct((1, t), jnp.int32)],
        compiler_params=_cparams(("parallel",)),
        name="merge",
    )(h, *ys, wg, bg, wb, wo, ln_g.reshape(1, d), ln_b.reshape(1, d),
      w_router.T, b_router.reshape(N_EXPERTS, 1))


N_PAIRS = 6
PAIR_A = (0, 0, 0, 1, 1, 3)
PAIR_B = (1, 2, 3, 3, 2, 2)
N_CLASSES = N_GROUPS * N_PAIRS
CLASS_ROWS = 32


def _route(h, wr, br):
    logits = lax.dot_general(wr, h, (((1,), (1,)), ((), ())),
                             preferred_element_type=F32, precision=lax.Precision.HIGHEST)
    mx = jnp.max(logits, axis=0, keepdims=True)
    e = jnp.exp(logits - mx)
    scores = e / jnp.sum(e, axis=0, keepdims=True)
    sel = scores + br
    epg = EXPERTS_PER_GROUP
    rows = [sel[i:i + 1, :] for i in range(N_EXPERTS)]
    srows = [scores[i:i + 1, :] for i in range(N_EXPERTS)]
    best_score = None
    best = None
    for g in range(N_GROUPS):
        r = rows[g * epg:(g + 1) * epg]
        gs = None
        for a in range(epg):
            for b in range(a + 1, epg):
                pair = r[a] + r[b]
                gs = pair if gs is None else jnp.maximum(gs, pair)
        if best is None:
            best_score, best = gs, jnp.zeros(gs.shape, jnp.int32)
        else:
            take = gs > best_score
            best_score = jnp.where(take, gs, best_score)
            best = jnp.where(take, g, best)
    cand, cscore = [], []
    for j in range(epg):
        c = rows[j]
        s = srows[j]
        for g in range(1, N_GROUPS):
            c = jnp.where(best == g, rows[g * epg + j], c)
            s = jnp.where(best == g, srows[g * epg + j], s)
        cand.append(c)
        cscore.append(s)
    v1, i1, s1 = cand[0], jnp.zeros(best.shape, jnp.int32), cscore[0]
    for j in range(1, epg):
        take = cand[j] > v1
        v1 = jnp.where(take, cand[j], v1)
        i1 = jnp.where(take, j, i1)
        s1 = jnp.where(take, cscore[j], s1)
    v2 = jnp.full(v1.shape, -jnp.inf, F32)
    i2 = jnp.zeros(best.shape, jnp.int32)
    s2 = jnp.zeros(v1.shape, F32)
    for j in range(epg):
        take = jnp.logical_and(i1 != j, cand[j] > v2)
        v2 = jnp.where(take, cand[j], v2)
        i2 = jnp.where(take, j, i2)
        s2 = jnp.where(take, cscore[j], s2)
    denom = s1 + s2
    g1 = s1 / denom
    g2 = s2 / denom
    lo = jnp.minimum(i1, i2)
    hi = jnp.maximum(i1, i2)
    pair = jnp.where(lo == 0, hi - 1, jnp.where(lo == 1, jnp.where(hi == 3, 3, 4), 5))
    a_loc = jnp.where(pair < 3, 0, jnp.where(pair < 5, 1, 3))
    first_is_a = i1 == a_loc
    return (best * N_PAIRS + pair, jnp.where(first_is_a, g1, g2), jnp.where(first_is_a, g2, g1))


def _rank_kernel(cls_ref, dest_ref, cnt_ref, run_ref, start_ref, *, blk):
    phase = pl.program_id(0)
    i = pl.program_id(1)
    tm = cls_ref.shape[1]
    onehot = lax.broadcasted_iota(jnp.int32, (CLASS_ROWS, tm), 0) == cls_ref[...]
    f = jnp.where(onehot, 1.0, 0.0)
    tot = jnp.sum(f, axis=1, keepdims=True)

    @pl.when(jnp.logical_and(phase == 0, i == 0))
    def _():
        run_ref[...] = jnp.zeros_like(run_ref)

    @pl.when(phase == 0)
    def _():
        run_ref[...] = run_ref[...] + tot

    @pl.when(jnp.logical_and(phase == 1, i == 0))
    def _():
        cnt = run_ref[...]
        cnt_ref[...] = cnt.astype(jnp.int32)
        padded = jnp.ceil(cnt * (1.0 / blk)) * blk
        acc = jnp.zeros((1, LANES), F32)
        for c in range(CLASS_ROWS):
            start_ref[c:c + 1, :] = acc
            acc = acc + padded[c:c + 1, :]
        run_ref[...] = jnp.zeros_like(run_ref)

    @pl.when(phase == 1)
    def _():
        s_idx = lax.broadcasted_iota(jnp.int32, (tm, tm), 0)
        t_idx = lax.broadcasted_iota(jnp.int32, (tm, tm), 1)
        tri = jnp.where(s_idx < t_idx, 1.0, 0.0).astype(BF16)
        before = _dot(f.astype(BF16), tri)
        base = start_ref[:, 0:1] + run_ref[:, 0:1]
        dest = jnp.sum(jnp.where(onehot, base + before, 0.0), axis=0, keepdims=True)
        dest_ref[...] = dest.astype(jnp.int32)
        run_ref[...] = run_ref[...] + tot


def _rank_call(cls, blk, tm=512):
    t = cls.shape[1]
    tm = min(tm, t)
    return pl.pallas_call(
        functools.partial(_rank_kernel, blk=blk),
        grid=(2, t // tm),
        in_specs=[pl.BlockSpec((1, tm), lambda p, i: (0, i))],
        out_specs=[pl.BlockSpec((1, tm), lambda p, i: (0, i * p)),
                   pl.BlockSpec((CLASS_ROWS, LANES), lambda p, i: (0, 0))],
        out_shape=[jax.ShapeDtypeStruct((1, t), jnp.int32),
                   jax.ShapeDtypeStruct((CLASS_ROWS, LANES), jnp.int32)],
        scratch_shapes=[pltpu.VMEM((CLASS_ROWS, LANES), F32),
                        pltpu.VMEM((CLASS_ROWS, LANES), F32)],
        compiler_params=_cparams(("arbitrary", "arbitrary")),
        name="moe_rank",
    )(cls)


def _swiglu(xb, wb1, wb3, wb2):
    a = _dot(xb, wb1[...])
    b = _dot(xb, wb3[...])
    return _dot((a * jax.nn.sigmoid(a) * b).astype(BF16), wb2[...])


def _experts_kernel(dest_ref, ba_ref, bb_ref, ve_ref, pe_ref, nb_ref, h_ref,
                    w1a, w3a, w2a, w1b, w3b, w2b, y_ref,
                    inv_ref, xbuf0, xbuf1, obuf0, obuf1, sa1, sa3, sa2, sb1, sb3, sb2, gsem, ssem,
                    *, blk, t, d, n_blk):
    i = pl.program_id(0)
    nb = nb_ref[0]
    xbuf = (xbuf0, xbuf1)
    obuf = (obuf0, obuf1)

    prev = jnp.maximum(i - 1, 0)
    for be_ref, srcs, dsts in ((ba_ref, (w1a, w3a, w2a), (sa1, sa3, sa2)),
                               (bb_ref, (w1b, w3b, w2b), (sb1, sb3, sb2))):
        @pl.when(jnp.logical_and(i < nb, jnp.logical_or(i == 0, be_ref[i] != be_ref[prev])))
        def _(srcs=srcs, dsts=dsts):
            for src, dst in zip(srcs, dsts):
                dst[...] = src[0]

    def gather(block, s):
        base = block * blk
        for j in range(blk):
            tok = inv_ref[base + j] & (t - 1)
            pltpu.make_async_copy(h_ref.at[pl.ds(tok, 1), :], xbuf[s].at[pl.ds(j, 1), :],
                                  gsem.at[s]).start(priority=j % 2)

    def wait_gather(s):
        for j in range(blk):
            pltpu.make_async_copy(h_ref.at[pl.ds(0, 1), :], xbuf[s].at[pl.ds(j, 1), :],
                                  gsem.at[s]).wait()

    def scatter(block, s):
        base = block * blk
        for j in range(blk):
            pltpu.make_async_copy(obuf[s].at[pl.ds(j, 1), :], y_ref.at[pl.ds(inv_ref[base + j], 1), :],
                                  ssem.at[s]).start(priority=j % 2)

    def wait_scatter(s):
        for j in range(blk):
            pltpu.make_async_copy(obuf[s].at[pl.ds(j, 1), :], y_ref.at[pl.ds(0, 1), :],
                                  ssem.at[s]).wait()

    @pl.when(i == 0)
    def _():
        def fill(tok, carry):
            inv_ref[dest_ref[tok]] = tok
            return carry

        lax.fori_loop(0, t, fill, 0, unroll=8)

        def pad(r, carry):
            inv_ref[r] = t + (r & (2 * blk - 1))
            return carry

        for c in range(N_CLASSES):
            lax.fori_loop(ve_ref[c], pe_ref[c], pad, 0)

        def dummy(j, carry):
            inv_ref[n_blk * blk + j] = t + 2 * blk + j
            return carry

        lax.fori_loop(0, blk, dummy, 0)
        obuf1[...] = jnp.zeros_like(obuf1)
        for q in range(3):
            zero_dump = pltpu.make_async_copy(obuf1, y_ref.at[pl.ds(t + q * blk, blk), :], ssem.at[0])
            zero_dump.start()
            zero_dump.wait()
        gather(0, 0)

    def block_step(s):
        wait_gather(s)

        @pl.when(i > 0)
        def _():
            wait_scatter(s)

        gather(jnp.minimum(i + 1, nb - 1), 1 - s)
        scatter(jnp.where(i == 0, n_blk, i - 1), 1 - s)
        x = xbuf[s][...]
        xb = x[:, 0:d].astype(BF16)
        obuf[s][...] = (x[:, d:d + 1] * _swiglu(xb, sa1, sa3, sa2)
                        + x[:, d + 1:d + 2] * _swiglu(xb, sb1, sb3, sb2))

        @pl.when(i == nb - 1)
        def _():
            wait_gather(1 - s)
            wait_scatter(1 - s)
            scatter(i, s)
            wait_scatter(s)

    for s in range(2):
        @pl.when(jnp.logical_and(i < nb, i % 2 == s))
        def _(s=s):
            block_step(s)


def _experts_call(haug, dest, blk_a, blk_b, valid_end, pad_end, n_used, w1, w3, w2, layer, blk):
    t, da = haug.shape
    d = da - LANES
    de = w1.shape[3]
    n_blk = blk_a.shape[0]
    assert t & (t - 1) == 0 and blk & (blk - 1) == 0, "token count and block size must be powers of two"
    amap = lambda i, dest, ba, bb, ve, pe, nb: (layer, ba[i], 0, 0)
    bmap = lambda i, dest, ba, bb, ve, pe, nb: (layer, bb[i], 0, 0)
    up, down = (None, 1, d, de), (None, 1, de, d)
    wspecs = [pl.BlockSpec(up, amap), pl.BlockSpec(up, amap), pl.BlockSpec(down, amap),
              pl.BlockSpec(up, bmap), pl.BlockSpec(up, bmap), pl.BlockSpec(down, bmap)]
    wscratch = [pltpu.VMEM((d, de), BF16), pltpu.VMEM((d, de), BF16), pltpu.VMEM((de, d), BF16)] * 2
    return pl.pallas_call(
        functools.partial(_experts_kernel, blk=blk, t=t, d=d, n_blk=n_blk),
        grid_spec=pltpu.PrefetchScalarGridSpec(
            num_scalar_prefetch=6,
            grid=(n_blk,),
            in_specs=[pl.BlockSpec(memory_space=pl.ANY)] + wspecs,
            out_specs=pl.BlockSpec(memory_space=pl.ANY),
            scratch_shapes=[pltpu.SMEM(((n_blk + 1) * blk,), jnp.int32),
                            pltpu.VMEM((blk, da), F32),
                            pltpu.VMEM((blk, da), F32),
                            pltpu.VMEM((blk, d), F32),
                            pltpu.VMEM((blk, d), F32)] + wscratch
            + [pltpu.SemaphoreType.DMA((2,)),
               pltpu.SemaphoreType.DMA((2,))],
        ),
        out_shape=jax.ShapeDtypeStruct((t + 3 * blk, d), F32),
        compiler_params=_cparams(("arbitrary",)),
        name="moe_experts",
    )(dest, blk_a, blk_b, valid_end, pad_end, n_used, haug, w1, w3, w2, w1, w3, w2)


def _combine_kernel(h_ref, y_ref, g_ref, b_ref, o_ref, *, alpha):
    o_ref[...] = _layer_norm(alpha * h_ref[...] + y_ref[...], g_ref[...], b_ref[...])


def _combine_call(haug, y, ln_g, ln_b, alpha, tm=512):
    t = haug.shape[0]
    d = y.shape[1]
    tm = min(tm, t)
    return pl.pallas_call(
        functools.partial(_combine_kernel, alpha=alpha),
        grid=(t // tm,),
        in_specs=[pl.BlockSpec((tm, d), lambda i: (i, 0)),
                  pl.BlockSpec((tm, d), lambda i: (i, 0)),
                  pl.BlockSpec((1, d), lambda i: (0, 0)),
                  pl.BlockSpec((1, d), lambda i: (0, 0))],
        out_specs=pl.BlockSpec((tm, d), lambda i: (i, 0)),
        out_shape=jax.ShapeDtypeStruct((t, d), F32),
        compiler_params=_cparams(("parallel",)),
        name="moe_combine",
    )(haug, y, ln_g.reshape(1, d), ln_b.reshape(1, d))


def _moe_layer(haug, cls, w1, w3, w2, layer, ln_g, ln_b, alpha, blk=256):
    t = haug.shape[0]
    dest, counts = _rank_call(cls, blk)
    n_blk = t // blk + N_CLASSES
    cnt = counts[:, 0]
    padded = ((cnt + blk - 1) // blk) * blk
    pad_end = jnp.cumsum(padded)
    blk_start = jnp.arange(n_blk, dtype=jnp.int32) * blk
    blk_cls = jnp.minimum(jnp.sum(pad_end[None, :] <= blk_start[:, None], axis=1), N_CLASSES - 1)
    group = blk_cls // N_PAIRS
    pair = blk_cls % N_PAIRS
    blk_a = (group * EXPERTS_PER_GROUP + jnp.asarray(PAIR_A, jnp.int32)[pair]).astype(jnp.int32)
    blk_b = (group * EXPERTS_PER_GROUP + jnp.asarray(PAIR_B, jnp.int32)[pair]).astype(jnp.int32)
    valid_end = (pad_end - padded + cnt).astype(jnp.int32)
    n_used = (pad_end[N_CLASSES - 1:N_CLASSES] // blk).astype(jnp.int32)
    y = _experts_call(haug, dest.reshape(-1), blk_a, blk_b, valid_end, pad_end.astype(jnp.int32), n_used,
                      w1, w3, w2, layer, blk)
    return _combine_call(haug, y, ln_g, ln_b, alpha)


def kernel(x, positions, ln_in_g, ln_in_b, w_in, conv_w, sgu_ln_g, sgu_ln_b, w_s, b_s, lambda_q1, lambda_k1, lambda_q2, lambda_k2, diff_subln_g, w_gate, b_gate, w_branch, w_o, ln1_g, ln1_b, w_router, b_router, w1, w3, w2, ln2_g, ln2_b):
    bsz, seq, d = x.shape
    depth = w_in.shape[0]
    alpha = (2.0 * depth) ** 0.25
    t = bsz * seq
    cos_t, sin_t = _rope_tables(positions)
    w_in_b, w_gate_b, w_branch_b, w_o_b = (w.astype(BF16) for w in (w_in, w_gate, w_branch, w_o))
    w1_b, w3_b, w2_b = (w.astype(BF16) for w in (w1, w3, w2))
    h = x.reshape(t, d)
    for l in range(depth):
        lambda_init = 0.8 - 0.6 * math.exp(-0.3 * l)
        if l == 0:
            h, z = _proj_call(h, w_in_b, l, ln=(ln_in_g, ln_in_b))
        else:
            z = _proj_call(h, w_in_b, l)
        y_conv = _conv_call(z, conv_w[l], bsz, seq)
        y_ret = _ret_call(z, cos_t, sin_t, bsz, seq)
        y_sgu = _sgu_call(z, sgu_ln_g[l], sgu_ln_b[l], w_s[l], b_s[l])
        qt, kr, vt = _qkprep_call(z, cos_t, sin_t, bsz, seq)
        lam_params = jnp.stack([lambda_q1[l], lambda_k1[l], lambda_q2[l], lambda_k2[l]])
        y_diff = _flash_call(qt, kr, vt, lam_params, diff_subln_g[l], bsz, seq, lambda_init)
        haug, cls = _merge_call(h, (y_conv, y_ret, y_sgu, y_diff), w_gate_b, b_gate[l], w_branch_b, w_o_b, l,
                                ln1_g[l], ln1_b[l], w_router, b_router, alpha)
        h = _moe_layer(haug, cls, w1_b, w3_b, w2_b, l, ln2_g[l], ln2_b[l], alpha)
    return h.reshape(bsz, seq, d)
```

```python
import functools
import math

import jax
import jax.numpy as jnp
from jax import lax
from jax.experimental import pallas as pl
from jax.experimental.pallas import tpu as pltpu

D_MODEL = 1024
BRANCH_WIDTH = 512
N_BRANCH = 4
CONV_WIDTH = BRANCH_WIDTH
CONV_K = 3
RET_HEADS = 4
RET_DK = 64
RET_DV = 128
RET_CHUNK = 128
RET_THETA = 10000.0
SGU_GROUPS = 4
SGU_GROUP_DIM = BRANCH_WIDTH // SGU_GROUPS
SGU_CHUNK = 128
SGU_WIDTH = BRANCH_WIDTH
DIFF_HEADS = 4
DIFF_HEAD_DIM = 64
DIFF_V_DIM = 2 * DIFF_HEAD_DIM
ROPE_THETA = 500000.0
ROT_DIM = DIFF_HEAD_DIM // 4
CONV_COLS = 3 * CONV_WIDTH
RET_COLS = 2 * RET_HEADS * RET_DK + 2 * RET_HEADS * RET_DV
SGU_COLS = 2 * SGU_WIDTH
DIFF_COLS = 2 * DIFF_HEADS * 2 * DIFF_HEAD_DIM + DIFF_HEADS * DIFF_V_DIM
IN_COLS = CONV_COLS + RET_COLS + SGU_COLS + DIFF_COLS
N_EXPERTS = 16
N_GROUPS = 4
EXPERTS_PER_GROUP = N_EXPERTS // N_GROUPS
TOP_K = 2
D_EXPERT = 1024
LN_EPS = 1e-5
RMS_EPS = 1e-6

LANES = 128
CONV_OFF = 0
RET_OFF = CONV_COLS
SGU_OFF = RET_OFF + RET_COLS
DIFF_OFF = SGU_OFF + SGU_COLS

NEG_BIG = -1e30
VMEM_LIMIT = 56 * 1024 * 1024

BF16 = jnp.bfloat16
F32 = jnp.float32


def _cparams(sem):
    return pltpu.CompilerParams(dimension_semantics=sem, vmem_limit_bytes=VMEM_LIMIT)


def _layer_norm(xf, g, b):
    mu = jnp.mean(xf, axis=-1, keepdims=True)
    xc = xf - mu
    var = jnp.mean(xc * xc, axis=-1, keepdims=True)
    return xc * lax.rsqrt(var + LN_EPS) * g + b


def _dot(a, b):
    return jnp.dot(a, b, preferred_element_type=F32)


def _dot_nt(a, b):
    return lax.dot_general(a, b, (((1,), (1,)), ((), ())), preferred_element_type=F32)


def _dot_tn(a, b):
    return lax.dot_general(a, b, (((0,), (0,)), ((), ())), preferred_element_type=F32)


def _rope_kernel(pos_ref, freq_ref, sign_ref, cos_ref, sin_ref):
    ang = pos_ref[...] * freq_ref[...]
    cos_ref[...] = jnp.cos(ang)
    sin_ref[...] = jnp.sin(ang) * sign_ref[...]


def _rope_tables(positions):
    t = positions.size
    pos = positions.reshape(t, 1).astype(F32)
    half_r = RET_DK // 2
    fr = RET_THETA ** (-jnp.arange(half_r, dtype=F32) / half_r)
    half_d = ROT_DIM // 2
    fd = ROPE_THETA ** (-jnp.arange(half_d, dtype=F32) / half_d)
    zeros_d = jnp.zeros((DIFF_HEAD_DIM - ROT_DIM,), F32)
    freq = jnp.concatenate([fr, fr, fd, fd, zeros_d]).reshape(1, LANES)
    sign = jnp.concatenate([-jnp.ones((half_r,), F32), jnp.ones((half_r,), F32),
                            -jnp.ones((half_d,), F32), jnp.ones((half_d,), F32),
                            zeros_d]).reshape(1, LANES)
    tm = min(t, 2048)
    return pl.pallas_call(
        _rope_kernel,
        grid=(t // tm,),
        in_specs=[pl.BlockSpec((tm, 1), lambda i: (i, 0)),
                  pl.BlockSpec((1, LANES), lambda i: (0, 0)),
                  pl.BlockSpec((1, LANES), lambda i: (0, 0))],
        out_specs=[pl.BlockSpec((tm, LANES), lambda i: (i, 0)),
                   pl.BlockSpec((tm, LANES), lambda i: (i, 0))],
        out_shape=[jax.ShapeDtypeStruct((t, LANES), F32)] * 2,
        compiler_params=_cparams(("parallel",)),
        name="rope_tables",
    )(pos, freq, sign)


def _tile_lanes(x, reps):
    return jnp.concatenate([x] * reps, axis=1)


def _rotate_half_split(x, cos, sin_signed, group, half):
    w = x.shape[1]
    lane = lax.broadcasted_iota(jnp.int32, x.shape, 1) % group
    partner = jnp.where(lane < half, pltpu.roll(x, w - half, axis=1), pltpu.roll(x, half, axis=1))
    return x * cos + partner * sin_signed


PROJ_CHUNK = 512


def _project(hb, w_ref, z_ref):
    for n0 in range(0, IN_COLS, PROJ_CHUNK):
        z_ref[:, n0:n0 + PROJ_CHUNK] = _dot(hb, w_ref[:, n0:n0 + PROJ_CHUNK]).astype(BF16)


def _proj_kernel(h_ref, w_ref, z_ref):
    _project(h_ref[...].astype(BF16), w_ref, z_ref)


def _ln_proj_kernel(x_ref, g_ref, b_ref, w_ref, h_ref, z_ref):
    h = _layer_norm(x_ref[...], g_ref[...], b_ref[...])
    h_ref[...] = h
    _project(h.astype(BF16), w_ref, z_ref)


def _proj_call(h, w_all, layer, ln=None, tm=512):
    t, d = h.shape
    row = pl.BlockSpec((tm, d), lambda i: (i, 0))
    vec = pl.BlockSpec((1, d), lambda i: (0, 0))
    wspec = pl.BlockSpec((None, d, IN_COLS), lambda i: (layer, 0, 0), pipeline_mode=pl.Buffered(1))
    zspec = pl.BlockSpec((tm, IN_COLS), lambda i: (i, 0))
    zshape = jax.ShapeDtypeStruct((t, IN_COLS), BF16)
    if ln is None:
        return pl.pallas_call(
            _proj_kernel, grid=(t // tm,), in_specs=[row, wspec], out_specs=zspec, out_shape=zshape,
            compiler_params=_cparams(("parallel",)), name="proj_in",
        )(h, w_all)
    return pl.pallas_call(
        _ln_proj_kernel, grid=(t // tm,), in_specs=[row, vec, vec, wspec], out_specs=[row, zspec],
        out_shape=[jax.ShapeDtypeStruct((t, d), F32), zshape],
        compiler_params=_cparams(("parallel",)), name="ln_proj_in",
    )(h, ln[0].reshape(1, d), ln[1].reshape(1, d), w_all)


CONV_HALO = 16


def _conv_kernel(b_ref, c_ref, u_ref, ch_ref, uh_ref, w_ref, o_ref):
    ts = c_ref.shape[0]
    first = pl.program_id(1) == 0
    cu = c_ref[...].astype(F32) * u_ref[...].astype(F32)
    halo = ch_ref[...].astype(F32) * uh_ref[...].astype(F32)
    halo = jnp.where(first, 0.0, halo)
    ext = jnp.concatenate([halo, cu], axis=0)
    n = ext.shape[0]
    prev1 = pltpu.roll(ext, 1, axis=0)[CONV_HALO:n]
    prev2 = pltpu.roll(ext, 2, axis=0)[CONV_HALO:n]
    w = w_ref[...]
    y = prev2 * w[0:1, :] + prev1 * w[1:2, :] + cu * w[2:3, :]
    o_ref[...] = (b_ref[...].astype(F32) * y).astype(BF16)


def _conv_call(z, conv_w, bsz, seq, ts=1024):
    t = bsz * seq
    ts = min(ts, seq)
    nt = seq // ts
    wb = CONV_WIDTH
    hb = ts // CONV_HALO
    col = lambda k: (lambda b, i: (b * nt + i, CONV_OFF // wb + k))
    halo = lambda k: (lambda b, i: (jnp.maximum((b * nt + i) * hb - 1, 0), CONV_OFF // wb + k))
    return pl.pallas_call(
        _conv_kernel,
        grid=(bsz, nt),
        in_specs=[pl.BlockSpec((ts, wb), col(0)),
                  pl.BlockSpec((ts, wb), col(1)),
                  pl.BlockSpec((ts, wb), col(2)),
                  pl.BlockSpec((CONV_HALO, wb), halo(1)),
                  pl.BlockSpec((CONV_HALO, wb), halo(2)),
                  pl.BlockSpec((CONV_K, wb), lambda b, i: (0, 0))],
        out_specs=pl.BlockSpec((ts, wb), lambda b, i: (b * nt + i, 0)),
        out_shape=jax.ShapeDtypeStruct((t, wb), BF16),
        compiler_params=_cparams(("parallel", "parallel")),
        name="conv_mixer",
    )(z, z, z, z, z, conv_w.reshape(CONV_K, wb))


def _sgu_kernel(u_ref, v_ref, g_ref, b_ref, ws_ref, bias_ref, o_ref):
    ts = u_ref.shape[0]
    c = SGU_CHUNK
    v = _layer_norm(v_ref[...].astype(F32), g_ref[...], b_ref[...]).astype(BF16)
    row = lax.broadcasted_iota(jnp.int32, (c, c), 0)
    colm = lax.broadcasted_iota(jnp.int32, (c, c), 1)
    bias = bias_ref[...]
    for g in range(SGU_GROUPS):
        w = jnp.where(row >= colm, ws_ref[g], 0.0).astype(BF16)
        lo = g * SGU_GROUP_DIM
        for n in range(ts // c):
            s = _dot(w, v[n * c:(n + 1) * c, lo:lo + SGU_GROUP_DIM]) + bias[:, lo:lo + SGU_GROUP_DIM]
            u = u_ref[n * c:(n + 1) * c, lo:lo + SGU_GROUP_DIM].astype(F32)
            o_ref[n * c:(n + 1) * c, lo:lo + SGU_GROUP_DIM] = (u * s).astype(BF16)


def _sgu_call(z, ln_g, ln_b, w_s, b_s, ts=512):
    t = z.shape[0]
    ts = min(ts, t)
    wb = SGU_WIDTH
    bias = jnp.repeat(b_s.T, SGU_GROUP_DIM, axis=1)
    return pl.pallas_call(
        _sgu_kernel,
        grid=(t // ts,),
        in_specs=[pl.BlockSpec((ts, wb), lambda i: (i, SGU_OFF // wb)),
                  pl.BlockSpec((ts, wb), lambda i: (i, SGU_OFF // wb + 1)),
                  pl.BlockSpec((1, wb), lambda i: (0, 0)),
                  pl.BlockSpec((1, wb), lambda i: (0, 0)),
                  pl.BlockSpec((SGU_GROUPS, SGU_CHUNK, SGU_CHUNK), lambda i: (0, 0, 0)),
                  pl.BlockSpec((SGU_CHUNK, wb), lambda i: (0, 0))],
        out_specs=pl.BlockSpec((ts, wb), lambda i: (i, 0)),
        out_shape=jax.ShapeDtypeStruct((t, wb), BF16),
        compiler_params=_cparams(("parallel",)),
        name="sgu_mixer",
    )(z, z, ln_g.reshape(1, wb), ln_b.reshape(1, wb), w_s, bias)


def _ret_tables():
    c = RET_CHUNK
    log_gamma = jnp.log1p(-jnp.exp2(-5.0 - jnp.arange(RET_HEADS, dtype=F32)))
    idx = jnp.arange(c, dtype=F32)
    rel = idx[:, None] - idx[None, :]
    decay = jnp.where(rel >= 0, jnp.exp(jnp.maximum(rel, 0.0)[None] * log_gamma[:, None, None]), 0.0)
    k_decay = jnp.exp((c - 1 - idx)[:, None] * log_gamma[None, :])
    q_decay = jnp.exp((idx + 1.0)[:, None] * log_gamma[None, :])
    chunk_decay = jnp.exp(c * log_gamma)
    hk = RET_HEADS * RET_DK
    kd = jnp.repeat(k_decay, RET_DK, axis=1) * (RET_DK ** -0.5)
    qd = jnp.repeat(q_decay, RET_DK, axis=1)
    cd = jnp.broadcast_to(jnp.repeat(chunk_decay, RET_DV)[None, :], (8, RET_HEADS * RET_DV))
    del hk
    return decay, kd, qd, cd


def _ret_kernel(q_ref, k_ref, v_ref, g_ref, cos_ref, sin_ref, decay_ref, kd_ref, qd_ref, cd_ref,
                o_ref, state_ref):
    ts = q_ref.shape[0]
    c = RET_CHUNK
    hk = RET_HEADS * RET_DK

    @pl.when(pl.program_id(1) == 0)
    def _():
        state_ref[...] = jnp.zeros_like(state_ref)

    cos = _tile_lanes(cos_ref[:, 0:RET_DK], RET_HEADS)
    sin = _tile_lanes(sin_ref[:, 0:RET_DK], RET_HEADS)
    q = _rotate_half_split(q_ref[...].astype(F32), cos, sin, RET_DK, RET_DK // 2)
    k = _rotate_half_split(k_ref[...].astype(F32), cos, sin, RET_DK, RET_DK // 2)
    del hk
    for n in range(ts // c):
        r0 = n * c
        qn = q[r0:r0 + c]
        kn = k[r0:r0 + c]
        qb = qn.astype(BF16)
        kb = (kn * (RET_DK ** -0.5)).astype(BF16)
        qdb = (qn * qd_ref[...]).astype(BF16)
        kdb = (kn * kd_ref[...]).astype(BF16)
        for h in range(RET_HEADS):
            ks = slice(h * RET_DK, (h + 1) * RET_DK)
            vs = slice(h * RET_DV, (h + 1) * RET_DV)
            vb = v_ref[r0:r0 + c, vs]
            scores = _dot_nt(qb[:, ks], kb[:, ks]) * decay_ref[h]
            inner = _dot(scores.astype(BF16), vb)
            state = state_ref[h]
            cross = _dot(qdb[:, ks], state.astype(BF16))
            kv = _dot_tn(kdb[:, ks], vb)
            state_ref[h] = state * cd_ref[0:1, vs] + kv
            o = inner + cross
            o = o * lax.rsqrt(jnp.mean(o * o, axis=-1, keepdims=True) + RMS_EPS)
            gate = g_ref[r0:r0 + c, vs].astype(F32)
            gate = gate * jax.nn.sigmoid(gate)
            o_ref[r0:r0 + c, vs] = (gate * o).astype(BF16)


def _ret_call(z, cos_t, sin_t, bsz, seq, ts=512):
    t = bsz * seq
    ts = min(ts, seq)
    nt = seq // ts
    decay, kd, qd, cd = _ret_tables()
    hk = RET_HEADS * RET_DK
    hv = RET_HEADS * RET_DV
    row = lambda b, i: b * nt + i
    return pl.pallas_call(
        _ret_kernel,
        grid=(bsz, nt),
        in_specs=[pl.BlockSpec((ts, hk), lambda b, i: (row(b, i), RET_OFF // hk)),
                  pl.BlockSpec((ts, hk), lambda b, i: (row(b, i), RET_OFF // hk + 1)),
                  pl.BlockSpec((ts, hv), lambda b, i: (row(b, i), (RET_OFF + 2 * hk) // hv)),
                  pl.BlockSpec((ts, hv), lambda b, i: (row(b, i), (RET_OFF + 2 * hk) // hv + 1)),
                  pl.BlockSpec((ts, LANES), lambda b, i: (row(b, i), 0)),
                  pl.BlockSpec((ts, LANES), lambda b, i: (row(b, i), 0)),
                  pl.BlockSpec((RET_HEADS, RET_CHUNK, RET_CHUNK), lambda b, i: (0, 0, 0)),
                  pl.BlockSpec((RET_CHUNK, hk), lambda b, i: (0, 0)),
                  pl.BlockSpec((RET_CHUNK, hk), lambda b, i: (0, 0)),
                  pl.BlockSpec((8, hv), lambda b, i: (0, 0))],
        out_specs=pl.BlockSpec((ts, hv), lambda b, i: (row(b, i), 0)),
        out_shape=jax.ShapeDtypeStruct((t, hv), BF16),
        scratch_shapes=[pltpu.VMEM((RET_HEADS, RET_DK, RET_DV), F32)],
        compiler_params=_cparams(("parallel", "arbitrary")),
        name="ret_mixer",
    )(z, z, z, z, cos_t, sin_t, decay, kd, qd, cd)


LOG2E = 1.4426950408889634


def _qkprep_kernel(q_ref, k_ref, v_ref, cos_ref, sin_ref, qt_ref, ko_ref, vt_ref):
    reps = q_ref.shape[1] // DIFF_HEAD_DIM
    cos = _tile_lanes(cos_ref[:, DIFF_HEAD_DIM:2 * DIFF_HEAD_DIM], reps)
    sin = _tile_lanes(sin_ref[:, DIFF_HEAD_DIM:2 * DIFF_HEAD_DIM], reps)
    q = _rotate_half_split(q_ref[...].astype(F32), cos, sin, DIFF_HEAD_DIM, ROT_DIM // 2)
    k = _rotate_half_split(k_ref[...].astype(F32), cos, sin, DIFF_HEAD_DIM, ROT_DIM // 2)
    qt_ref[...] = (q * (DIFF_HEAD_DIM ** -0.5 * LOG2E)).T.astype(BF16)
    ko_ref[...] = k.astype(BF16)
    vt_ref[...] = v_ref[...].astype(F32).T.astype(BF16)


def _qkprep_call(z, cos_t, sin_t, bsz, seq, tm=512):
    t = bsz * seq
    tm = min(tm, seq)
    nt = seq // tm
    hq = DIFF_HEADS * 2 * DIFF_HEAD_DIM
    row = lambda b, i: b * nt + i
    return pl.pallas_call(
        _qkprep_kernel,
        grid=(bsz, nt),
        in_specs=[pl.BlockSpec((tm, hq), lambda b, i: (row(b, i), DIFF_OFF // hq)),
                  pl.BlockSpec((tm, hq), lambda b, i: (row(b, i), DIFF_OFF // hq + 1)),
                  pl.BlockSpec((tm, hq), lambda b, i: (row(b, i), DIFF_OFF // hq + 2)),
                  pl.BlockSpec((tm, LANES), lambda b, i: (row(b, i), 0)),
                  pl.BlockSpec((tm, LANES), lambda b, i: (row(b, i), 0))],
        out_specs=[pl.BlockSpec((None, hq, tm), lambda b, i: (b, 0, i)),
                   pl.BlockSpec((tm, hq), lambda b, i: (row(b, i), 0)),
                   pl.BlockSpec((None, hq, tm), lambda b, i: (b, 0, i))],
        out_shape=[jax.ShapeDtypeStruct((bsz, hq, seq), BF16),
                   jax.ShapeDtypeStruct((t, hq), BF16),
                   jax.ShapeDtypeStruct((bsz, hq, seq), BF16)],
        compiler_params=_cparams(("parallel", "parallel")),
        name="diff_qkprep",
    )(z, z, z, cos_t, sin_t)


SUM_ROWS = 16


def _flash_kernel(qi_ref, ki_ref, qt_ref, k_ref, vt_ref, lam_ref, g_ref, o_ref,
                  qd_ref, m_ref, acc_ref, *, tb, cq, lambda_init):
    p = pl.program_id(2)
    qi = qi_ref[p]
    kp = ki_ref[p]
    hd = DIFF_HEAD_DIM
    dv = DIFF_V_DIM

    @pl.when(kp == 0)
    def _():
        qd_ref[...] = jnp.zeros_like(qd_ref)
        qd_ref[0:hd, 0:tb] = qt_ref[0:hd, :]
        qd_ref[hd:2 * hd, tb:2 * tb] = qt_ref[hd:2 * hd, :]
        m_ref[...] = jnp.full_like(m_ref, NEG_BIG)
        acc_ref[...] = jnp.zeros_like(acc_ref)

    def step(diagonal, half):
        k = k_ref[half * tb:(half + 1) * tb, :]
        vta = jnp.concatenate([vt_ref[:, half * tb:(half + 1) * tb], jnp.ones((SUM_ROWS, tb), BF16)],
                              axis=0)
        nc = 2 * tb // cq
        cols = [slice(c * cq, (c + 1) * cq) for c in range(nc)]
        nkey = [((c * cq) % tb + cq) if diagonal else tb for c in range(nc)]
        m_prev = m_ref[...]
        s = [_dot(k[0:nkey[c]], qd_ref[:, cols[c]]) for c in range(nc)]
        if diagonal:
            for c in range(nc):
                key = lax.broadcasted_iota(jnp.int32, (nkey[c], cq), 0)
                qpos = lax.broadcasted_iota(jnp.int32, (nkey[c], cq), 1) + (c * cq) % tb
                s[c] = jnp.where(key <= qpos, s[c], NEG_BIG)
        m_new = [jnp.maximum(m_prev[:, cols[c]], jnp.max(s[c], axis=0, keepdims=True)) for c in range(nc)]
        pexp = [jnp.exp2(s[c] - m_new[c]).astype(BF16) for c in range(nc)]
        alpha = [jnp.exp2(m_prev[:, cols[c]] - m_new[c]) for c in range(nc)]
        for c in range(nc):
            cs = cols[c]
            m_ref[:, cs] = m_new[c]
            acc_ref[:, cs] = alpha[c] * acc_ref[:, cs] + _dot(vta[:, 0:nkey[c]], pexp[c])

    def finalize():
        lam_p = lam_ref[...]
        lam = (jnp.exp(jnp.sum(lam_p[0:1] * lam_p[1:2], axis=-1, keepdims=True))
               - jnp.exp(jnp.sum(lam_p[2:3] * lam_p[3:4], axis=-1, keepdims=True)) + lambda_init)
        o = acc_ref[0:dv, :] / acc_ref[dv:dv + 1, :]
        o = (o[:, 0:tb] - lam * o[:, tb:2 * tb]).T
        o = o * lax.rsqrt(jnp.mean(o * o, axis=-1, keepdims=True) + RMS_EPS)
        o_ref[...] = (o * g_ref[...] * (1.0 - lambda_init)).astype(BF16)

    for half in range(2):
        kb = 2 * kp + half

        @pl.when(kb < qi)
        def _(half=half):
            step(False, half)

        @pl.when(kb == qi)
        def _(half=half):
            step(True, half)
            finalize()


def _flash_call(qt, kr, vt, lam_params, subln_g, bsz, seq, lambda_init, tb=1024, cq=256):
    t = bsz * seq
    tb = min(tb, seq // 2)
    cq = min(cq, tb)
    nb = seq // tb
    assert nb % 2 == 0, "key blocks are processed in pairs"
    qi_list, ki_list = [], []
    for qi in range(nb):
        for kp in range(qi // 2 + 1):
            qi_list.append(qi)
            ki_list.append(kp)
    qi_arr = jnp.asarray(qi_list, jnp.int32)
    ki_arr = jnp.asarray(ki_list, jnp.int32)
    hd = 2 * DIFF_HEAD_DIM
    grid_spec = pltpu.PrefetchScalarGridSpec(
        num_scalar_prefetch=2,
        grid=(bsz, DIFF_HEADS, len(qi_list)),
        in_specs=[pl.BlockSpec((None, hd, tb), lambda b, h, p, qi, ki: (b, h, qi[p])),
                  pl.BlockSpec((2 * tb, hd), lambda b, h, p, qi, ki: (b * (nb // 2) + ki[p], h)),
                  pl.BlockSpec((None, DIFF_V_DIM, 2 * tb), lambda b, h, p, qi, ki: (b, h, ki[p])),
                  pl.BlockSpec((4, DIFF_HEAD_DIM), lambda b, h, p, qi, ki: (0, 0)),
                  pl.BlockSpec((1, DIFF_V_DIM), lambda b, h, p, qi, ki: (0, 0))],
        out_specs=pl.BlockSpec((tb, DIFF_V_DIM), lambda b, h, p, qi, ki: (b * nb + qi[p], h)),
        scratch_shapes=[pltpu.VMEM((hd, 2 * tb), BF16),
                        pltpu.VMEM((1, 2 * tb), F32),
                        pltpu.VMEM((DIFF_V_DIM + SUM_ROWS, 2 * tb), F32)],
    )
    return pl.pallas_call(
        functools.partial(_flash_kernel, tb=tb, cq=cq, lambda_init=lambda_init),
        grid_spec=grid_spec,
        out_shape=jax.ShapeDtypeStruct((t, DIFF_HEADS * DIFF_V_DIM), BF16),
        compiler_params=_cparams(("parallel", "parallel", "arbitrary")),
        name="diff_flash",
    )(qi_arr, ki_arr, qt, kr, vt, lam_params, subln_g.reshape(1, DIFF_V_DIM))


def _merge_kernel(h_ref, y0_ref, y1_ref, y2_ref, y3_ref, wg_ref, bg_ref, wb_ref, wo_ref,
                  g_ref, b_ref, wr_ref, br_ref, o_ref, cls_ref, *, alpha):
    d = h_ref.shape[1]
    h = h_ref[...]
    hb = h.astype(BF16)
    merged = None
    for g, y_ref in enumerate((y0_ref, y1_ref, y2_ref, y3_ref)):
        gate = jax.nn.sigmoid(_dot(hb, wg_ref[g]) + bg_ref[g:g + 1, :])
        term = gate * _dot(y_ref[...], wb_ref[g])
        merged = term if merged is None else merged + term
    t = _dot(merged.astype(BF16), wo_ref[...])
    out = _layer_norm(alpha * h + t, g_ref[...], b_ref[...])
    o_ref[:, 0:d] = out
    cls, ga, gb = _route(out, wr_ref[...], br_ref[...])
    cls_ref[...] = cls
    g8 = jnp.concatenate([ga, gb, jnp.zeros((6, ga.shape[1]), F32)], axis=0)
    sel = (lax.broadcasted_iota(jnp.int32, (8, LANES), 0)
           == lax.broadcasted_iota(jnp.int32, (8, LANES), 1)).astype(F32)
    o_ref[:, d:d + LANES] = lax.dot_general(g8, sel, (((0,), (0,)), ((), ())),
                                            preferred_element_type=F32,
                                            precision=lax.Precision.HIGHEST)


def _merge_call(h, ys, wg, bg, wb, wo, layer, ln_g, ln_b, w_router, b_router, alpha, tm=512):
    t, d = h.shape
    tm = min(tm, t)
    w = BRANCH_WIDTH
    const = dict(pipeline_mode=pl.Buffered(1))
    return pl.pallas_call(
        functools.partial(_merge_kernel, alpha=alpha),
        grid=(t // tm,),
        in_specs=[pl.BlockSpec((tm, d), lambda i: (i, 0))]
        + [pl.BlockSpec((tm, w), lambda i: (i, 0))] * N_BRANCH
        + [pl.BlockSpec((None, N_BRANCH, d, d), lambda i: (layer, 0, 0, 0), **const),
           pl.BlockSpec((N_BRANCH, d), lambda i: (0, 0)),
           pl.BlockSpec((None, N_BRANCH, w, d), lambda i: (layer, 0, 0, 0), **const),
           pl.BlockSpec((None, d, d), lambda i: (layer, 0, 0), **const),
           pl.BlockSpec((1, d), lambda i: (0, 0)),
           pl.BlockSpec((1, d), lambda i: (0, 0)),
           pl.BlockSpec((N_EXPERTS, d), lambda i: (0, 0)),
           pl.BlockSpec((N_EXPERTS, 1), lambda i: (0, 0))],
        out_specs=[pl.BlockSpec((tm, d + LANES), lambda i: (i, 0)),
                   pl.BlockSpec((1, tm), lambda i: (0, i))],
        out_shape=[jax.ShapeDtypeStruct((t, d + LANES), F32),
                   jax.ShapeDtypeStruct((1, t), jnp.int32)],
        compiler_params=_cparams(("parallel",)),
        name="merge",
    )(h, *ys, wg, bg, wb, wo, ln_g.reshape(1, d), ln_b.reshape(1, d),
      w_router.T, b_router.reshape(N_EXPERTS, 1))


N_PAIRS = 6
PAIR_A = (0, 0, 0, 1, 1, 3)
PAIR_B = (1, 2, 3, 3, 2, 2)
N_CLASSES = N_GROUPS * N_PAIRS
CLASS_ROWS = 32


def _route(h, wr, br):
    logits = lax.dot_general(wr, h, (((1,), (1,)), ((), ())),
                             preferred_element_type=F32, precision=lax.Precision.HIGHEST)
    mx = jnp.max(logits, axis=0, keepdims=True)
    e = jnp.exp(logits - mx)
    scores = e / jnp.sum(e, axis=0, keepdims=True)
    sel = scores + br
    epg = EXPERTS_PER_GROUP
    rows = [sel[i:i + 1, :] for i in range(N_EXPERTS)]
    srows = [scores[i:i + 1, :] for i in range(N_EXPERTS)]
    best_score = None
    best = None
    for g in range(N_GROUPS):
        r = rows[g * epg:(g + 1) * epg]
        gs = None
        for a in range(epg):
            for b in range(a + 1, epg):
                pair = r[a] + r[b]
                gs = pair if gs is None else jnp.maximum(gs, pair)
        if best is None:
            best_score, best = gs, jnp.zeros(gs.shape, jnp.int32)
        else:
            take = gs > best_score
            best_score = jnp.where(take, gs, best_score)
            best = jnp.where(take, g, best)
    cand, cscore = [], []
    for j in range(epg):
        c = rows[j]
        s = srows[j]
        for g in range(1, N_GROUPS):
            c = jnp.where(best == g, rows[g * epg + j], c)
            s = jnp.where(best == g, srows[g * epg + j], s)
        cand.append(c)
        cscore.append(s)
    v1, i1, s1 = cand[0], jnp.zeros(best.shape, jnp.int32), cscore[0]
    for j in range(1, epg):
        take = cand[j] > v1
        v1 = jnp.where(take, cand[j], v1)
        i1 = jnp.where(take, j, i1)
        s1 = jnp.where(take, cscore[j], s1)
    v2 = jnp.full(v1.shape, -jnp.inf, F32)
    i2 = jnp.zeros(best.shape, jnp.int32)
    s2 = jnp.zeros(v1.shape, F32)
    for j in range(epg):
        take = jnp.logical_and(i1 != j, cand[j] > v2)
        v2 = jnp.where(take, cand[j], v2)
        i2 = jnp.where(take, j, i2)
        s2 = jnp.where(take, cscore[j], s2)
    denom = s1 + s2
    g1 = s1 / denom
    g2 = s2 / denom
    lo = jnp.minimum(i1, i2)
    hi = jnp.maximum(i1, i2)
    pair = jnp.where(lo == 0, hi - 1, jnp.where(lo == 1, jnp.where(hi == 3, 3, 4), 5))
    a_loc = jnp.where(pair < 3, 0, jnp.where(pair < 5, 1, 3))
    first_is_a = i1 == a_loc
    return (best * N_PAIRS + pair, jnp.where(first_is_a, g1, g2), jnp.where(first_is_a, g2, g1))


def _rank_kernel(cls_ref, dest_ref, cnt_ref, run_ref, start_ref, *, blk):
    phase = pl.program_id(0)
    i = pl.program_id(1)
    tm = cls_ref.shape[1]
    onehot = lax.broadcasted_iota(jnp.int32, (CLASS_ROWS, tm), 0) == cls_ref[...]
    f = jnp.where(onehot, 1.0, 0.0)
    tot = jnp.sum(f, axis=1, keepdims=True)

    @pl.when(jnp.logical_and(phase == 0, i == 0))
    def _():
        run_ref[...] = jnp.zeros_like(run_ref)

    @pl.when(phase == 0)
    def _():
        run_ref[...] = run_ref[...] + tot

    @pl.when(jnp.logical_and(phase == 1, i == 0))
    def _():
        cnt = run_ref[...]
        cnt_ref[...] = cnt.astype(jnp.int32)
        padded = jnp.ceil(cnt * (1.0 / blk)) * blk
        acc = jnp.zeros((1, LANES), F32)
        for c in range(CLASS_ROWS):
            start_ref[c:c + 1, :] = acc
            acc = acc + padded[c:c + 1, :]
        run_ref[...] = jnp.zeros_like(run_ref)

    @pl.when(phase == 1)
    def _():
        s_idx = lax.broadcasted_iota(jnp.int32, (tm, tm), 0)
        t_idx = lax.broadcasted_iota(jnp.int32, (tm, tm), 1)
        tri = jnp.where(s_idx < t_idx, 1.0, 0.0).astype(BF16)
        before = _dot(f.astype(BF16), tri)
        base = start_ref[:, 0:1] + run_ref[:, 0:1]
        dest = jnp.sum(jnp.where(onehot, base + before, 0.0), axis=0, keepdims=True)
        dest_ref[...] = dest.astype(jnp.int32)
        run_ref[...] = run_ref[...] + tot


def _rank_call(cls, blk, tm=1024):
    t = cls.shape[1]
    tm = min(tm, t)
    return pl.pallas_call(
        functools.partial(_rank_kernel, blk=blk),
        grid=(2, t // tm),
        in_specs=[pl.BlockSpec((1, tm), lambda p, i: (0, i))],
        out_specs=[pl.BlockSpec((1, tm), lambda p, i: (0, i * p)),
                   pl.BlockSpec((CLASS_ROWS, LANES), lambda p, i: (0, 0))],
        out_shape=[jax.ShapeDtypeStruct((1, t), jnp.int32),
                   jax.ShapeDtypeStruct((CLASS_ROWS, LANES), jnp.int32)],
        scratch_shapes=[pltpu.VMEM((CLASS_ROWS, LANES), F32),
                        pltpu.VMEM((CLASS_ROWS, LANES), F32)],
        compiler_params=_cparams(("arbitrary", "arbitrary")),
        name="moe_rank",
    )(cls)


def _swiglu(xb, wb1, wb3, wb2):
    a = _dot(xb, wb1[...])
    b = _dot(xb, wb3[...])
    return _dot((a * jax.nn.sigmoid(a) * b).astype(BF16), wb2[...])


def _experts_kernel(dest_ref, ba_ref, bb_ref, ve_ref, pe_ref, nb_ref, h_ref,
                    w1a, w3a, w2a, w1b, w3b, w2b, y_ref,
                    inv_ref, xbuf0, xbuf1, obuf0, obuf1, sa1, sa3, sa2, sb1, sb3, sb2, gsem, ssem,
                    *, blk, t, d, n_blk):
    i = pl.program_id(0)
    nb = nb_ref[0]
    xbuf = (xbuf0, xbuf1)
    obuf = (obuf0, obuf1)

    prev = jnp.maximum(i - 1, 0)
    for be_ref, srcs, dsts in ((ba_ref, (w1a, w3a, w2a), (sa1, sa3, sa2)),
                               (bb_ref, (w1b, w3b, w2b), (sb1, sb3, sb2))):
        @pl.when(jnp.logical_and(i < nb, jnp.logical_or(i == 0, be_ref[i] != be_ref[prev])))
        def _(srcs=srcs, dsts=dsts):
            for src, dst in zip(srcs, dsts):
                dst[...] = src[0]

    def gather(block, s):
        base = block * blk
        for j in range(blk):
            tok = inv_ref[base + j] & (t - 1)
            pltpu.make_async_copy(h_ref.at[pl.ds(tok, 1), :], xbuf[s].at[pl.ds(j, 1), :],
                                  gsem.at[s]).start(priority=j % 2)

    def wait_gather(s):
        for j in range(blk):
            pltpu.make_async_copy(h_ref.at[pl.ds(0, 1), :], xbuf[s].at[pl.ds(j, 1), :],
                                  gsem.at[s]).wait()

    def scatter(block, s):
        base = block * blk
        for j in range(blk):
            pltpu.make_async_copy(obuf[s].at[pl.ds(j, 1), :], y_ref.at[pl.ds(inv_ref[base + j], 1), :],
                                  ssem.at[s]).start(priority=j % 2)

    def wait_scatter(s):
        for j in range(blk):
            pltpu.make_async_copy(obuf[s].at[pl.ds(j, 1), :], y_ref.at[pl.ds(0, 1), :],
                                  ssem.at[s]).wait()

    @pl.when(i == 0)
    def _():
        def fill(tok, carry):
            inv_ref[dest_ref[tok]] = tok
            return carry

        lax.fori_loop(0, t, fill, 0, unroll=8)

        def pad(r, carry):
            inv_ref[r] = t + (r & (2 * blk - 1))
            return carry

        for c in range(N_CLASSES):
            lax.fori_loop(ve_ref[c], pe_ref[c], pad, 0)

        def dummy(j, carry):
            inv_ref[n_blk * blk + j] = t + 2 * blk + j
            return carry

        lax.fori_loop(0, blk, dummy, 0)
        obuf1[...] = jnp.zeros_like(obuf1)
        for q in range(3):
            zero_dump = pltpu.make_async_copy(obuf1, y_ref.at[pl.ds(t + q * blk, blk), :], ssem.at[0])
            zero_dump.start()
            zero_dump.wait()
        gather(0, 0)

    def block_step(s):
        wait_gather(s)

        @pl.when(i > 0)
        def _():
            wait_scatter(s)

        gather(jnp.minimum(i + 1, nb - 1), 1 - s)
        scatter(jnp.where(i == 0, n_blk, i - 1), 1 - s)
        x = xbuf[s][...]
        xb = x[:, 0:d].astype(BF16)
        obuf[s][...] = (x[:, d:d + 1] * _swiglu(xb, sa1, sa3, sa2)
                        + x[:, d + 1:d + 2] * _swiglu(xb, sb1, sb3, sb2))

        @pl.when(i == nb - 1)
        def _():
            wait_gather(1 - s)
            wait_scatter(1 - s)
            scatter(i, s)
            wait_scatter(s)

    for s in range(2):
        @pl.when(jnp.logical_and(i < nb, i % 2 == s))
        def _(s=s):
            block_step(s)


def _experts_call(haug, dest, blk_a, blk_b, valid_end, pad_end, n_used, w1, w3, w2, layer, blk):
    t, da = haug.shape
    d = da - LANES
    de = w1.shape[3]
    n_blk = blk_a.shape[0]
    assert t & (t - 1) == 0 and blk & (blk - 1) == 0, "token count and block size must be powers of two"
    amap = lambda i, dest, ba, bb, ve, pe, nb: (layer, ba[i], 0, 0)
    bmap = lambda i, dest, ba, bb, ve, pe, nb: (layer, bb[i], 0, 0)
    up, down = (None, 1, d, de), (None, 1, de, d)
    wspecs = [pl.BlockSpec(up, amap), pl.BlockSpec(up, amap), pl.BlockSpec(down, amap),
              pl.BlockSpec(up, bmap), pl.BlockSpec(up, bmap), pl.BlockSpec(down, bmap)]
    wscratch = [pltpu.VMEM((d, de), BF16), pltpu.VMEM((d, de), BF16), pltpu.VMEM((de, d), BF16)] * 2
    return pl.pallas_call(
        functools.partial(_experts_kernel, blk=blk, t=t, d=d, n_blk=n_blk),
        grid_spec=pltpu.PrefetchScalarGridSpec(
            num_scalar_prefetch=6,
            grid=(n_blk,),
            in_specs=[pl.BlockSpec(memory_space=pl.ANY)] + wspecs,
            out_specs=pl.BlockSpec(memory_space=pl.ANY),
            scratch_shapes=[pltpu.SMEM(((n_blk + 1) * blk,), jnp.int32),
                            pltpu.VMEM((blk, da), F32),
                            pltpu.VMEM((blk, da), F32),
                            pltpu.VMEM((blk, d), F32),
                            pltpu.VMEM((blk, d), F32)] + wscratch
            + [pltpu.SemaphoreType.DMA((2,)),
               pltpu.SemaphoreType.DMA((2,))],
        ),
        out_shape=jax.ShapeDtypeStruct((t + 3 * blk, d), F32),
        compiler_params=_cparams(("arbitrary",)),
        name="moe_experts",
    )(dest, blk_a, blk_b, valid_end, pad_end, n_used, haug, w1, w3, w2, w1, w3, w2)


def _combine_kernel(h_ref, y_ref, g_ref, b_ref, o_ref, *, alpha):
    o_ref[...] = _layer_norm(alpha * h_ref[...] + y_ref[...], g_ref[...], b_ref[...])


def _combine_call(haug, y, ln_g, ln_b, alpha, tm=512):
    t = haug.shape[0]
    d = y.shape[1]
    tm = min(tm, t)
    return pl.pallas_call(
        functools.partial(_combine_kernel, alpha=alpha),
        grid=(t // tm,),
        in_specs=[pl.BlockSpec((tm, d), lambda i: (i, 0)),
                  pl.BlockSpec((tm, d), lambda i: (i, 0)),
                  pl.BlockSpec((1, d), lambda i: (0, 0)),
                  pl.BlockSpec((1, d), lambda i: (0, 0))],
        out_specs=pl.BlockSpec((tm, d), lambda i: (i, 0)),
        out_shape=jax.ShapeDtypeStruct((t, d), F32),
        compiler_params=_cparams(("parallel",)),
        name="moe_combine",
    )(haug, y, ln_g.reshape(1, d), ln_b.reshape(1, d))


def _moe_layer(haug, cls, w1, w3, w2, layer, ln_g, ln_b, alpha, blk=256):
    t = haug.shape[0]
    dest, counts = _rank_call(cls, blk)
    n_blk = t // blk + N_CLASSES
    cnt = counts[:, 0]
    padded = ((cnt + blk - 1) // blk) * blk
    pad_end = jnp.cumsum(padded)
    blk_start = jnp.arange(n_blk, dtype=jnp.int32) * blk
    blk_cls = jnp.minimum(jnp.sum(pad_end[None, :] <= blk_start[:, None], axis=1), N_CLASSES - 1)
    group = blk_cls // N_PAIRS
    pair = blk_cls % N_PAIRS
    blk_a = (group * EXPERTS_PER_GROUP + jnp.asarray(PAIR_A, jnp.int32)[pair]).astype(jnp.int32)
    blk_b = (group * EXPERTS_PER_GROUP + jnp.asarray(PAIR_B, jnp.int32)[pair]).astype(jnp.int32)
    valid_end = (pad_end - padded + cnt).astype(jnp.int32)
    n_used = (pad_end[N_CLASSES - 1:N_CLASSES] // blk).astype(jnp.int32)
    y = _experts_call(haug, dest.reshape(-1), blk_a, blk_b, valid_end, pad_end.astype(jnp.int32), n_used,
                      w1, w3, w2, layer, blk)
    return _combine_call(haug, y, ln_g, ln_b, alpha)


def kernel(x, positions, ln_in_g, ln_in_b, w_in, conv_w, sgu_ln_g, sgu_ln_b, w_s, b_s, lambda_q1, lambda_k1, lambda_q2, lambda_k2, diff_subln_g, w_gate, b_gate, w_branch, w_o, ln1_g, ln1_b, w_router, b_router, w1, w3, w2, ln2_g, ln2_b):
    bsz, seq, d = x.shape
    depth = w_in.shape[0]
    alpha = (2.0 * depth) ** 0.25
    t = bsz * seq
    cos_t, sin_t = _rope_tables(positions)
    w_in_b, w_gate_b, w_branch_b, w_o_b = (w.astype(BF16) for w in (w_in, w_gate, w_branch, w_o))
    w1_b, w3_b, w2_b = (w.astype(BF16) for w in (w1, w3, w2))
    h = x.reshape(t, d)
    for l in range(depth):
        lambda_init = 0.8 - 0.6 * math.exp(-0.3 * l)
        if l == 0:
            h, z = _proj_call(h, w_in_b, l, ln=(ln_in_g, ln_in_b))
        else:
            z = _proj_call(h, w_in_b, l)
        y_conv = _conv_call(z, conv_w[l], bsz, seq)
        y_ret = _ret_call(z, cos_t, sin_t, bsz, seq)
        y_sgu = _sgu_call(z, sgu_ln_g[l], sgu_ln_b[l], w_s[l], b_s[l])
        qt, kr, vt = _qkprep_call(z, cos_t, sin_t, bsz, seq)
        lam_params = jnp.stack([lambda_q1[l], lambda_k1[l], lambda_q2[l], lambda_k2[l]])
        y_diff = _flash_call(qt, kr, vt, lam_params, diff_subln_g[l], bsz, seq, lambda_init)
        haug, cls = _merge_call(h, (y_conv, y_ret, y_sgu, y_diff), w_gate_b, b_gate[l], w_branch_b, w_o_b, l,
                                ln1_g[l], ln1_b[l], w_router, b_router, alpha)
        h = _moe_layer(haug, cls, w1_b, w3_b, w2_b, l, ln2_g[l], ln2_b[l], alpha)
    return h.reshape(bsz, seq, d)
```

```python
import functools
import math

import jax
import jax.numpy as jnp
from jax import lax
from jax.experimental import pallas as pl
from jax.experimental.pallas import tpu as pltpu

D_MODEL = 1024
BRANCH_WIDTH = 512
N_BRANCH = 4
CONV_WIDTH = BRANCH_WIDTH
CONV_K = 3
RET_HEADS = 4
RET_DK = 64
RET_DV = 128
RET_CHUNK = 128
RET_THETA = 10000.0
SGU_GROUPS = 4
SGU_GROUP_DIM = BRANCH_WIDTH // SGU_GROUPS
SGU_CHUNK = 128
SGU_WIDTH = BRANCH_WIDTH
DIFF_HEADS = 4
DIFF_HEAD_DIM = 64
DIFF_V_DIM = 2 * DIFF_HEAD_DIM
ROPE_THETA = 500000.0
ROT_DIM = DIFF_HEAD_DIM // 4
CONV_COLS = 3 * CONV_WIDTH
RET_COLS = 2 * RET_HEADS * RET_DK + 2 * RET_HEADS * RET_DV
SGU_COLS = 2 * SGU_WIDTH
DIFF_COLS = 2 * DIFF_HEADS * 2 * DIFF_HEAD_DIM + DIFF_HEADS * DIFF_V_DIM
IN_COLS = CONV_COLS + RET_COLS + SGU_COLS + DIFF_COLS
N_EXPERTS = 16
N_GROUPS = 4
EXPERTS_PER_GROUP = N_EXPERTS // N_GROUPS
TOP_K = 2
D_EXPERT = 1024
LN_EPS = 1e-5
RMS_EPS = 1e-6

LANES = 128
CONV_OFF = 0
RET_OFF = CONV_COLS
SGU_OFF = RET_OFF + RET_COLS
DIFF_OFF = SGU_OFF + SGU_COLS

NEG_BIG = -1e30
VMEM_LIMIT = 56 * 1024 * 1024

BF16 = jnp.bfloat16
F32 = jnp.float32


def _cparams(sem):
    return pltpu.CompilerParams(dimension_semantics=sem, vmem_limit_bytes=VMEM_LIMIT)


def _layer_norm(xf, g, b):
    mu = jnp.mean(xf, axis=-1, keepdims=True)
    xc = xf - mu
    var = jnp.mean(xc * xc, axis=-1, keepdims=True)
    return xc * lax.rsqrt(var + LN_EPS) * g + b


def _dot(a, b):
    return jnp.dot(a, b, preferred_element_type=F32)


def _dot_nt(a, b):
    return lax.dot_general(a, b, (((1,), (1,)), ((), ())), preferred_element_type=F32)


def _dot_tn(a, b):
    return lax.dot_general(a, b, (((0,), (0,)), ((), ())), preferred_element_type=F32)


def _rope_kernel(pos_ref, freq_ref, sign_ref, cos_ref, sin_ref):
    ang = pos_ref[...] * freq_ref[...]
    cos_ref[...] = jnp.cos(ang)
    sin_ref[...] = jnp.sin(ang) * sign_ref[...]


def _rope_tables(positions):
    t = positions.size
    pos = positions.reshape(t, 1).astype(F32)
    half_r = RET_DK // 2
    fr = RET_THETA ** (-jnp.arange(half_r, dtype=F32) / half_r)
    half_d = ROT_DIM // 2
    fd = ROPE_THETA ** (-jnp.arange(half_d, dtype=F32) / half_d)
    zeros_d = jnp.zeros((DIFF_HEAD_DIM - ROT_DIM,), F32)
    freq = jnp.concatenate([fr, fr, fd, fd, zeros_d]).reshape(1, LANES)
    sign = jnp.concatenate([-jnp.ones((half_r,), F32), jnp.ones((half_r,), F32),
                            -jnp.ones((half_d,), F32), jnp.ones((half_d,), F32),
                            zeros_d]).reshape(1, LANES)
    tm = min(t, 2048)
    return pl.pallas_call(
        _rope_kernel,
        grid=(t // tm,),
        in_specs=[pl.BlockSpec((tm, 1), lambda i: (i, 0)),
                  pl.BlockSpec((1, LANES), lambda i: (0, 0)),
                  pl.BlockSpec((1, LANES), lambda i: (0, 0))],
        out_specs=[pl.BlockSpec((tm, LANES), lambda i: (i, 0)),
                   pl.BlockSpec((tm, LANES), lambda i: (i, 0))],
        out_shape=[jax.ShapeDtypeStruct((t, LANES), F32)] * 2,
        compiler_params=_cparams(("parallel",)),
        name="rope_tables",
    )(pos, freq, sign)


def _tile_lanes(x, reps):
    return jnp.concatenate([x] * reps, axis=1)


def _rotate_half_split(x, cos, sin_signed, group, half):
    w = x.shape[1]
    lane = lax.broadcasted_iota(jnp.int32, x.shape, 1) % group
    partner = jnp.where(lane < half, pltpu.roll(x, w - half, axis=1), pltpu.roll(x, half, axis=1))
    return x * cos + partner * sin_signed


PROJ_CHUNK = 512


def _project(hb, w_ref, z_ref):
    for n0 in range(0, IN_COLS, PROJ_CHUNK):
        z_ref[:, n0:n0 + PROJ_CHUNK] = _dot(hb, w_ref[:, n0:n0 + PROJ_CHUNK]).astype(BF16)


def _proj_kernel(h_ref, w_ref, z_ref):
    _project(h_ref[...].astype(BF16), w_ref, z_ref)


def _ln_proj_kernel(x_ref, g_ref, b_ref, w_ref, h_ref, z_ref):
    h = _layer_norm(x_ref[...], g_ref[...], b_ref[...])
    h_ref[...] = h
    _project(h.astype(BF16), w_ref, z_ref)


def _proj_call(h, w_all, layer, ln=None, tm=512):
    t, d = h.shape
    row = pl.BlockSpec((tm, d), lambda i: (i, 0))
    vec = pl.BlockSpec((1, d), lambda i: (0, 0))
    wspec = pl.BlockSpec((None, d, IN_COLS), lambda i: (layer, 0, 0), pipeline_mode=pl.Buffered(1))
    zspec = pl.BlockSpec((tm, IN_COLS), lambda i: (i, 0))
    zshape = jax.ShapeDtypeStruct((t, IN_COLS), BF16)
    if ln is None:
        return pl.pallas_call(
            _proj_kernel, grid=(t // tm,), in_specs=[row, wspec], out_specs=zspec, out_shape=zshape,
            compiler_params=_cparams(("parallel",)), name="proj_in",
        )(h, w_all)
    return pl.pallas_call(
        _ln_proj_kernel, grid=(t // tm,), in_specs=[row, vec, vec, wspec], out_specs=[row, zspec],
        out_shape=[jax.ShapeDtypeStruct((t, d), F32), zshape],
        compiler_params=_cparams(("parallel",)), name="ln_proj_in",
    )(h, ln[0].reshape(1, d), ln[1].reshape(1, d), w_all)


CONV_HALO = 16


def _conv_kernel(b_ref, c_ref, u_ref, ch_ref, uh_ref, w_ref, o_ref):
    ts = c_ref.shape[0]
    first = pl.program_id(1) == 0
    cu = c_ref[...].astype(F32) * u_ref[...].astype(F32)
    halo = ch_ref[...].astype(F32) * uh_ref[...].astype(F32)
    halo = jnp.where(first, 0.0, halo)
    ext = jnp.concatenate([halo, cu], axis=0)
    n = ext.shape[0]
    prev1 = pltpu.roll(ext, 1, axis=0)[CONV_HALO:n]
    prev2 = pltpu.roll(ext, 2, axis=0)[CONV_HALO:n]
    w = w_ref[...]
    y = prev2 * w[0:1, :] + prev1 * w[1:2, :] + cu * w[2:3, :]
    o_ref[...] = (b_ref[...].astype(F32) * y).astype(BF16)


def _conv_call(z, conv_w, bsz, seq, ts=1024):
    t = bsz * seq
    ts = min(ts, seq)
    nt = seq // ts
    wb = CONV_WIDTH
    hb = ts // CONV_HALO
    col = lambda k: (lambda b, i: (b * nt + i, CONV_OFF // wb + k))
    halo = lambda k: (lambda b, i: (jnp.maximum((b * nt + i) * hb - 1, 0), CONV_OFF // wb + k))
    return pl.pallas_call(
        _conv_kernel,
        grid=(bsz, nt),
        in_specs=[pl.BlockSpec((ts, wb), col(0)),
                  pl.BlockSpec((ts, wb), col(1)),
                  pl.BlockSpec((ts, wb), col(2)),
                  pl.BlockSpec((CONV_HALO, wb), halo(1)),
                  pl.BlockSpec((CONV_HALO, wb), halo(2)),
                  pl.BlockSpec((CONV_K, wb), lambda b, i: (0, 0))],
        out_specs=pl.BlockSpec((ts, wb), lambda b, i: (b * nt + i, 0)),
        out_shape=jax.ShapeDtypeStruct((t, wb), BF16),
        compiler_params=_cparams(("parallel", "parallel")),
        name="conv_mixer",
    )(z, z, z, z, z, conv_w.reshape(CONV_K, wb))


def _sgu_kernel(u_ref, v_ref, g_ref, b_ref, ws_ref, bias_ref, o_ref):
    ts = u_ref.shape[0]
    c = SGU_CHUNK
    v = _layer_norm(v_ref[...].astype(F32), g_ref[...], b_ref[...]).astype(BF16)
    row = lax.broadcasted_iota(jnp.int32, (c, c), 0)
    colm = lax.broadcasted_iota(jnp.int32, (c, c), 1)
    bias = bias_ref[...]
    for g in range(SGU_GROUPS):
        w = jnp.where(row >= colm, ws_ref[g], 0.0).astype(BF16)
        lo = g * SGU_GROUP_DIM
        for n in range(ts // c):
            s = _dot(w, v[n * c:(n + 1) * c, lo:lo + SGU_GROUP_DIM]) + bias[:, lo:lo + SGU_GROUP_DIM]
            u = u_ref[n * c:(n + 1) * c, lo:lo + SGU_GROUP_DIM].astype(F32)
            o_ref[n * c:(n + 1) * c, lo:lo + SGU_GROUP_DIM] = (u * s).astype(BF16)


def _sgu_call(z, ln_g, ln_b, w_s, b_s, ts=1024):
    t = z.shape[0]
    ts = min(ts, t)
    wb = SGU_WIDTH
    bias = jnp.repeat(b_s.T, SGU_GROUP_DIM, axis=1)
    return pl.pallas_call(
        _sgu_kernel,
        grid=(t // ts,),
        in_specs=[pl.BlockSpec((ts, wb), lambda i: (i, SGU_OFF // wb)),
                  pl.BlockSpec((ts, wb), lambda i: (i, SGU_OFF // wb + 1)),
                  pl.BlockSpec((1, wb), lambda i: (0, 0)),
                  pl.BlockSpec((1, wb), lambda i: (0, 0)),
                  pl.BlockSpec((SGU_GROUPS, SGU_CHUNK, SGU_CHUNK), lambda i: (0, 0, 0)),
                  pl.BlockSpec((SGU_CHUNK, wb), lambda i: (0, 0))],
        out_specs=pl.BlockSpec((ts, wb), lambda i: (i, 0)),
        out_shape=jax.ShapeDtypeStruct((t, wb), BF16),
        compiler_params=_cparams(("parallel",)),
        name="sgu_mixer",
    )(z, z, ln_g.reshape(1, wb), ln_b.reshape(1, wb), w_s, bias)


def _ret_tables():
    c = RET_CHUNK
    log_gamma = jnp.log1p(-jnp.exp2(-5.0 - jnp.arange(RET_HEADS, dtype=F32)))
    idx = jnp.arange(c, dtype=F32)
    rel = idx[:, None] - idx[None, :]
    decay = jnp.where(rel >= 0, jnp.exp(jnp.maximum(rel, 0.0)[None] * log_gamma[:, None, None]), 0.0)
    k_decay = jnp.exp((c - 1 - idx)[:, None] * log_gamma[None, :])
    q_decay = jnp.exp((idx + 1.0)[:, None] * log_gamma[None, :])
    chunk_decay = jnp.exp(c * log_gamma)
    hk = RET_HEADS * RET_DK
    kd = jnp.repeat(k_decay, RET_DK, axis=1) * (RET_DK ** -0.5)
    qd = jnp.repeat(q_decay, RET_DK, axis=1)
    cd = jnp.broadcast_to(jnp.repeat(chunk_decay, RET_DV)[None, :], (8, RET_HEADS * RET_DV))
    del hk
    return decay, kd, qd, cd


def _ret_kernel(q_ref, k_ref, v_ref, g_ref, cos_ref, sin_ref, decay_ref, kd_ref, qd_ref, cd_ref,
                o_ref, state_ref):
    ts = q_ref.shape[0]
    c = RET_CHUNK
    hk = RET_HEADS * RET_DK

    @pl.when(pl.program_id(1) == 0)
    def _():
        state_ref[...] = jnp.zeros_like(state_ref)

    cos = _tile_lanes(cos_ref[:, 0:RET_DK], RET_HEADS)
    sin = _tile_lanes(sin_ref[:, 0:RET_DK], RET_HEADS)
    q = _rotate_half_split(q_ref[...].astype(F32), cos, sin, RET_DK, RET_DK // 2)
    k = _rotate_half_split(k_ref[...].astype(F32), cos, sin, RET_DK, RET_DK // 2)
    del hk
    for n in range(ts // c):
        r0 = n * c
        qn = q[r0:r0 + c]
        kn = k[r0:r0 + c]
        qb = qn.astype(BF16)
        kb = (kn * (RET_DK ** -0.5)).astype(BF16)
        qdb = (qn * qd_ref[...]).astype(BF16)
        kdb = (kn * kd_ref[...]).astype(BF16)
        for h in range(RET_HEADS):
            ks = slice(h * RET_DK, (h + 1) * RET_DK)
            vs = slice(h * RET_DV, (h + 1) * RET_DV)
            vb = v_ref[r0:r0 + c, vs]
            scores = _dot_nt(qb[:, ks], kb[:, ks]) * decay_ref[h]
            inner = _dot(scores.astype(BF16), vb)
            state = state_ref[h]
            cross = _dot(qdb[:, ks], state.astype(BF16))
            kv = _dot_tn(kdb[:, ks], vb)
            state_ref[h] = state * cd_ref[0:1, vs] + kv
            o = inner + cross
            o = o * lax.rsqrt(jnp.mean(o * o, axis=-1, keepdims=True) + RMS_EPS)
            gate = g_ref[r0:r0 + c, vs].astype(F32)
            gate = gate * jax.nn.sigmoid(gate)
            o_ref[r0:r0 + c, vs] = (gate * o).astype(BF16)


def _ret_call(z, cos_t, sin_t, bsz, seq, ts=512):
    t = bsz * seq
    ts = min(ts, seq)
    nt = seq // ts
    decay, kd, qd, cd = _ret_tables()
    hk = RET_HEADS * RET_DK
    hv = RET_HEADS * RET_DV
    row = lambda b, i: b * nt + i
    return pl.pallas_call(
        _ret_kernel,
        grid=(bsz, nt),
        in_specs=[pl.BlockSpec((ts, hk), lambda b, i: (row(b, i), RET_OFF // hk)),
                  pl.BlockSpec((ts, hk), lambda b, i: (row(b, i), RET_OFF // hk + 1)),
                  pl.BlockSpec((ts, hv), lambda b, i: (row(b, i), (RET_OFF + 2 * hk) // hv)),
                  pl.BlockSpec((ts, hv), lambda b, i: (row(b, i), (RET_OFF + 2 * hk) // hv + 1)),
                  pl.BlockSpec((ts, LANES), lambda b, i: (row(b, i), 0)),
                  pl.BlockSpec((ts, LANES), lambda b, i: (row(b, i), 0)),
                  pl.BlockSpec((RET_HEADS, RET_CHUNK, RET_CHUNK), lambda b, i: (0, 0, 0)),
                  pl.BlockSpec((RET_CHUNK, hk), lambda b, i: (0, 0)),
                  pl.BlockSpec((RET_CHUNK, hk), lambda b, i: (0, 0)),
                  pl.BlockSpec((8, hv), lambda b, i: (0, 0))],
        out_specs=pl.BlockSpec((ts, hv), lambda b, i: (row(b, i), 0)),
        out_shape=jax.ShapeDtypeStruct((t, hv), BF16),
        scratch_shapes=[pltpu.VMEM((RET_HEADS, RET_DK, RET_DV), F32)],
        compiler_params=_cparams(("parallel", "arbitrary")),
        name="ret_mixer",
    )(z, z, z, z, cos_t, sin_t, decay, kd, qd, cd)


LOG2E = 1.4426950408889634


def _qkprep_kernel(q_ref, k_ref, v_ref, cos_ref, sin_ref, qt_ref, ko_ref, vt_ref):
    reps = q_ref.shape[1] // DIFF_HEAD_DIM
    cos = _tile_lanes(cos_ref[:, DIFF_HEAD_DIM:2 * DIFF_HEAD_DIM], reps)
    sin = _tile_lanes(sin_ref[:, DIFF_HEAD_DIM:2 * DIFF_HEAD_DIM], reps)
    q = _rotate_half_split(q_ref[...].astype(F32), cos, sin, DIFF_HEAD_DIM, ROT_DIM // 2)
    k = _rotate_half_split(k_ref[...].astype(F32), cos, sin, DIFF_HEAD_DIM, ROT_DIM // 2)
    qt_ref[...] = (q * (DIFF_HEAD_DIM ** -0.5 * LOG2E)).T.astype(BF16)
    ko_ref[...] = k.astype(BF16)
    vt_ref[...] = v_ref[...].astype(F32).T.astype(BF16)


def _qkprep_call(z, cos_t, sin_t, bsz, seq, tm=1024):
    t = bsz * seq
    tm = min(tm, seq)
    nt = seq // tm
    hq = DIFF_HEADS * 2 * DIFF_HEAD_DIM
    row = lambda b, i: b * nt + i
    return pl.pallas_call(
        _qkprep_kernel,
        grid=(bsz, nt),
        in_specs=[pl.BlockSpec((tm, hq), lambda b, i: (row(b, i), DIFF_OFF // hq)),
                  pl.BlockSpec((tm, hq), lambda b, i: (row(b, i), DIFF_OFF // hq + 1)),
                  pl.BlockSpec((tm, hq), lambda b, i: (row(b, i), DIFF_OFF // hq + 2)),
                  pl.BlockSpec((tm, LANES), lambda b, i: (row(b, i), 0)),
                  pl.BlockSpec((tm, LANES), lambda b, i: (row(b, i), 0))],
        out_specs=[pl.BlockSpec((None, hq, tm), lambda b, i: (b, 0, i)),
                   pl.BlockSpec((tm, hq), lambda b, i: (row(b, i), 0)),
                   pl.BlockSpec((None, hq, tm), lambda b, i: (b, 0, i))],
        out_shape=[jax.ShapeDtypeStruct((bsz, hq, seq), BF16),
                   jax.ShapeDtypeStruct((t, hq), BF16),
                   jax.ShapeDtypeStruct((bsz, hq, seq), BF16)],
        compiler_params=_cparams(("parallel", "parallel")),
        name="diff_qkprep",
    )(z, z, z, cos_t, sin_t)


SUM_ROWS = 16
KB_PER_STEP = 4


def _flash_kernel(qi_ref, ki_ref, qt_ref, k_ref, vt_ref, lam_ref, g_ref, o_ref,
                  qd_ref, m_ref, acc_ref, *, tb, cq, lambda_init):
    p = pl.program_id(2)
    qi = qi_ref[p]
    kp = ki_ref[p]
    hd = DIFF_HEAD_DIM
    dv = DIFF_V_DIM

    @pl.when(kp == 0)
    def _():
        qd_ref[...] = jnp.zeros_like(qd_ref)
        qd_ref[0:hd, 0:tb] = qt_ref[0:hd, :]
        qd_ref[hd:2 * hd, tb:2 * tb] = qt_ref[hd:2 * hd, :]
        m_ref[...] = jnp.full_like(m_ref, NEG_BIG)
        acc_ref[...] = jnp.zeros_like(acc_ref)

    def step(diagonal, half):
        k = k_ref[half * tb:(half + 1) * tb, :]
        vta = jnp.concatenate([vt_ref[:, half * tb:(half + 1) * tb], jnp.ones((SUM_ROWS, tb), BF16)],
                              axis=0)
        nc = 2 * tb // cq
        cols = [slice(c * cq, (c + 1) * cq) for c in range(nc)]
        nkey = [((c * cq) % tb + cq) if diagonal else tb for c in range(nc)]
        m_prev = m_ref[...]
        s = [_dot(k[0:nkey[c]], qd_ref[:, cols[c]]) for c in range(nc)]
        if diagonal:
            for c in range(nc):
                key = lax.broadcasted_iota(jnp.int32, (nkey[c], cq), 0)
                qpos = lax.broadcasted_iota(jnp.int32, (nkey[c], cq), 1) + (c * cq) % tb
                s[c] = jnp.where(key <= qpos, s[c], NEG_BIG)
        m_new = [jnp.maximum(m_prev[:, cols[c]], jnp.max(s[c], axis=0, keepdims=True)) for c in range(nc)]
        pexp = [jnp.exp2(s[c] - m_new[c]).astype(BF16) for c in range(nc)]
        alpha = [jnp.exp2(m_prev[:, cols[c]] - m_new[c]) for c in range(nc)]
        for c in range(nc):
            cs = cols[c]
            m_ref[:, cs] = m_new[c]
            acc_ref[:, cs] = alpha[c] * acc_ref[:, cs] + _dot(vta[:, 0:nkey[c]], pexp[c])

    def finalize():
        lam_p = lam_ref[...]
        lam = (jnp.exp(jnp.sum(lam_p[0:1] * lam_p[1:2], axis=-1, keepdims=True))
               - jnp.exp(jnp.sum(lam_p[2:3] * lam_p[3:4], axis=-1, keepdims=True)) + lambda_init)
        o = acc_ref[0:dv, :] / acc_ref[dv:dv + 1, :]
        o = (o[:, 0:tb] - lam * o[:, tb:2 * tb]).T
        o = o * lax.rsqrt(jnp.mean(o * o, axis=-1, keepdims=True) + RMS_EPS)
        o_ref[...] = (o * g_ref[...] * (1.0 - lambda_init)).astype(BF16)

    for half in range(KB_PER_STEP):
        kb = KB_PER_STEP * kp + half

        @pl.when(kb < qi)
        def _(half=half):
            step(False, half)

        @pl.when(kb == qi)
        def _(half=half):
            step(True, half)
            finalize()


def _flash_call(qt, kr, vt, lam_params, subln_g, bsz, seq, lambda_init, tb=1024, cq=256):
    t = bsz * seq
    kg = KB_PER_STEP
    tb = min(tb, seq // kg)
    cq = min(cq, tb)
    nb = seq // tb
    assert nb % kg == 0, "key blocks are processed in groups of KB_PER_STEP"
    qi_list, ki_list = [], []
    for qi in range(nb):
        for kp in range(qi // kg + 1):
            qi_list.append(qi)
            ki_list.append(kp)
    qi_arr = jnp.asarray(qi_list, jnp.int32)
    ki_arr = jnp.asarray(ki_list, jnp.int32)
    hd = 2 * DIFF_HEAD_DIM
    grid_spec = pltpu.PrefetchScalarGridSpec(
        num_scalar_prefetch=2,
        grid=(bsz, DIFF_HEADS, len(qi_list)),
        in_specs=[pl.BlockSpec((None, hd, tb), lambda b, h, p, qi, ki: (b, h, qi[p])),
                  pl.BlockSpec((kg * tb, hd), lambda b, h, p, qi, ki: (b * (nb // kg) + ki[p], h)),
                  pl.BlockSpec((None, DIFF_V_DIM, kg * tb), lambda b, h, p, qi, ki: (b, h, ki[p])),
                  pl.BlockSpec((4, DIFF_HEAD_DIM), lambda b, h, p, qi, ki: (0, 0)),
                  pl.BlockSpec((1, DIFF_V_DIM), lambda b, h, p, qi, ki: (0, 0))],
        out_specs=pl.BlockSpec((tb, DIFF_V_DIM), lambda b, h, p, qi, ki: (b * nb + qi[p], h)),
        scratch_shapes=[pltpu.VMEM((hd, 2 * tb), BF16),
                        pltpu.VMEM((1, 2 * tb), F32),
                        pltpu.VMEM((DIFF_V_DIM + SUM_ROWS, 2 * tb), F32)],
    )
    return pl.pallas_call(
        functools.partial(_flash_kernel, tb=tb, cq=cq, lambda_init=lambda_init),
        grid_spec=grid_spec,
        out_shape=jax.ShapeDtypeStruct((t, DIFF_HEADS * DIFF_V_DIM), BF16),
        compiler_params=_cparams(("parallel", "parallel", "arbitrary")),
        name="diff_flash",
    )(qi_arr, ki_arr, qt, kr, vt, lam_params, subln_g.reshape(1, DIFF_V_DIM))


def _merge_kernel(h_ref, y0_ref, y1_ref, y2_ref, y3_ref, wg_ref, bg_ref, wb_ref, wo_ref,
                  g_ref, b_ref, wr_ref, br_ref, o_ref, cls_ref, *, alpha):
    d = h_ref.shape[1]
    h = h_ref[...]
    hb = h.astype(BF16)
    merged = None
    for g, y_ref in enumerate((y0_ref, y1_ref, y2_ref, y3_ref)):
        gate = jax.nn.sigmoid(_dot(hb, wg_ref[g]) + bg_ref[g:g + 1, :])
        term = gate * _dot(y_ref[...], wb_ref[g])
        merged = term if merged is None else merged + term
    t = _dot(merged.astype(BF16), wo_ref[...])
    out = _layer_norm(alpha * h + t, g_ref[...], b_ref[...])
    o_ref[:, 0:d] = out
    cls, ga, gb = _route(out, wr_ref[...], br_ref[...])
    cls_ref[...] = cls
    g8 = jnp.concatenate([ga, gb, jnp.zeros((6, ga.shape[1]), F32)], axis=0)
    sel = (lax.broadcasted_iota(jnp.int32, (8, LANES), 0)
           == lax.broadcasted_iota(jnp.int32, (8, LANES), 1)).astype(F32)
    o_ref[:, d:d + LANES] = lax.dot_general(g8, sel, (((0,), (0,)), ((), ())),
                                            preferred_element_type=F32,
                                            precision=lax.Precision.HIGHEST)


def _merge_call(h, ys, wg, bg, wb, wo, layer, ln_g, ln_b, w_router, b_router, alpha, tm=512):
    t, d = h.shape
    tm = min(tm, t)
    w = BRANCH_WIDTH
    const = dict(pipeline_mode=pl.Buffered(1))
    return pl.pallas_call(
        functools.partial(_merge_kernel, alpha=alpha),
        grid=(t // tm,),
        in_specs=[pl.BlockSpec((tm, d), lambda i: (i, 0))]
        + [pl.BlockSpec((tm, w), lambda i: (i, 0))] * N_BRANCH
        + [pl.BlockSpec((None, N_BRANCH, d, d), lambda i: (layer, 0, 0, 0), **const),
           pl.BlockSpec((N_BRANCH, d), lambda i: (0, 0)),
           pl.BlockSpec((None, N_BRANCH, w, d), lambda i: (layer, 0, 0, 0), **const),
           pl.BlockSpec((None, d, d), lambda i: (layer, 0, 0), **const),
           pl.BlockSpec((1, d), lambda i: (0, 0)),
           pl.BlockSpec((1, d), lambda i: (0, 0)),
           pl.BlockSpec((N_EXPERTS, d), lambda i: (0, 0)),
           pl.BlockSpec((N_EXPERTS, 1), lambda i: (0, 0))],
        out_specs=[pl.BlockSpec((tm, d + LANES), lambda i: (i, 0)),
                   pl.BlockSpec((1, tm), lambda i: (0, i))],
        out_shape=[jax.ShapeDtypeStruct((t, d + LANES), F32),
                   jax.ShapeDtypeStruct((1, t), jnp.int32)],
        compiler_params=_cparams(("parallel",)),
        name="merge",
    )(h, *ys, wg, bg, wb, wo, ln_g.reshape(1, d), ln_b.reshape(1, d),
      w_router.T, b_router.reshape(N_EXPERTS, 1))


N_PAIRS = 6
PAIR_A = (0, 0, 0, 1, 1, 3)
PAIR_B = (1, 2, 3, 3, 2, 2)
N_CLASSES = N_GROUPS * N_PAIRS
CLASS_ROWS = 32


def _route(h, wr, br):
    logits = lax.dot_general(wr, h, (((1,), (1,)), ((), ())),
                             preferred_element_type=F32, precision=lax.Precision.HIGHEST)
    mx = jnp.max(logits, axis=0, keepdims=True)
    e = jnp.exp(logits - mx)
    scores = e / jnp.sum(e, axis=0, keepdims=True)
    sel = scores + br
    epg = EXPERTS_PER_GROUP
    rows = [sel[i:i + 1, :] for i in range(N_EXPERTS)]
    srows = [scores[i:i + 1, :] for i in range(N_EXPERTS)]
    best_score = None
    best = None
    for g in range(N_GROUPS):
        r = rows[g * epg:(g + 1) * epg]
        gs = None
        for a in range(epg):
            for b in range(a + 1, epg):
                pair = r[a] + r[b]
                gs = pair if gs is None else jnp.maximum(gs, pair)
        if best is None:
            best_score, best = gs, jnp.zeros(gs.shape, jnp.int32)
        else:
            take = gs > best_score
            best_score = jnp.where(take, gs, best_score)
            best = jnp.where(take, g, best)
    cand, cscore = [], []
    for j in range(epg):
        c = rows[j]
        s = srows[j]
        for g in range(1, N_GROUPS):
            c = jnp.where(best == g, rows[g * epg + j], c)
            s = jnp.where(best == g, srows[g * epg + j], s)
        cand.append(c)
        cscore.append(s)
    v1, i1, s1 = cand[0], jnp.zeros(best.shape, jnp.int32), cscore[0]
    for j in range(1, epg):
        take = cand[j] > v1
        v1 = jnp.where(take, cand[j], v1)
        i1 = jnp.where(take, j, i1)
        s1 = jnp.where(take, cscore[j], s1)
    v2 = jnp.full(v1.shape, -jnp.inf, F32)
    i2 = jnp.zeros(best.shape, jnp.int32)
    s2 = jnp.zeros(v1.shape, F32)
    for j in range(epg):
        take = jnp.logical_and(i1 != j, cand[j] > v2)
        v2 = jnp.where(take, cand[j], v2)
        i2 = jnp.where(take, j, i2)
        s2 = jnp.where(take, cscore[j], s2)
    denom = s1 + s2
    g1 = s1 / denom
    g2 = s2 / denom
    lo = jnp.minimum(i1, i2)
    hi = jnp.maximum(i1, i2)
    pair = jnp.where(lo == 0, hi - 1, jnp.where(lo == 1, jnp.where(hi == 3, 3, 4), 5))
    a_loc = jnp.where(pair < 3, 0, jnp.where(pair < 5, 1, 3))
    first_is_a = i1 == a_loc
    return (best * N_PAIRS + pair, jnp.where(first_is_a, g1, g2), jnp.where(first_is_a, g2, g1))


def _rank_kernel(cls_ref, dest_ref, cnt_ref, run_ref, start_ref, *, blk):
    phase = pl.program_id(0)
    i = pl.program_id(1)
    tm = cls_ref.shape[1]
    onehot = lax.broadcasted_iota(jnp.int32, (CLASS_ROWS, tm), 0) == cls_ref[...]
    f = jnp.where(onehot, 1.0, 0.0)
    tot = jnp.sum(f, axis=1, keepdims=True)

    @pl.when(jnp.logical_and(phase == 0, i == 0))
    def _():
        run_ref[...] = jnp.zeros_like(run_ref)

    @pl.when(phase == 0)
    def _():
        run_ref[...] = run_ref[...] + tot

    @pl.when(jnp.logical_and(phase == 1, i == 0))
    def _():
        cnt = run_ref[...]
        cnt_ref[...] = cnt.astype(jnp.int32)
        padded = jnp.ceil(cnt * (1.0 / blk)) * blk
        acc = jnp.zeros((1, LANES), F32)
        for c in range(CLASS_ROWS):
            start_ref[c:c + 1, :] = acc
            acc = acc + padded[c:c + 1, :]
        run_ref[...] = jnp.zeros_like(run_ref)

    @pl.when(phase == 1)
    def _():
        s_idx = lax.broadcasted_iota(jnp.int32, (tm, tm), 0)
        t_idx = lax.broadcasted_iota(jnp.int32, (tm, tm), 1)
        tri = jnp.where(s_idx < t_idx, 1.0, 0.0).astype(BF16)
        before = _dot(f.astype(BF16), tri)
        base = start_ref[:, 0:1] + run_ref[:, 0:1]
        dest = jnp.sum(jnp.where(onehot, base + before, 0.0), axis=0, keepdims=True)
        dest_ref[...] = dest.astype(jnp.int32)
        run_ref[...] = run_ref[...] + tot


def _rank_call(cls, blk, tm=1024):
    t = cls.shape[1]
    tm = min(tm, t)
    return pl.pallas_call(
        functools.partial(_rank_kernel, blk=blk),
        grid=(2, t // tm),
        in_specs=[pl.BlockSpec((1, tm), lambda p, i: (0, i))],
        out_specs=[pl.BlockSpec((1, tm), lambda p, i: (0, i * p)),
                   pl.BlockSpec((CLASS_ROWS, LANES), lambda p, i: (0, 0))],
        out_shape=[jax.ShapeDtypeStruct((1, t), jnp.int32),
                   jax.ShapeDtypeStruct((CLASS_ROWS, LANES), jnp.int32)],
        scratch_shapes=[pltpu.VMEM((CLASS_ROWS, LANES), F32),
                        pltpu.VMEM((CLASS_ROWS, LANES), F32)],
        compiler_params=_cparams(("arbitrary", "arbitrary")),
        name="moe_rank",
    )(cls)


def _swiglu(xb, wb1, wb3, wb2):
    a = _dot(xb, wb1[...])
    b = _dot(xb, wb3[...])
    return _dot((a * jax.nn.sigmoid(a) * b).astype(BF16), wb2[...])


def _experts_kernel(dest_ref, ba_ref, bb_ref, ve_ref, pe_ref, nb_ref, h_ref,
                    w1a, w3a, w2a, w1b, w3b, w2b, y_ref,
                    inv_ref, xbuf0, xbuf1, obuf0, obuf1, sa1, sa3, sa2, sb1, sb3, sb2, gsem, ssem,
                    *, blk, t, d, n_blk):
    i = pl.program_id(0)
    nb = nb_ref[0]
    xbuf = (xbuf0, xbuf1)
    obuf = (obuf0, obuf1)

    prev = jnp.maximum(i - 1, 0)
    for be_ref, srcs, dsts in ((ba_ref, (w1a, w3a, w2a), (sa1, sa3, sa2)),
                               (bb_ref, (w1b, w3b, w2b), (sb1, sb3, sb2))):
        @pl.when(jnp.logical_and(i < nb, jnp.logical_or(i == 0, be_ref[i] != be_ref[prev])))
        def _(srcs=srcs, dsts=dsts):
            for src, dst in zip(srcs, dsts):
                dst[...] = src[0]

    def gather(block, s):
        base = block * blk
        for j in range(blk):
            tok = inv_ref[base + j] & (t - 1)
            pltpu.make_async_copy(h_ref.at[pl.ds(tok, 1), :], xbuf[s].at[pl.ds(j, 1), :],
                                  gsem.at[s]).start(priority=j % 2)

    def wait_gather(s):
        for j in range(blk):
            pltpu.make_async_copy(h_ref.at[pl.ds(0, 1), :], xbuf[s].at[pl.ds(j, 1), :],
                                  gsem.at[s]).wait()

    def scatter(block, s):
        base = block * blk
        for j in range(blk):
            pltpu.make_async_copy(obuf[s].at[pl.ds(j, 1), :], y_ref.at[pl.ds(inv_ref[base + j], 1), :],
                                  ssem.at[s]).start(priority=j % 2)

    def wait_scatter(s):
        for j in range(blk):
            pltpu.make_async_copy(obuf[s].at[pl.ds(j, 1), :], y_ref.at[pl.ds(0, 1), :],
                                  ssem.at[s]).wait()

    @pl.when(i == 0)
    def _():
        def fill(tok, carry):
            inv_ref[dest_ref[tok]] = tok
            return carry

        lax.fori_loop(0, t, fill, 0, unroll=8)

        def pad(r, carry):
            inv_ref[r] = t + (r & (2 * blk - 1))
            return carry

        for c in range(N_CLASSES):
            lax.fori_loop(ve_ref[c], pe_ref[c], pad, 0)

        def dummy(j, carry):
            inv_ref[n_blk * blk + j] = t + 2 * blk + j
            return carry

        lax.fori_loop(0, blk, dummy, 0)
        obuf1[...] = jnp.zeros_like(obuf1)
        for q in range(3):
            zero_dump = pltpu.make_async_copy(obuf1, y_ref.at[pl.ds(t + q * blk, blk), :], ssem.at[0])
            zero_dump.start()
            zero_dump.wait()
        gather(0, 0)

    def block_step(s):
        wait_gather(s)

        @pl.when(i > 0)
        def _():
            wait_scatter(s)

        gather(jnp.minimum(i + 1, nb - 1), 1 - s)
        scatter(jnp.where(i == 0, n_blk, i - 1), 1 - s)
        x = xbuf[s][...]
        xb = x[:, 0:d].astype(BF16)
        obuf[s][...] = (x[:, d:d + 1] * _swiglu(xb, sa1, sa3, sa2)
                        + x[:, d + 1:d + 2] * _swiglu(xb, sb1, sb3, sb2))

        @pl.when(i == nb - 1)
        def _():
            wait_gather(1 - s)
            wait_scatter(1 - s)
            scatter(i, s)
            wait_scatter(s)

    for s in range(2):
        @pl.when(jnp.logical_and(i < nb, i % 2 == s))
        def _(s=s):
            block_step(s)


def _experts_call(haug, dest, blk_a, blk_b, valid_end, pad_end, n_used, w1, w3, w2, layer, blk):
    t, da = haug.shape
    d = da - LANES
    de = w1.shape[3]
    n_blk = blk_a.shape[0]
    assert t & (t - 1) == 0 and blk & (blk - 1) == 0, "token count and block size must be powers of two"
    amap = lambda i, dest, ba, bb, ve, pe, nb: (layer, ba[i], 0, 0)
    bmap = lambda i, dest, ba, bb, ve, pe, nb: (layer, bb[i], 0, 0)
    up, down = (None, 1, d, de), (None, 1, de, d)
    wspecs = [pl.BlockSpec(up, amap), pl.BlockSpec(up, amap), pl.BlockSpec(down, amap),
              pl.BlockSpec(up, bmap), pl.BlockSpec(up, bmap), pl.BlockSpec(down, bmap)]
    wscratch = [pltpu.VMEM((d, de), BF16), pltpu.VMEM((d, de), BF16), pltpu.VMEM((de, d), BF16)] * 2
    return pl.pallas_call(
        functools.partial(_experts_kernel, blk=blk, t=t, d=d, n_blk=n_blk),
        grid_spec=pltpu.PrefetchScalarGridSpec(
            num_scalar_prefetch=6,
            grid=(n_blk,),
            in_specs=[pl.BlockSpec(memory_space=pl.ANY)] + wspecs,
            out_specs=pl.BlockSpec(memory_space=pl.ANY),
            scratch_shapes=[pltpu.SMEM(((n_blk + 1) * blk,), jnp.int32),
                            pltpu.VMEM((blk, da), F32),
                            pltpu.VMEM((blk, da), F32),
                            pltpu.VMEM((blk, d), F32),
                            pltpu.VMEM((blk, d), F32)] + wscratch
            + [pltpu.SemaphoreType.DMA((2,)),
               pltpu.SemaphoreType.DMA((2,))],
        ),
        out_shape=jax.ShapeDtypeStruct((t + 3 * blk, d), F32),
        compiler_params=_cparams(("arbitrary",)),
        name="moe_experts",
    )(dest, blk_a, blk_b, valid_end, pad_end, n_used, haug, w1, w3, w2, w1, w3, w2)


def _combine_kernel(h_ref, y_ref, g_ref, b_ref, o_ref, *, alpha):
    o_ref[...] = _layer_norm(alpha * h_ref[...] + y_ref[...], g_ref[...], b_ref[...])


def _combine_call(haug, y, ln_g, ln_b, alpha, tm=512):
    t = haug.shape[0]
    d = y.shape[1]
    tm = min(tm, t)
    return pl.pallas_call(
        functools.partial(_combine_kernel, alpha=alpha),
        grid=(t // tm,),
        in_specs=[pl.BlockSpec((tm, d), lambda i: (i, 0)),
                  pl.BlockSpec((tm, d), lambda i: (i, 0)),
                  pl.BlockSpec((1, d), lambda i: (0, 0)),
                  pl.BlockSpec((1, d), lambda i: (0, 0))],
        out_specs=pl.BlockSpec((tm, d), lambda i: (i, 0)),
        out_shape=jax.ShapeDtypeStruct((t, d), F32),
        compiler_params=_cparams(("parallel",)),
        name="moe_combine",
    )(haug, y, ln_g.reshape(1, d), ln_b.reshape(1, d))


def _moe_layer(haug, cls, w1, w3, w2, layer, ln_g, ln_b, alpha, blk=256):
    t = haug.shape[0]
    dest, counts = _rank_call(cls, blk)
    n_blk = t // blk + N_CLASSES
    cnt = counts[:, 0]
    padded = ((cnt + blk - 1) // blk) * blk
    pad_end = jnp.cumsum(padded)
    blk_start = jnp.arange(n_blk, dtype=jnp.int32) * blk
    blk_cls = jnp.minimum(jnp.sum(pad_end[None, :] <= blk_start[:, None], axis=1), N_CLASSES - 1)
    group = blk_cls // N_PAIRS
    pair = blk_cls % N_PAIRS
    blk_a = (group * EXPERTS_PER_GROUP + jnp.asarray(PAIR_A, jnp.int32)[pair]).astype(jnp.int32)
    blk_b = (group * EXPERTS_PER_GROUP + jnp.asarray(PAIR_B, jnp.int32)[pair]).astype(jnp.int32)
    valid_end = (pad_end - padded + cnt).astype(jnp.int32)
    n_used = (pad_end[N_CLASSES - 1:N_CLASSES] // blk).astype(jnp.int32)
    y = _experts_call(haug, dest.reshape(-1), blk_a, blk_b, valid_end, pad_end.astype(jnp.int32), n_used,
                      w1, w3, w2, layer, blk)
    return _combine_call(haug, y, ln_g, ln_b, alpha)


def kernel(x, positions, ln_in_g, ln_in_b, w_in, conv_w, sgu_ln_g, sgu_ln_b, w_s, b_s, lambda_q1, lambda_k1, lambda_q2, lambda_k2, diff_subln_g, w_gate, b_gate, w_branch, w_o, ln1_g, ln1_b, w_router, b_router, w1, w3, w2, ln2_g, ln2_b):
    bsz, seq, d = x.shape
    depth = w_in.shape[0]
    alpha = (2.0 * depth) ** 0.25
    t = bsz * seq
    cos_t, sin_t = _rope_tables(positions)
    w_in_b, w_gate_b, w_branch_b, w_o_b = (w.astype(BF16) for w in (w_in, w_gate, w_branch, w_o))
    w1_b, w3_b, w2_b = (w.astype(BF16) for w in (w1, w3, w2))
    h = x.reshape(t, d)
    for l in range(depth):
        lambda_init = 0.8 - 0.6 * math.exp(-0.3 * l)
        if l == 0:
            h, z = _proj_call(h, w_in_b, l, ln=(ln_in_g, ln_in_b))
        else:
            z = _proj_call(h, w_in_b, l)
        y_conv = _conv_call(z, conv_w[l], bsz, seq)
        y_ret = _ret_call(z, cos_t, sin_t, bsz, seq)
        y_sgu = _sgu_call(z, sgu_ln_g[l], sgu_ln_b[l], w_s[l], b_s[l])
        qt, kr, vt = _qkprep_call(z, cos_t, sin_t, bsz, seq)
        lam_params = jnp.stack([lambda_q1[l], lambda_k1[l], lambda_q2[l], lambda_k2[l]])
        y_diff = _flash_call(qt, kr, vt, lam_params, diff_subln_g[l], bsz, seq, lambda_init)
        haug, cls = _merge_call(h, (y_conv, y_ret, y_sgu, y_diff), w_gate_b, b_gate[l], w_branch_b, w_o_b, l,
                                ln1_g[l], ln1_b[l], w_router, b_router, alpha)
        h = _moe_layer(haug, cls, w1_b, w3_b, w2_b, l, ln2_g[l], ln2_b[l], alpha)
    return h.reshape(bsz, seq, d)
```

```python
import functools
import math

import jax
import jax.numpy as jnp
from jax import lax
from jax.experimental import pallas as pl
from jax.experimental.pallas import tpu as pltpu

D_MODEL = 1024
BRANCH_WIDTH = 512
N_BRANCH = 4
CONV_WIDTH = BRANCH_WIDTH
CONV_K = 3
RET_HEADS = 4
RET_DK = 64
RET_DV = 128
RET_CHUNK = 128
RET_THETA = 10000.0
SGU_GROUPS = 4
SGU_GROUP_DIM = BRANCH_WIDTH // SGU_GROUPS
SGU_CHUNK = 128
SGU_WIDTH = BRANCH_WIDTH
DIFF_HEADS = 4
DIFF_HEAD_DIM = 64
DIFF_V_DIM = 2 * DIFF_HEAD_DIM
ROPE_THETA = 500000.0
ROT_DIM = DIFF_HEAD_DIM // 4
CONV_COLS = 3 * CONV_WIDTH
RET_COLS = 2 * RET_HEADS * RET_DK + 2 * RET_HEADS * RET_DV
SGU_COLS = 2 * SGU_WIDTH
DIFF_COLS = 2 * DIFF_HEADS * 2 * DIFF_HEAD_DIM + DIFF_HEADS * DIFF_V_DIM
IN_COLS = CONV_COLS + RET_COLS + SGU_COLS + DIFF_COLS
N_EXPERTS = 16
N_GROUPS = 4
EXPERTS_PER_GROUP = N_EXPERTS // N_GROUPS
TOP_K = 2
D_EXPERT = 1024
LN_EPS = 1e-5
RMS_EPS = 1e-6

LANES = 128
CONV_OFF = 0
RET_OFF = CONV_COLS
SGU_OFF = RET_OFF + RET_COLS
DIFF_OFF = SGU_OFF + SGU_COLS

NEG_BIG = -1e30
VMEM_LIMIT = 56 * 1024 * 1024

BF16 = jnp.bfloat16
F32 = jnp.float32


def _cparams(sem):
    return pltpu.CompilerParams(dimension_semantics=sem, vmem_limit_bytes=VMEM_LIMIT)


def _layer_norm(xf, g, b):
    mu = jnp.mean(xf, axis=-1, keepdims=True)
    xc = xf - mu
    var = jnp.mean(xc * xc, axis=-1, keepdims=True)
    return xc * lax.rsqrt(var + LN_EPS) * g + b


def _dot(a, b):
    return jnp.dot(a, b, preferred_element_type=F32)


def _dot_nt(a, b):
    return lax.dot_general(a, b, (((1,), (1,)), ((), ())), preferred_element_type=F32)


def _dot_tn(a, b):
    return lax.dot_general(a, b, (((0,), (0,)), ((), ())), preferred_element_type=F32)


def _rope_kernel(pos_ref, freq_ref, sign_ref, cos_ref, sin_ref):
    ang = pos_ref[...] * freq_ref[...]
    cos_ref[...] = jnp.cos(ang)
    sin_ref[...] = jnp.sin(ang) * sign_ref[...]


def _rope_tables(positions):
    t = positions.size
    pos = positions.reshape(t, 1).astype(F32)
    half_r = RET_DK // 2
    fr = RET_THETA ** (-jnp.arange(half_r, dtype=F32) / half_r)
    half_d = ROT_DIM // 2
    fd = ROPE_THETA ** (-jnp.arange(half_d, dtype=F32) / half_d)
    zeros_d = jnp.zeros((DIFF_HEAD_DIM - ROT_DIM,), F32)
    freq = jnp.concatenate([fr, fr, fd, fd, zeros_d]).reshape(1, LANES)
    sign = jnp.concatenate([-jnp.ones((half_r,), F32), jnp.ones((half_r,), F32),
                            -jnp.ones((half_d,), F32), jnp.ones((half_d,), F32),
                            zeros_d]).reshape(1, LANES)
    tm = min(t, 2048)
    return pl.pallas_call(
        _rope_kernel,
        grid=(t // tm,),
        in_specs=[pl.BlockSpec((tm, 1), lambda i: (i, 0)),
                  pl.BlockSpec((1, LANES), lambda i: (0, 0)),
                  pl.BlockSpec((1, LANES), lambda i: (0, 0))],
        out_specs=[pl.BlockSpec((tm, LANES), lambda i: (i, 0)),
                   pl.BlockSpec((tm, LANES), lambda i: (i, 0))],
        out_shape=[jax.ShapeDtypeStruct((t, LANES), F32)] * 2,
        compiler_params=_cparams(("parallel",)),
        name="rope_tables",
    )(pos, freq, sign)


def _tile_lanes(x, reps):
    return jnp.concatenate([x] * reps, axis=1)


def _rotate_half_split(x, cos, sin_signed, group, half):
    w = x.shape[1]
    lane = lax.broadcasted_iota(jnp.int32, x.shape, 1) % group
    partner = jnp.where(lane < half, pltpu.roll(x, w - half, axis=1), pltpu.roll(x, half, axis=1))
    return x * cos + partner * sin_signed


PROJ_CHUNK = 512


def _project(hb, w_ref, z_ref):
    for n0 in range(0, IN_COLS, PROJ_CHUNK):
        z_ref[:, n0:n0 + PROJ_CHUNK] = _dot(hb, w_ref[:, n0:n0 + PROJ_CHUNK]).astype(BF16)


def _proj_kernel(h_ref, w_ref, z_ref):
    _project(h_ref[...].astype(BF16), w_ref, z_ref)


def _ln_proj_kernel(x_ref, g_ref, b_ref, w_ref, h_ref, z_ref):
    h = _layer_norm(x_ref[...], g_ref[...], b_ref[...])
    h_ref[...] = h
    _project(h.astype(BF16), w_ref, z_ref)


def _proj_call(h, w_all, layer, ln=None, tm=512):
    t, d = h.shape
    row = pl.BlockSpec((tm, d), lambda i: (i, 0))
    vec = pl.BlockSpec((1, d), lambda i: (0, 0))
    wspec = pl.BlockSpec((None, d, IN_COLS), lambda i: (layer, 0, 0), pipeline_mode=pl.Buffered(1))
    zspec = pl.BlockSpec((tm, IN_COLS), lambda i: (i, 0))
    zshape = jax.ShapeDtypeStruct((t, IN_COLS), BF16)
    if ln is None:
        return pl.pallas_call(
            _proj_kernel, grid=(t // tm,), in_specs=[row, wspec], out_specs=zspec, out_shape=zshape,
            compiler_params=_cparams(("parallel",)), name="proj_in",
        )(h, w_all)
    return pl.pallas_call(
        _ln_proj_kernel, grid=(t // tm,), in_specs=[row, vec, vec, wspec], out_specs=[row, zspec],
        out_shape=[jax.ShapeDtypeStruct((t, d), F32), zshape],
        compiler_params=_cparams(("parallel",)), name="ln_proj_in",
    )(h, ln[0].reshape(1, d), ln[1].reshape(1, d), w_all)


CONV_HALO = 16


def _conv_kernel(b_ref, c_ref, u_ref, ch_ref, uh_ref, w_ref, o_ref):
    ts = c_ref.shape[0]
    first = pl.program_id(1) == 0
    cu = c_ref[...].astype(F32) * u_ref[...].astype(F32)
    halo = ch_ref[...].astype(F32) * uh_ref[...].astype(F32)
    halo = jnp.where(first, 0.0, halo)
    ext = jnp.concatenate([halo, cu], axis=0)
    n = ext.shape[0]
    prev1 = pltpu.roll(ext, 1, axis=0)[CONV_HALO:n]
    prev2 = pltpu.roll(ext, 2, axis=0)[CONV_HALO:n]
    w = w_ref[...]
    y = prev2 * w[0:1, :] + prev1 * w[1:2, :] + cu * w[2:3, :]
    o_ref[...] = (b_ref[...].astype(F32) * y).astype(BF16)


def _conv_call(z, conv_w, bsz, seq, ts=1024):
    t = bsz * seq
    ts = min(ts, seq)
    nt = seq // ts
    wb = CONV_WIDTH
    hb = ts // CONV_HALO
    col = lambda k: (lambda b, i: (b * nt + i, CONV_OFF // wb + k))
    halo = lambda k: (lambda b, i: (jnp.maximum((b * nt + i) * hb - 1, 0), CONV_OFF // wb + k))
    return pl.pallas_call(
        _conv_kernel,
        grid=(bsz, nt),
        in_specs=[pl.BlockSpec((ts, wb), col(0)),
                  pl.BlockSpec((ts, wb), col(1)),
                  pl.BlockSpec((ts, wb), col(2)),
                  pl.BlockSpec((CONV_HALO, wb), halo(1)),
                  pl.BlockSpec((CONV_HALO, wb), halo(2)),
                  pl.BlockSpec((CONV_K, wb), lambda b, i: (0, 0))],
        out_specs=pl.BlockSpec((ts, wb), lambda b, i: (b * nt + i, 0)),
        out_shape=jax.ShapeDtypeStruct((t, wb), BF16),
        compiler_params=_cparams(("parallel", "parallel")),
        name="conv_mixer",
    )(z, z, z, z, z, conv_w.reshape(CONV_K, wb))


def _sgu_kernel(u_ref, v_ref, g_ref, b_ref, ws_ref, bias_ref, o_ref):
    ts = u_ref.shape[0]
    c = SGU_CHUNK
    v = _layer_norm(v_ref[...].astype(F32), g_ref[...], b_ref[...]).astype(BF16)
    row = lax.broadcasted_iota(jnp.int32, (c, c), 0)
    colm = lax.broadcasted_iota(jnp.int32, (c, c), 1)
    bias = bias_ref[...]
    for g in range(SGU_GROUPS):
        w = jnp.where(row >= colm, ws_ref[g], 0.0).astype(BF16)
        lo = g * SGU_GROUP_DIM
        for n in range(ts // c):
            s = _dot(w, v[n * c:(n + 1) * c, lo:lo + SGU_GROUP_DIM]) + bias[:, lo:lo + SGU_GROUP_DIM]
            u = u_ref[n * c:(n + 1) * c, lo:lo + SGU_GROUP_DIM].astype(F32)
            o_ref[n * c:(n + 1) * c, lo:lo + SGU_GROUP_DIM] = (u * s).astype(BF16)


def _sgu_call(z, ln_g, ln_b, w_s, b_s, ts=1024):
    t = z.shape[0]
    ts = min(ts, t)
    wb = SGU_WIDTH
    bias = jnp.repeat(b_s.T, SGU_GROUP_DIM, axis=1)
    return pl.pallas_call(
        _sgu_kernel,
        grid=(t // ts,),
        in_specs=[pl.BlockSpec((ts, wb), lambda i: (i, SGU_OFF // wb)),
                  pl.BlockSpec((ts, wb), lambda i: (i, SGU_OFF // wb + 1)),
                  pl.BlockSpec((1, wb), lambda i: (0, 0)),
                  pl.BlockSpec((1, wb), lambda i: (0, 0)),
                  pl.BlockSpec((SGU_GROUPS, SGU_CHUNK, SGU_CHUNK), lambda i: (0, 0, 0)),
                  pl.BlockSpec((SGU_CHUNK, wb), lambda i: (0, 0))],
        out_specs=pl.BlockSpec((ts, wb), lambda i: (i, 0)),
        out_shape=jax.ShapeDtypeStruct((t, wb), BF16),
        compiler_params=_cparams(("parallel",)),
        name="sgu_mixer",
    )(z, z, ln_g.reshape(1, wb), ln_b.reshape(1, wb), w_s, bias)


def _ret_tables():
    c = RET_CHUNK
    log_gamma = jnp.log1p(-jnp.exp2(-5.0 - jnp.arange(RET_HEADS, dtype=F32)))
    idx = jnp.arange(c, dtype=F32)
    rel = idx[:, None] - idx[None, :]
    decay = jnp.where(rel >= 0, jnp.exp(jnp.maximum(rel, 0.0)[None] * log_gamma[:, None, None]), 0.0)
    k_decay = jnp.exp((c - 1 - idx)[:, None] * log_gamma[None, :])
    q_decay = jnp.exp((idx + 1.0)[:, None] * log_gamma[None, :])
    chunk_decay = jnp.exp(c * log_gamma)
    hk = RET_HEADS * RET_DK
    kd = jnp.repeat(k_decay, RET_DK, axis=1) * (RET_DK ** -0.5)
    qd = jnp.repeat(q_decay, RET_DK, axis=1)
    cd = jnp.broadcast_to(jnp.repeat(chunk_decay, RET_DV)[None, :], (8, RET_HEADS * RET_DV))
    del hk
    return decay, kd, qd, cd


def _ret_kernel(q_ref, k_ref, v_ref, g_ref, cos_ref, sin_ref, decay_ref, kd_ref, qd_ref, cd_ref,
                o_ref, state_ref):
    ts = q_ref.shape[0]
    c = RET_CHUNK
    hk = RET_HEADS * RET_DK

    @pl.when(pl.program_id(1) == 0)
    def _():
        state_ref[...] = jnp.zeros_like(state_ref)

    cos = _tile_lanes(cos_ref[:, 0:RET_DK], RET_HEADS)
    sin = _tile_lanes(sin_ref[:, 0:RET_DK], RET_HEADS)
    q = _rotate_half_split(q_ref[...].astype(F32), cos, sin, RET_DK, RET_DK // 2)
    k = _rotate_half_split(k_ref[...].astype(F32), cos, sin, RET_DK, RET_DK // 2)
    del hk
    for n in range(ts // c):
        r0 = n * c
        qn = q[r0:r0 + c]
        kn = k[r0:r0 + c]
        qb = qn.astype(BF16)
        kb = (kn * (RET_DK ** -0.5)).astype(BF16)
        qdb = (qn * qd_ref[...]).astype(BF16)
        kdb = (kn * kd_ref[...]).astype(BF16)
        for h in range(RET_HEADS):
            ks = slice(h * RET_DK, (h + 1) * RET_DK)
            vs = slice(h * RET_DV, (h + 1) * RET_DV)
            vb = v_ref[r0:r0 + c, vs]
            scores = _dot_nt(qb[:, ks], kb[:, ks]) * decay_ref[h]
            inner = _dot(scores.astype(BF16), vb)
            state = state_ref[h]
            cross = _dot(qdb[:, ks], state.astype(BF16))
            kv = _dot_tn(kdb[:, ks], vb)
            state_ref[h] = state * cd_ref[0:1, vs] + kv
            o = inner + cross
            o = o * lax.rsqrt(jnp.mean(o * o, axis=-1, keepdims=True) + RMS_EPS)
            gate = g_ref[r0:r0 + c, vs].astype(F32)
            gate = gate * jax.nn.sigmoid(gate)
            o_ref[r0:r0 + c, vs] = (gate * o).astype(BF16)


def _ret_call(z, cos_t, sin_t, bsz, seq, ts=512):
    t = bsz * seq
    ts = min(ts, seq)
    nt = seq // ts
    decay, kd, qd, cd = _ret_tables()
    hk = RET_HEADS * RET_DK
    hv = RET_HEADS * RET_DV
    row = lambda b, i: b * nt + i
    return pl.pallas_call(
        _ret_kernel,
        grid=(bsz, nt),
        in_specs=[pl.BlockSpec((ts, hk), lambda b, i: (row(b, i), RET_OFF // hk)),
                  pl.BlockSpec((ts, hk), lambda b, i: (row(b, i), RET_OFF // hk + 1)),
                  pl.BlockSpec((ts, hv), lambda b, i: (row(b, i), (RET_OFF + 2 * hk) // hv)),
                  pl.BlockSpec((ts, hv), lambda b, i: (row(b, i), (RET_OFF + 2 * hk) // hv + 1)),
                  pl.BlockSpec((ts, LANES), lambda b, i: (row(b, i), 0)),
                  pl.BlockSpec((ts, LANES), lambda b, i: (row(b, i), 0)),
                  pl.BlockSpec((RET_HEADS, RET_CHUNK, RET_CHUNK), lambda b, i: (0, 0, 0)),
                  pl.BlockSpec((RET_CHUNK, hk), lambda b, i: (0, 0)),
                  pl.BlockSpec((RET_CHUNK, hk), lambda b, i: (0, 0)),
                  pl.BlockSpec((8, hv), lambda b, i: (0, 0))],
        out_specs=pl.BlockSpec((ts, hv), lambda b, i: (row(b, i), 0)),
        out_shape=jax.ShapeDtypeStruct((t, hv), BF16),
        scratch_shapes=[pltpu.VMEM((RET_HEADS, RET_DK, RET_DV), F32)],
        compiler_params=_cparams(("parallel", "arbitrary")),
        name="ret_mixer",
    )(z, z, z, z, cos_t, sin_t, decay, kd, qd, cd)


LOG2E = 1.4426950408889634


def _qkprep_kernel(q_ref, k_ref, v_ref, cos_ref, sin_ref, qt_ref, ko_ref, vt_ref):
    reps = q_ref.shape[1] // DIFF_HEAD_DIM
    cos = _tile_lanes(cos_ref[:, DIFF_HEAD_DIM:2 * DIFF_HEAD_DIM], reps)
    sin = _tile_lanes(sin_ref[:, DIFF_HEAD_DIM:2 * DIFF_HEAD_DIM], reps)
    q = _rotate_half_split(q_ref[...].astype(F32), cos, sin, DIFF_HEAD_DIM, ROT_DIM // 2)
    k = _rotate_half_split(k_ref[...].astype(F32), cos, sin, DIFF_HEAD_DIM, ROT_DIM // 2)
    qt_ref[...] = (q * (DIFF_HEAD_DIM ** -0.5 * LOG2E)).T.astype(BF16)
    ko_ref[...] = k.astype(BF16)
    vt_ref[...] = v_ref[...].astype(F32).T.astype(BF16)


def _qkprep_call(z, cos_t, sin_t, bsz, seq, tm=1024):
    t = bsz * seq
    tm = min(tm, seq)
    nt = seq // tm
    hq = DIFF_HEADS * 2 * DIFF_HEAD_DIM
    row = lambda b, i: b * nt + i
    return pl.pallas_call(
        _qkprep_kernel,
        grid=(bsz, nt),
        in_specs=[pl.BlockSpec((tm, hq), lambda b, i: (row(b, i), DIFF_OFF // hq)),
                  pl.BlockSpec((tm, hq), lambda b, i: (row(b, i), DIFF_OFF // hq + 1)),
                  pl.BlockSpec((tm, hq), lambda b, i: (row(b, i), DIFF_OFF // hq + 2)),
                  pl.BlockSpec((tm, LANES), lambda b, i: (row(b, i), 0)),
                  pl.BlockSpec((tm, LANES), lambda b, i: (row(b, i), 0))],
        out_specs=[pl.BlockSpec((None, hq, tm), lambda b, i: (b, 0, i)),
                   pl.BlockSpec((tm, hq), lambda b, i: (row(b, i), 0)),
                   pl.BlockSpec((None, hq, tm), lambda b, i: (b, 0, i))],
        out_shape=[jax.ShapeDtypeStruct((bsz, hq, seq), BF16),
                   jax.ShapeDtypeStruct((t, hq), BF16),
                   jax.ShapeDtypeStruct((bsz, hq, seq), BF16)],
        compiler_params=_cparams(("parallel", "parallel")),
        name="diff_qkprep",
    )(z, z, z, cos_t, sin_t)


SUM_ROWS = 16
KB_PER_STEP = 2


def _flash_kernel(qi_ref, ki_ref, qt_ref, k_ref, vt_ref, lam_ref, g_ref, o_ref,
                  qd_ref, m_ref, acc_ref, *, tb, cq, lambda_init):
    p = pl.program_id(2)
    qi = qi_ref[p]
    kp = ki_ref[p]
    hd = DIFF_HEAD_DIM
    dv = DIFF_V_DIM

    @pl.when(kp == 0)
    def _():
        qd_ref[...] = jnp.zeros_like(qd_ref)
        qd_ref[0:hd, 0:tb] = qt_ref[0:hd, :]
        qd_ref[hd:2 * hd, tb:2 * tb] = qt_ref[hd:2 * hd, :]
        m_ref[...] = jnp.full_like(m_ref, NEG_BIG)
        acc_ref[...] = jnp.zeros_like(acc_ref)

    def step(diagonal, half):
        k = k_ref[half * tb:(half + 1) * tb, :]
        vta = jnp.concatenate([vt_ref[:, half * tb:(half + 1) * tb], jnp.ones((SUM_ROWS, tb), BF16)],
                              axis=0)
        nc = 2 * tb // cq
        cols = [slice(c * cq, (c + 1) * cq) for c in range(nc)]
        nkey = [((c * cq) % tb + cq) if diagonal else tb for c in range(nc)]
        m_prev = m_ref[...]
        s = [_dot(k[0:nkey[c]], qd_ref[:, cols[c]]) for c in range(nc)]
        if diagonal:
            for c in range(nc):
                key = lax.broadcasted_iota(jnp.int32, (nkey[c], cq), 0)
                qpos = lax.broadcasted_iota(jnp.int32, (nkey[c], cq), 1) + (c * cq) % tb
                s[c] = jnp.where(key <= qpos, s[c], NEG_BIG)
        m_new = [jnp.maximum(m_prev[:, cols[c]], jnp.max(s[c], axis=0, keepdims=True)) for c in range(nc)]
        pexp = [jnp.exp2(s[c] - m_new[c]).astype(BF16) for c in range(nc)]
        alpha = [jnp.exp2(m_prev[:, cols[c]] - m_new[c]) for c in range(nc)]
        for c in range(nc):
            cs = cols[c]
            m_ref[:, cs] = m_new[c]
            acc_ref[:, cs] = alpha[c] * acc_ref[:, cs] + _dot(vta[:, 0:nkey[c]], pexp[c])

    def finalize():
        lam_p = lam_ref[...]
        lam = (jnp.exp(jnp.sum(lam_p[0:1] * lam_p[1:2], axis=-1, keepdims=True))
               - jnp.exp(jnp.sum(lam_p[2:3] * lam_p[3:4], axis=-1, keepdims=True)) + lambda_init)
        o = acc_ref[0:dv, :] / acc_ref[dv:dv + 1, :]
        o = (o[:, 0:tb] - lam * o[:, tb:2 * tb]).T
        o = o * lax.rsqrt(jnp.mean(o * o, axis=-1, keepdims=True) + RMS_EPS)
        o_ref[...] = (o * g_ref[...] * (1.0 - lambda_init)).astype(BF16)

    for half in range(KB_PER_STEP):
        kb = KB_PER_STEP * kp + half

        @pl.when(kb < qi)
        def _(half=half):
            step(False, half)

        @pl.when(kb == qi)
        def _(half=half):
            step(True, half)
            finalize()


def _flash_call(qt, kr, vt, lam_params, subln_g, bsz, seq, lambda_init, tb=1024, cq=256):
    t = bsz * seq
    kg = KB_PER_STEP
    tb = min(tb, seq // kg)
    cq = min(cq, tb)
    nb = seq // tb
    assert nb % kg == 0, "key blocks are processed in groups of KB_PER_STEP"
    qi_list, ki_list = [], []
    for qi in range(nb):
        for kp in range(qi // kg + 1):
            qi_list.append(qi)
            ki_list.append(kp)
    qi_arr = jnp.asarray(qi_list, jnp.int32)
    ki_arr = jnp.asarray(ki_list, jnp.int32)
    hd = 2 * DIFF_HEAD_DIM
    grid_spec = pltpu.PrefetchScalarGridSpec(
        num_scalar_prefetch=2,
        grid=(bsz, DIFF_HEADS, len(qi_list)),
        in_specs=[pl.BlockSpec((None, hd, tb), lambda b, h, p, qi, ki: (b, h, qi[p])),
                  pl.BlockSpec((kg * tb, hd), lambda b, h, p, qi, ki: (b * (nb // kg) + ki[p], h)),
                  pl.BlockSpec((None, DIFF_V_DIM, kg * tb), lambda b, h, p, qi, ki: (b, h, ki[p])),
                  pl.BlockSpec((4, DIFF_HEAD_DIM), lambda b, h, p, qi, ki: (0, 0)),
                  pl.BlockSpec((1, DIFF_V_DIM), lambda b, h, p, qi, ki: (0, 0))],
        out_specs=pl.BlockSpec((tb, DIFF_V_DIM), lambda b, h, p, qi, ki: (b * nb + qi[p], h)),
        scratch_shapes=[pltpu.VMEM((hd, 2 * tb), BF16),
                        pltpu.VMEM((1, 2 * tb), F32),
                        pltpu.VMEM((DIFF_V_DIM + SUM_ROWS, 2 * tb), F32)],
    )
    return pl.pallas_call(
        functools.partial(_flash_kernel, tb=tb, cq=cq, lambda_init=lambda_init),
        grid_spec=grid_spec,
        out_shape=jax.ShapeDtypeStruct((t, DIFF_HEADS * DIFF_V_DIM), BF16),
        compiler_params=_cparams(("parallel", "parallel", "arbitrary")),
        name="diff_flash",
    )(qi_arr, ki_arr, qt, kr, vt, lam_params, subln_g.reshape(1, DIFF_V_DIM))


def _merge_kernel(h_ref, y0_ref, y1_ref, y2_ref, y3_ref, wg_ref, bg_ref, wb_ref, wo_ref,
                  g_ref, b_ref, wr_ref, br_ref, o_ref, cls_ref, *, alpha):
    d = h_ref.shape[1]
    h = h_ref[...]
    hb = h.astype(BF16)
    merged = None
    for g, y_ref in enumerate((y0_ref, y1_ref, y2_ref, y3_ref)):
        gate = jax.nn.sigmoid(_dot(hb, wg_ref[g]) + bg_ref[g:g + 1, :])
        term = gate * _dot(y_ref[...], wb_ref[g])
        merged = term if merged is None else merged + term
    t = _dot(merged.astype(BF16), wo_ref[...])
    out = _layer_norm(alpha * h + t, g_ref[...], b_ref[...])
    o_ref[:, 0:d] = out
    cls, ga, gb = _route(out, wr_ref[...], br_ref[...])
    cls_ref[...] = cls
    g8 = jnp.concatenate([ga, gb, jnp.zeros((6, ga.shape[1]), F32)], axis=0)
    sel = (lax.broadcasted_iota(jnp.int32, (8, LANES), 0)
           == lax.broadcasted_iota(jnp.int32, (8, LANES), 1)).astype(F32)
    o_ref[:, d:d + LANES] = lax.dot_general(g8, sel, (((0,), (0,)), ((), ())),
                                            preferred_element_type=F32,
                                            precision=lax.Precision.HIGHEST)


def _merge_call(h, ys, wg, bg, wb, wo, layer, ln_g, ln_b, w_router, b_router, alpha, tm=512):
    t, d = h.shape
    tm = min(tm, t)
    w = BRANCH_WIDTH
    const = dict(pipeline_mode=pl.Buffered(1))
    return pl.pallas_call(
        functools.partial(_merge_kernel, alpha=alpha),
        grid=(t // tm,),
        in_specs=[pl.BlockSpec((tm, d), lambda i: (i, 0))]
        + [pl.BlockSpec((tm, w), lambda i: (i, 0))] * N_BRANCH
        + [pl.BlockSpec((None, N_BRANCH, d, d), lambda i: (layer, 0, 0, 0), **const),
           pl.BlockSpec((N_BRANCH, d), lambda i: (0, 0)),
           pl.BlockSpec((None, N_BRANCH, w, d), lambda i: (layer, 0, 0, 0), **const),
           pl.BlockSpec((None, d, d), lambda i: (layer, 0, 0), **const),
           pl.BlockSpec((1, d), lambda i: (0, 0)),
           pl.BlockSpec((1, d), lambda i: (0, 0)),
           pl.BlockSpec((N_EXPERTS, d), lambda i: (0, 0)),
           pl.BlockSpec((N_EXPERTS, 1), lambda i: (0, 0))],
        out_specs=[pl.BlockSpec((tm, d + LANES), lambda i: (i, 0)),
                   pl.BlockSpec((1, tm), lambda i: (0, i))],
        out_shape=[jax.ShapeDtypeStruct((t, d + LANES), F32),
                   jax.ShapeDtypeStruct((1, t), jnp.int32)],
        compiler_params=_cparams(("parallel",)),
        name="merge",
    )(h, *ys, wg, bg, wb, wo, ln_g.reshape(1, d), ln_b.reshape(1, d),
      w_router.T, b_router.reshape(N_EXPERTS, 1))


N_PAIRS = 6
PAIR_A = (0, 0, 0, 1, 1, 3)
PAIR_B = (1, 2, 3, 3, 2, 2)
N_CLASSES = N_GROUPS * N_PAIRS
CLASS_ROWS = 32


def _route(h, wr, br):
    logits = lax.dot_general(wr, h, (((1,), (1,)), ((), ())),
                             preferred_element_type=F32, precision=lax.Precision.HIGHEST)
    mx = jnp.max(logits, axis=0, keepdims=True)
    e = jnp.exp(logits - mx)
    scores = e / jnp.sum(e, axis=0, keepdims=True)
    sel = scores + br
    epg = EXPERTS_PER_GROUP
    rows = [sel[i:i + 1, :] for i in range(N_EXPERTS)]
    srows = [scores[i:i + 1, :] for i in range(N_EXPERTS)]
    best_score = None
    best = None
    for g in range(N_GROUPS):
        r = rows[g * epg:(g + 1) * epg]
        gs = None
        for a in range(epg):
            for b in range(a + 1, epg):
                pair = r[a] + r[b]
                gs = pair if gs is None else jnp.maximum(gs, pair)
        if best is None:
            best_score, best = gs, jnp.zeros(gs.shape, jnp.int32)
        else:
            take = gs > best_score
            best_score = jnp.where(take, gs, best_score)
            best = jnp.where(take, g, best)
    cand, cscore = [], []
    for j in range(epg):
        c = rows[j]
        s = srows[j]
        for g in range(1, N_GROUPS):
            c = jnp.where(best == g, rows[g * epg + j], c)
            s = jnp.where(best == g, srows[g * epg + j], s)
        cand.append(c)
        cscore.append(s)
    v1, i1, s1 = cand[0], jnp.zeros(best.shape, jnp.int32), cscore[0]
    for j in range(1, epg):
        take = cand[j] > v1
        v1 = jnp.where(take, cand[j], v1)
        i1 = jnp.where(take, j, i1)
        s1 = jnp.where(take, cscore[j], s1)
    v2 = jnp.full(v1.shape, -jnp.inf, F32)
    i2 = jnp.zeros(best.shape, jnp.int32)
    s2 = jnp.zeros(v1.shape, F32)
    for j in range(epg):
        take = jnp.logical_and(i1 != j, cand[j] > v2)
        v2 = jnp.where(take, cand[j], v2)
        i2 = jnp.where(take, j, i2)
        s2 = jnp.where(take, cscore[j], s2)
    denom = s1 + s2
    g1 = s1 / denom
    g2 = s2 / denom
    lo = jnp.minimum(i1, i2)
    hi = jnp.maximum(i1, i2)
    pair = jnp.where(lo == 0, hi - 1, jnp.where(lo == 1, jnp.where(hi == 3, 3, 4), 5))
    a_loc = jnp.where(pair < 3, 0, jnp.where(pair < 5, 1, 3))
    first_is_a = i1 == a_loc
    return (best * N_PAIRS + pair, jnp.where(first_is_a, g1, g2), jnp.where(first_is_a, g2, g1))


def _rank_kernel(cls_ref, dest_ref, cnt_ref, run_ref, start_ref, *, blk):
    phase = pl.program_id(0)
    i = pl.program_id(1)
    tm = cls_ref.shape[1]
    onehot = lax.broadcasted_iota(jnp.int32, (CLASS_ROWS, tm), 0) == cls_ref[...]
    f = jnp.where(onehot, 1.0, 0.0)
    tot = jnp.sum(f, axis=1, keepdims=True)

    @pl.when(jnp.logical_and(phase == 0, i == 0))
    def _():
        run_ref[...] = jnp.zeros_like(run_ref)

    @pl.when(phase == 0)
    def _():
        run_ref[...] = run_ref[...] + tot

    @pl.when(jnp.logical_and(phase == 1, i == 0))
    def _():
        cnt = run_ref[...]
        cnt_ref[...] = cnt.astype(jnp.int32)
        padded = jnp.ceil(cnt * (1.0 / blk)) * blk
        acc = jnp.zeros((1, LANES), F32)
        for c in range(CLASS_ROWS):
            start_ref[c:c + 1, :] = acc
            acc = acc + padded[c:c + 1, :]
        run_ref[...] = jnp.zeros_like(run_ref)

    @pl.when(phase == 1)
    def _():
        s_idx = lax.broadcasted_iota(jnp.int32, (tm, tm), 0)
        t_idx = lax.broadcasted_iota(jnp.int32, (tm, tm), 1)
        tri = jnp.where(s_idx < t_idx, 1.0, 0.0).astype(BF16)
        before = _dot(f.astype(BF16), tri)
        base = start_ref[:, 0:1] + run_ref[:, 0:1]
        dest = jnp.sum(jnp.where(onehot, base + before, 0.0), axis=0, keepdims=True)
        dest_ref[...] = dest.astype(jnp.int32)
        run_ref[...] = run_ref[...] + tot


def _rank_call(cls, blk, tm=1024):
    t = cls.shape[1]
    tm = min(tm, t)
    return pl.pallas_call(
        functools.partial(_rank_kernel, blk=blk),
        grid=(2, t // tm),
        in_specs=[pl.BlockSpec((1, tm), lambda p, i: (0, i))],
        out_specs=[pl.BlockSpec((1, tm), lambda p, i: (0, i * p)),
                   pl.BlockSpec((CLASS_ROWS, LANES), lambda p, i: (0, 0))],
        out_shape=[jax.ShapeDtypeStruct((1, t), jnp.int32),
                   jax.ShapeDtypeStruct((CLASS_ROWS, LANES), jnp.int32)],
        scratch_shapes=[pltpu.VMEM((CLASS_ROWS, LANES), F32),
                        pltpu.VMEM((CLASS_ROWS, LANES), F32)],
        compiler_params=_cparams(("arbitrary", "arbitrary")),
        name="moe_rank",
    )(cls)


def _swiglu(xb, wb1, wb3, wb2):
    a = _dot(xb, wb1[...])
    b = _dot(xb, wb3[...])
    return _dot((a * jax.nn.sigmoid(a) * b).astype(BF16), wb2[...])


def _experts_kernel(dest_ref, ba_ref, bb_ref, ve_ref, pe_ref, nb_ref, h_ref,
                    w1a, w3a, w2a, w1b, w3b, w2b, y_ref,
                    inv_ref, xbuf0, xbuf1, obuf0, obuf1, sa1, sa3, sa2, sb1, sb3, sb2, gsem, ssem,
                    *, blk, t, d, n_blk):
    i = pl.program_id(0)
    nb = nb_ref[0]
    xbuf = (xbuf0, xbuf1)
    obuf = (obuf0, obuf1)

    prev = jnp.maximum(i - 1, 0)
    for be_ref, srcs, dsts in ((ba_ref, (w1a, w3a, w2a), (sa1, sa3, sa2)),
                               (bb_ref, (w1b, w3b, w2b), (sb1, sb3, sb2))):
        @pl.when(jnp.logical_and(i < nb, jnp.logical_or(i == 0, be_ref[i] != be_ref[prev])))
        def _(srcs=srcs, dsts=dsts):
            for src, dst in zip(srcs, dsts):
                dst[...] = src[0]

    def gather(block, s):
        base = block * blk
        for j in range(blk):
            tok = inv_ref[base + j] & (t - 1)
            pltpu.make_async_copy(h_ref.at[pl.ds(tok, 1), :], xbuf[s].at[pl.ds(j, 1), :],
                                  gsem.at[s]).start(priority=j % 2)

    def wait_gather(s):
        for j in range(blk):
            pltpu.make_async_copy(h_ref.at[pl.ds(0, 1), :], xbuf[s].at[pl.ds(j, 1), :],
                                  gsem.at[s]).wait()

    def scatter(block, s):
        base = block * blk
        for j in range(blk):
            pltpu.make_async_copy(obuf[s].at[pl.ds(j, 1), :], y_ref.at[pl.ds(inv_ref[base + j], 1), :],
                                  ssem.at[s]).start(priority=j % 2)

    def wait_scatter(s):
        for j in range(blk):
            pltpu.make_async_copy(obuf[s].at[pl.ds(j, 1), :], y_ref.at[pl.ds(0, 1), :],
                                  ssem.at[s]).wait()

    @pl.when(i == 0)
    def _():
        def fill(tok, carry):
            inv_ref[dest_ref[tok]] = tok
            return carry

        lax.fori_loop(0, t, fill, 0, unroll=8)

        def pad(r, carry):
            inv_ref[r] = t + (r & (2 * blk - 1))
            return carry

        for c in range(N_CLASSES):
            lax.fori_loop(ve_ref[c], pe_ref[c], pad, 0)

        def dummy(j, carry):
            inv_ref[n_blk * blk + j] = t + 2 * blk + j
            return carry

        lax.fori_loop(0, blk, dummy, 0)
        obuf1[...] = jnp.zeros_like(obuf1)
        for q in range(3):
            zero_dump = pltpu.make_async_copy(obuf1, y_ref.at[pl.ds(t + q * blk, blk), :], ssem.at[0])
            zero_dump.start()
            zero_dump.wait()
        gather(0, 0)

    def block_step(s):
        wait_gather(s)

        @pl.when(i > 0)
        def _():
            wait_scatter(s)

        gather(jnp.minimum(i + 1, nb - 1), 1 - s)
        scatter(jnp.where(i == 0, n_blk, i - 1), 1 - s)
        x = xbuf[s][...]
        xb = x[:, 0:d].astype(BF16)
        obuf[s][...] = (x[:, d:d + 1] * _swiglu(xb, sa1, sa3, sa2)
                        + x[:, d + 1:d + 2] * _swiglu(xb, sb1, sb3, sb2))

        @pl.when(i == nb - 1)
        def _():
            wait_gather(1 - s)
            wait_scatter(1 - s)
            scatter(i, s)
            wait_scatter(s)

    for s in range(2):
        @pl.when(jnp.logical_and(i < nb, i % 2 == s))
        def _(s=s):
            block_step(s)


def _experts_call(haug, dest, blk_a, blk_b, valid_end, pad_end, n_used, w1, w3, w2, layer, blk):
    t, da = haug.shape
    d = da - LANES
    de = w1.shape[3]
    n_blk = blk_a.shape[0]
    assert t & (t - 1) == 0 and blk & (blk - 1) == 0, "token count and block size must be powers of two"
    amap = lambda i, dest, ba, bb, ve, pe, nb: (layer, ba[i], 0, 0)
    bmap = lambda i, dest, ba, bb, ve, pe, nb: (layer, bb[i], 0, 0)
    up, down = (None, 1, d, de), (None, 1, de, d)
    wspecs = [pl.BlockSpec(up, amap), pl.BlockSpec(up, amap), pl.BlockSpec(down, amap),
              pl.BlockSpec(up, bmap), pl.BlockSpec(up, bmap), pl.BlockSpec(down, bmap)]
    wscratch = [pltpu.VMEM((d, de), BF16), pltpu.VMEM((d, de), BF16), pltpu.VMEM((de, d), BF16)] * 2
    return pl.pallas_call(
        functools.partial(_experts_kernel, blk=blk, t=t, d=d, n_blk=n_blk),
        grid_spec=pltpu.PrefetchScalarGridSpec(
            num_scalar_prefetch=6,
            grid=(n_blk,),
            in_specs=[pl.BlockSpec(memory_space=pl.ANY)] + wspecs,
            out_specs=pl.BlockSpec(memory_space=pl.ANY),
            scratch_shapes=[pltpu.SMEM(((n_blk + 1) * blk,), jnp.int32),
                            pltpu.VMEM((blk, da), F32),
                            pltpu.VMEM((blk, da), F32),
                            pltpu.VMEM((blk, d), F32),
                            pltpu.VMEM((blk, d), F32)] + wscratch
            + [pltpu.SemaphoreType.DMA((2,)),
               pltpu.SemaphoreType.DMA((2,))],
        ),
        out_shape=jax.ShapeDtypeStruct((t + 3 * blk, d), F32),
        compiler_params=_cparams(("arbitrary",)),
        name="moe_experts",
    )(dest, blk_a, blk_b, valid_end, pad_end, n_used, haug, w1, w3, w2, w1, w3, w2)


def _combine_kernel(h_ref, y_ref, g_ref, b_ref, o_ref, *, alpha):
    o_ref[...] = _layer_norm(alpha * h_ref[...] + y_ref[...], g_ref[...], b_ref[...])


def _combine_call(haug, y, ln_g, ln_b, alpha, tm=512):
    t = haug.shape[0]
    d = y.shape[1]
    tm = min(tm, t)
    return pl.pallas_call(
        functools.partial(_combine_kernel, alpha=alpha),
        grid=(t // tm,),
        in_specs=[pl.BlockSpec((tm, d), lambda i: (i, 0)),
                  pl.BlockSpec((tm, d), lambda i: (i, 0)),
                  pl.BlockSpec((1, d), lambda i: (0, 0)),
                  pl.BlockSpec((1, d), lambda i: (0, 0))],
        out_specs=pl.BlockSpec((tm, d), lambda i: (i, 0)),
        out_shape=jax.ShapeDtypeStruct((t, d), F32),
        compiler_params=_cparams(("parallel",)),
        name="moe_combine",
    )(haug, y, ln_g.reshape(1, d), ln_b.reshape(1, d))


def _moe_layer(haug, cls, w1, w3, w2, layer, ln_g, ln_b, alpha, blk=256):
    t = haug.shape[0]
    dest, counts = _rank_call(cls, blk)
    n_blk = t // blk + N_CLASSES
    cnt = counts[:, 0]
    padded = ((cnt + blk - 1) // blk) * blk
    pad_end = jnp.cumsum(padded)
    blk_start = jnp.arange(n_blk, dtype=jnp.int32) * blk
    blk_cls = jnp.minimum(jnp.sum(pad_end[None, :] <= blk_start[:, None], axis=1), N_CLASSES - 1)
    group = blk_cls // N_PAIRS
    pair = blk_cls % N_PAIRS
    blk_a = (group * EXPERTS_PER_GROUP + jnp.asarray(PAIR_A, jnp.int32)[pair]).astype(jnp.int32)
    blk_b = (group * EXPERTS_PER_GROUP + jnp.asarray(PAIR_B, jnp.int32)[pair]).astype(jnp.int32)
    valid_end = (pad_end - padded + cnt).astype(jnp.int32)
    n_used = (pad_end[N_CLASSES - 1:N_CLASSES] // blk).astype(jnp.int32)
    y = _experts_call(haug, dest.reshape(-1), blk_a, blk_b, valid_end, pad_end.astype(jnp.int32), n_used,
                      w1, w3, w2, layer, blk)
    return _combine_call(haug, y, ln_g, ln_b, alpha)


def kernel(x, positions, ln_in_g, ln_in_b, w_in, conv_w, sgu_ln_g, sgu_ln_b, w_s, b_s, lambda_q1, lambda_k1, lambda_q2, lambda_k2, diff_subln_g, w_gate, b_gate, w_branch, w_o, ln1_g, ln1_b, w_router, b_router, w1, w3, w2, ln2_g, ln2_b):
    bsz, seq, d = x.shape
    depth = w_in.shape[0]
    alpha = (2.0 * depth) ** 0.25
    t = bsz * seq
    cos_t, sin_t = _rope_tables(positions)
    w_in_b, w_gate_b, w_branch_b, w_o_b = (w.astype(BF16) for w in (w_in, w_gate, w_branch, w_o))
    w1_b, w3_b, w2_b = (w.astype(BF16) for w in (w1, w3, w2))
    h = x.reshape(t, d)
    for l in range(depth):
        lambda_init = 0.8 - 0.6 * math.exp(-0.3 * l)
        if l == 0:
            h, z = _proj_call(h, w_in_b, l, ln=(ln_in_g, ln_in_b))
        else:
            z = _proj_call(h, w_in_b, l)
        y_conv = _conv_call(z, conv_w[l], bsz, seq)
        y_ret = _ret_call(z, cos_t, sin_t, bsz, seq)
        y_sgu = _sgu_call(z, sgu_ln_g[l], sgu_ln_b[l], w_s[l], b_s[l])
        qt, kr, vt = _qkprep_call(z, cos_t, sin_t, bsz, seq)
        lam_params = jnp.stack([lambda_q1[l], lambda_k1[l], lambda_q2[l], lambda_k2[l]])
        y_diff = _flash_call(qt, kr, vt, lam_params, diff_subln_g[l], bsz, seq, lambda_init)
        haug, cls = _merge_call(h, (y_conv, y_ret, y_sgu, y_diff), w_gate_b, b_gate[l], w_branch_b, w_o_b, l,
                                ln1_g[l], ln1_b[l], w_router, b_router, alpha)
        h = _moe_layer(haug, cls, w1_b, w3_b, w2_b, l, ln2_g[l], ln2_b[l], alpha)
    return h.reshape(bsz, seq, d)
```
